```python
import math
import jax, jax.numpy as jnp
from jax import lax
import numpy as np

D_MODEL = 1024
BATCH = 16
SEQ = 4096
DEPTH = 2

GRID_W = 64
CTX_LEN = 256
HEAD_DIM = 64
N_Q_HEADS = 8
N_KV_HEADS = 2
GQA_GROUP = N_Q_HEADS // N_KV_HEADS
ATTN_WIDTH = N_Q_HEADS * HEAD_DIM
KV_WIDTH = N_KV_HEADS * HEAD_DIM
RNN_WIDTH = 512
RNN_BLOCKS = 8
RNN_BLOCK_DIM = RNN_WIDTH // RNN_BLOCKS
CONV_WIDTH = 4
LRU_C = 8.0
WINDOW = 128
Q_BLOCK = 128
ROPE_BASE = 10000.0
N_BRANCHES = 3
BRANCH_WIDTH = 512
N_IN = 2 * RNN_WIDTH + 2 * (ATTN_WIDTH + 2 * KV_WIDTH) + N_BRANCHES * D_MODEL
N_EXPERTS = 16
N_GROUPS = 4
EXPERTS_PER_GROUP = N_EXPERTS // N_GROUPS
TOP_K = 2
D_EXPERT = 1024
EPS = 1e-6
NEG_INF = -1e30

kernel_name = "hybrid_rglru_gqa_window_moe_diffusion"


def rms_norm(x, g):
    xf = x.astype(jnp.float32)
    y = xf * lax.rsqrt(jnp.mean(xf * xf, axis=-1, keepdims=True) + EPS)
    return (y * g.astype(jnp.float32)).astype(x.dtype)


def modulate(x, g, shift, scale):
    return rms_norm(x, g) * (1 + scale) + shift


def rope_tables(rows):
    row = jnp.repeat(jnp.arange(rows, dtype=jnp.float32), GRID_W)
    col = jnp.tile(jnp.arange(GRID_W, dtype=jnp.float32), rows)
    n_freq = HEAD_DIM // 4
    inv = ROPE_BASE ** (-jnp.arange(n_freq, dtype=jnp.float32) / n_freq)
    ang = jnp.concatenate([row[:, None] * inv, col[:, None] * inv], axis=-1)
    return jnp.cos(ang), jnp.sin(ang)


def apply_rope(t, cos, sin):
    half = HEAD_DIM // 2
    shape = (cos.shape[0],) + (1,) * (t.ndim - 3) + (half,)
    cos = cos.reshape(shape).astype(t.dtype)
    sin = sin.reshape(shape).astype(t.dtype)
    t1, t2 = t[..., :half], t[..., half:]
    return jnp.concatenate([t1 * cos - t2 * sin, t2 * cos + t1 * sin], axis=-1)


def attend(q, k, v, mask=None, sink=None):
    s = jnp.einsum('bqhgd,bkhd->bhgqk', q, k).astype(jnp.float32) * (HEAD_DIM ** -0.5)
    if mask is not None:
        s = jnp.where(mask, s, NEG_INF)
    if sink is not None:
        sk = jnp.broadcast_to(sink.astype(jnp.float32).reshape(1, N_KV_HEADS, GQA_GROUP, 1, 1), s.shape[:-1] + (1,))
        p = jax.nn.softmax(jnp.concatenate([s, sk], axis=-1), axis=-1)[..., :-1]
    else:
        p = jax.nn.softmax(s, axis=-1)
    return jnp.einsum('bhgqk,bkhd->bqhgd', p.astype(v.dtype), v)


def global_attention_latent(q, k_lat, v_lat, k_ctx, v_ctx):
    B, S = q.shape[:2]
    k = jnp.concatenate([k_ctx, k_lat], axis=1)
    v = jnp.concatenate([v_ctx, v_lat], axis=1)
    nb = S // Q_BLOCK
    qb = q.reshape(B, nb, Q_BLOCK, N_KV_HEADS, GQA_GROUP, HEAD_DIM).swapaxes(0, 1)
    ob = lax.map(lambda qi: attend(qi, k, v), qb)
    return ob.swapaxes(0, 1).reshape(B, S, ATTN_WIDTH)


def window_attention_latent(q, k_lat, v_lat, k_ctx, v_ctx, sink):
    B, S = q.shape[:2]
    nb = S // Q_BLOCK
    band = Q_BLOCK + 2 * WINDOW
    pad = ((0, 0), (WINDOW, WINDOW), (0, 0), (0, 0))
    kp = jnp.pad(k_lat, pad)
    vp = jnp.pad(v_lat, pad)
    qq = jnp.arange(Q_BLOCK)[:, None]
    kk = jnp.arange(band)[None, :]
    in_win = jnp.abs(kk - WINDOW - qq) <= WINDOW
    ctx_mask = jnp.ones((Q_BLOCK, k_ctx.shape[1]), dtype=bool)
    qb = q.reshape(B, nb, Q_BLOCK, N_KV_HEADS, GQA_GROUP, HEAD_DIM).swapaxes(0, 1)

    def blk(args):
        qi, bi = args
        start = bi * Q_BLOCK
        kw = lax.dynamic_slice_in_dim(kp, start, band, axis=1)
        vw = lax.dynamic_slice_in_dim(vp, start, band, axis=1)
        key_pos = start - WINDOW + kk
        valid = in_win & (key_pos >= 0) & (key_pos < S)
        mask = jnp.concatenate([ctx_mask, valid], axis=1)
        return attend(qi, jnp.concatenate([k_ctx, kw], axis=1), jnp.concatenate([v_ctx, vw], axis=1), mask, sink)

    ob = lax.map(blk, (qb, jnp.arange(nb)))
    return ob.swapaxes(0, 1).reshape(B, S, ATTN_WIDTH)


def centred_dwconv(x, w, b):
    left = CONV_WIDTH // 2
    right = CONV_WIDTH - 1 - left
    T = x.shape[1]
    xp = jnp.pad(x, ((0, 0), (left, right), (0, 0)))
    return sum(xp[:, k:k + T] * w[k] for k in range(CONV_WIDTH)) + b


def _lin_combine(e1, e2):
    a1, b1 = e1
    a2, b2 = e2
    return a1 * a2, a2 * b1 + b2


def rglru(xr, wa, ba, wx, bx, lam, h0, reverse):
    B, T, _ = xr.shape
    xf = xr.astype(jnp.float32)
    xb = xf.reshape(B, T, RNN_BLOCKS, RNN_BLOCK_DIM)
    r = jax.nn.sigmoid(jnp.einsum('btnd,nde->btne', xb, wa.astype(jnp.float32)).reshape(B, T, RNN_WIDTH) + ba)
    i = jax.nn.sigmoid(jnp.einsum('btnd,nde->btne', xb, wx.astype(jnp.float32)).reshape(B, T, RNN_WIDTH) + bx)
    log_a = -LRU_C * r * jax.nn.softplus(-lam.astype(jnp.float32))
    a = jnp.exp(log_a)
    b = jnp.sqrt(-jnp.expm1(2.0 * log_a)) * (i * xf)
    a_cum, h = lax.associative_scan(_lin_combine, (a, b), axis=1, reverse=reverse)
    h = h + a_cum * h0[:, None, :]
    final = h[:, 0] if reverse else h[:, -1]
    return h, final


def _split_columns(p):
    sizes = (RNN_WIDTH, RNN_WIDTH, ATTN_WIDTH, KV_WIDTH, KV_WIDTH, ATTN_WIDTH, KV_WIDTH, KV_WIDTH, N_BRANCHES * D_MODEL)
    out, start = [], 0
    for n in sizes:
        out.append(p[..., start:start + n])
        start += n
    return out


def project_stream(h, w_in, b_merge, conv_w, conv_b, q_g, k_g, rope):
    B, T, _ = h.shape
    xr, gr, gq, gk, gv, wq, wk, wv, mg = _split_columns(h @ w_in)
    xr = centred_dwconv(xr, conv_w, conv_b)
    gr = jax.nn.gelu(gr)
    qs = (B, T, N_KV_HEADS, GQA_GROUP, HEAD_DIM)
    ks = (B, T, N_KV_HEADS, HEAD_DIM)
    gq = rms_norm(gq.reshape(qs), q_g[0])
    gk = rms_norm(gk.reshape(ks), k_g[0])
    gv = gv.reshape(ks)
    wq = rms_norm(wq.reshape(qs), q_g[1])
    wk = rms_norm(wk.reshape(ks), k_g[1])
    wv = wv.reshape(ks)
    if rope is not None:
        cos, sin = rope
        gq, gk, wq, wk = [apply_rope(t, cos, sin) for t in (gq, gk, wq, wk)]
    mg = jax.nn.sigmoid((mg + b_merge).astype(jnp.float32)).astype(h.dtype).reshape(B, T, N_BRANCHES, D_MODEL)
    return xr, gr, gq, gk, gv, wq, wk, wv, mg


def merge_branches(ys, mg, w_branch, w_out):
    merged = sum(mg[:, :, b] * (ys[b] @ w_branch[b]) for b in range(N_BRANCHES))
    return merged @ w_out


def moe(h, router_w, router_b, w1, w3, w2):
    B, T, D = h.shape
    t = h.reshape(-1, D)
    s = jax.nn.sigmoid(t.astype(jnp.float32) @ router_w.astype(jnp.float32))
    sel = s + router_b.astype(jnp.float32)
    grp_score = lax.top_k(sel.reshape(-1, N_GROUPS, EXPERTS_PER_GROUP), TOP_K)[0].sum(-1)
    best = jnp.argmax(grp_score, axis=-1)
    expert_group = jnp.arange(N_EXPERTS) // EXPERTS_PER_GROUP
    emask = expert_group[None, :] == best[:, None]
    _, idx = lax.top_k(jnp.where(emask, sel, NEG_INF), TOP_K)
    w = jnp.take_along_axis(s, idx, axis=-1)
    w = w / jnp.sum(w, axis=-1, keepdims=True)
    combine = jnp.sum(jax.nn.one_hot(idx, N_EXPERTS, dtype=jnp.float32) * w[..., None], axis=1)
    out = jnp.zeros(t.shape, jnp.float32)
    for e in range(N_EXPERTS):
        hid = jax.nn.silu(t @ w1[e]) * (t @ w3[e])
        out = out + combine[:, e:e + 1] * (hid @ w2[e]).astype(jnp.float32)
    return out.astype(h.dtype).reshape(B, T, D)


def setup_inputs(seed: int = 0) -> dict:
    key = jax.random.key(seed)
    ks = jax.random.split(key, 32)
    f32 = jnp.float32
    L, D = DEPTH, D_MODEL

    def nrm(k, shape, scale):
        return jax.random.normal(k, shape, f32) * scale

    a0 = jax.random.uniform(ks[16], (L, 2, RNN_WIDTH), f32, 0.9, 0.999)
    p = a0 ** (1.0 / LRU_C)
    lam = jnp.log(p) - jnp.log1p(-p)
    return {
        "x": nrm(ks[0], (BATCH, SEQ, D), 1.0),
        "c": nrm(ks[1], (BATCH, D), 1.0),
        "ctx": nrm(ks[2], (BATCH, CTX_LEN, D), 1.0),
        "c_ctx": nrm(ks[3], (D,), 1.0),
        "w_mod": nrm(ks[4], (L, D, 6 * D), D ** -0.5),
        "b_mod": nrm(ks[5], (L, 6 * D), 0.02),
        "norm1_g": 1.0 + nrm(ks[6], (L, D), 0.02),
        "norm2_g": 1.0 + nrm(ks[7], (L, D), 0.02),
        "w_in": nrm(ks[8], (L, D, N_IN), D ** -0.5),
        "b_merge": nrm(ks[9], (L, N_BRANCHES * D), 0.02),
        "conv_w": nrm(ks[10], (L, CONV_WIDTH, RNN_WIDTH), CONV_WIDTH ** -0.5),
        "conv_b": nrm(ks[11], (L, RNN_WIDTH), 0.02),
        "lru_wa": nrm(ks[12], (L, 2, RNN_BLOCKS, RNN_BLOCK_DIM, RNN_BLOCK_DIM), RNN_BLOCK_DIM ** -0.5),
        "lru_ba": nrm(ks[13], (L, 2, RNN_WIDTH), 0.02),
        "lru_wx": nrm(ks[14], (L, 2, RNN_BLOCKS, RNN_BLOCK_DIM, RNN_BLOCK_DIM), RNN_BLOCK_DIM ** -0.5),
        "lru_bx": nrm(ks[15], (L, 2, RNN_WIDTH), 0.02),
        "lru_lambda": lam,
        "q_norm_g": 1.0 + nrm(ks[17], (L, 2, HEAD_DIM), 0.02),
        "k_norm_g": 1.0 + nrm(ks[18], (L, 2, HEAD_DIM), 0.02),
        "sink": nrm(ks[19], (L, N_Q_HEADS), 1.0),
        "w_branch": nrm(ks[20], (L, N_BRANCHES, BRANCH_WIDTH, D), BRANCH_WIDTH ** -0.5),
        "w_out": nrm(ks[21], (L, D, D), D ** -0.5),
        "router_w": nrm(ks[22], (D, N_EXPERTS), D ** -0.5),
        "router_b": nrm(ks[23], (N_EXPERTS,), 0.01),
        "expert_w1": nrm(ks[24], (L, N_EXPERTS, D, D_EXPERT), D ** -0.5),
        "expert_w3": nrm(ks[25], (L, N_EXPERTS, D, D_EXPERT), D ** -0.5),
        "expert_w2": nrm(ks[26], (L, N_EXPERTS, D_EXPERT, D), D_EXPERT ** -0.5),
    }


def reference(x, c, ctx, c_ctx, w_mod, b_mod, norm1_g, norm2_g, w_in, b_merge, conv_w, conv_b,
              lru_wa, lru_ba, lru_wx, lru_bx, lru_lambda, q_norm_g, k_norm_g, sink, w_branch, w_out,
              router_w, router_b, expert_w1, expert_w3, expert_w2):
    B, S, _ = x.shape
    rows = S // GRID_W
    rope = rope_tables(rows)
    sc = jax.nn.silu(c)
    scc = jax.nn.silu(c_ctx)
    xc = ctx
    zero_state = jnp.zeros((B, RNN_WIDTH), jnp.float32)
    for l in range(DEPTH):
        last = l == DEPTH - 1
        sh1, sc1, g1, sh2, sc2, g2 = jnp.split((sc @ w_mod[l] + b_mod[l])[:, None, :], 6, axis=-1)
        sh1c, sc1c, g1c, sh2c, sc2c, g2c = jnp.split(scc @ w_mod[l] + b_mod[l], 6, axis=-1)

        h = modulate(x, norm1_g[l], sh1, sc1)
        hc = modulate(xc, norm1_g[l], sh1c, sc1c)
        xr_c, gr_c, gq_c, gk_c, gv_c, wq_c, wk_c, wv_c, mg_c = project_stream(
            hc, w_in[l], b_merge[l], conv_w[l], conv_b[l], q_norm_g[l], k_norm_g[l], None)
        xr, gr, gq, gk, gv, wq, wk, wv, mg = project_stream(
            h, w_in[l], b_merge[l], conv_w[l], conv_b[l], q_norm_g[l], k_norm_g[l], rope)

        hf_c, s_fwd = rglru(xr_c, lru_wa[l, 0], lru_ba[l, 0], lru_wx[l, 0], lru_bx[l, 0], lru_lambda[l, 0], zero_state, False)
        hb_c, s_bwd = rglru(xr_c, lru_wa[l, 1], lru_ba[l, 1], lru_wx[l, 1], lru_bx[l, 1], lru_lambda[l, 1], zero_state, True)
        hf, _ = rglru(xr, lru_wa[l, 0], lru_ba[l, 0], lru_wx[l, 0], lru_bx[l, 0], lru_lambda[l, 0], s_fwd, False)
        hb, _ = rglru(xr, lru_wa[l, 1], lru_ba[l, 1], lru_wx[l, 1], lru_bx[l, 1], lru_lambda[l, 1], s_bwd, True)
        y_rnn = (hf + hb).astype(x.dtype) * gr
        y_glob = global_attention_latent(gq, gk, gv, gk_c, gv_c)
        y_win = window_attention_latent(wq, wk, wv, wk_c, wv_c, sink[l])
        x = x + g1 * merge_branches((y_rnn, y_glob, y_win), mg, w_branch[l], w_out[l])

        if not last:
            n_ctx = xc.shape[1]
            y_rnn_c = (hf_c + hb_c).astype(xc.dtype) * gr_c
            y_glob_c = attend(gq_c, gk_c, gv_c).reshape(B, n_ctx, ATTN_WIDTH)
            y_win_c = attend(wq_c, wk_c, wv_c, sink=sink[l]).reshape(B, n_ctx, ATTN_WIDTH)
            xc = xc + g1c * merge_branches((y_rnn_c, y_glob_c, y_win_c), mg_c, w_branch[l], w_out[l])
            h2 = modulate(x, norm2_g[l], sh2, sc2)
            h2c = modulate(xc, norm2_g[l], sh2c, sc2c)
            y_all = moe(jnp.concatenate([h2c, h2], axis=1), router_w, router_b,
                        expert_w1[l], expert_w3[l], expert_w2[l])
            xc = xc + g2c * y_all[:, :n_ctx]
            x = x + g2 * y_all[:, n_ctx:]
        else:
            h2 = modulate(x, norm2_g[l], sh2, sc2)
            x = x + g2 * moe(h2, router_w, router_b, expert_w1[l], expert_w3[l], expert_w2[l])
    return x
```

```python
import functools

import jax
import jax.numpy as jnp
from jax import lax
from jax.experimental import pallas as pl
from jax.experimental.pallas import tpu as pltpu

F32 = jnp.float32
BF16 = jnp.bfloat16
I32 = jnp.int32

HEAD_DIM = 64
N_Q_HEADS = 8
N_KV_HEADS = 2
GQA_GROUP = N_Q_HEADS // N_KV_HEADS
ATTN_WIDTH = N_Q_HEADS * HEAD_DIM
KV_WIDTH = N_KV_HEADS * HEAD_DIM
RNN_WIDTH = 512
RNN_BLOCKS = 8
CONV_WIDTH = 4
LRU_C = 8.0
WINDOW = 128
GRID_W = 64
ROPE_BASE = 10000.0
N_BRANCHES = 3
N_EXPERTS = 16
N_GROUPS = 4
EXPERTS_PER_GROUP = 4
EPS = 1e-6
NEG_INF = -1e30

PAIR_SLOT_A = (0, 0, 0, 1, 1, 3)
PAIR_SLOT_B = (1, 2, 3, 3, 2, 2)
N_PAIRS = 6
N_BUCKETS = N_GROUPS * N_PAIRS
BUCKET_ROWS = 32

LANES = 128
VMEM_LIMIT = 56 * 1024 * 1024

HIGHEST = lax.Precision.HIGHEST


def _cparams(sem):
    return pltpu.CompilerParams(dimension_semantics=sem, vmem_limit_bytes=VMEM_LIMIT)


def _sigmoid(x):
    return 1.0 / (1.0 + jnp.exp(-x))


def _rms_mod(x, g, scale, shift):
    ms = jnp.mean(x * x, axis=-1, keepdims=True)
    return (x * lax.rsqrt(ms + EPS) * g) * (1.0 + scale) + shift


def _mod_kernel(a_ref, w_ref, b_ref, o_ref):
    a = a_ref[...]
    a = a * _sigmoid(a)
    o_ref[0] = jnp.dot(a, w_ref[0], precision=HIGHEST, preferred_element_type=F32) + b_ref[0]


def _modulation(cvec, w_mod, b_mod):
    rows, d = cvec.shape
    n_layers, _, n_out = w_mod.shape
    tn = 1536
    return pl.pallas_call(
        _mod_kernel,
        grid=(n_layers, n_out // tn),
        in_specs=[
            pl.BlockSpec((rows, d), lambda l, j: (0, 0)),
            pl.BlockSpec((1, d, tn), lambda l, j: (l, 0, j)),
            pl.BlockSpec((1, 1, tn), lambda l, j: (l, 0, j)),
        ],
        out_specs=pl.BlockSpec((1, rows, tn), lambda l, j: (l, 0, j)),
        out_shape=jax.ShapeDtypeStruct((n_layers, rows, n_out), F32),
        compiler_params=_cparams(("arbitrary", "arbitrary")),
        name="modulation",
    )(cvec, w_mod, b_mod.reshape(n_layers, 1, n_out))


def _head_norm_rope(p, gain, cos_t, sin_t, seg, lo32):
    ss = p * p
    hi = ss.astype(BF16)
    lo = (ss - hi.astype(F32)).astype(BF16)
    mean = jnp.dot(hi, seg, preferred_element_type=F32) + jnp.dot(lo, seg, preferred_element_type=F32)
    n = p * lax.rsqrt(mean + EPS) * gain
    partner = jnp.where(lo32, pltpu.roll(n, 96, 1), pltpu.roll(n, 32, 1))
    return n * cos_t + partner * sin_t


def _inproj_kernel(x_ref, sh_ref, sc_ref, g_ref, w_ref, bm_ref, qgain_ref, kgain_ref, cos_ref, sin_ref, seg_ref,
                   xr_o, gr_o, qg_o, kg_o, vg_o, qw_o, kw_o, vw_o, mg_o, *, n_ctx_tiles):
    i = pl.program_id(0)
    is_lat = i >= n_ctx_tiles
    h = _rms_mod(x_ref[...], g_ref[...], sc_ref[0], sh_ref[0])
    hb = h.astype(BF16)
    tm = hb.shape[0]

    def proj(c0, width):
        return jnp.dot(hb, w_ref[:, c0:c0 + width], preferred_element_type=F32)

    xr_o[...] = proj(0, RNN_WIDTH).astype(BF16)
    gr_o[...] = jax.nn.gelu(proj(RNN_WIDTH, RNN_WIDTH)).astype(BF16)

    lane = lax.broadcasted_iota(I32, (tm, LANES), 1)
    lo32 = (lane & (HEAD_DIM - 1)) < (HEAD_DIM // 2)
    lo64 = lane < HEAD_DIM
    cos_t = jnp.where(is_lat, cos_ref[...], 1.0)
    sin_t = jnp.where(is_lat, sin_ref[...], 0.0)
    seg = seg_ref[...]
    zero = jnp.zeros((tm, LANES), F32)

    base = 2 * RNN_WIDTH
    for a, (q_o, k_o, v_o) in enumerate(((qg_o, kg_o, vg_o), (qw_o, kw_o, vw_o))):
        qgain = qgain_ref[a:a + 1, :]
        kgain = kgain_ref[a:a + 1, :]
        for c in range(ATTN_WIDTH // LANES):
            p = proj(base + c * LANES, LANES)
            y = _head_norm_rope(p, qgain, cos_t, sin_t, seg, lo32) * (HEAD_DIM ** -0.5)
            yr = pltpu.roll(y, HEAD_DIM, 1)
            if c < 2:
                out_a, out_b = jnp.where(lo64, y, zero), jnp.where(lo64, yr, zero)
            else:
                out_a, out_b = jnp.where(lo64, zero, yr), jnp.where(lo64, zero, y)
            q_o[2 * c] = out_a.astype(BF16)
            q_o[2 * c + 1] = out_b.astype(BF16)
        pk = proj(base + ATTN_WIDTH, KV_WIDTH)
        k_o[...] = _head_norm_rope(pk, kgain, cos_t, sin_t, seg, lo32).astype(BF16)
        v_o[...] = proj(base + ATTN_WIDTH + KV_WIDTH, KV_WIDTH).astype(BF16)
        base += ATTN_WIDTH + 2 * KV_WIDTH

    d = x_ref.shape[1]
    for b in range(N_BRANCHES):
        p = proj(base + b * d, d) + bm_ref[:, b * d:(b + 1) * d]
        mg_o[:, b * d:(b + 1) * d] = _sigmoid(p).astype(BF16)


def _input_projection(x_all, mod_sh, mod_sc, g, w_in, b_merge, qgain, kgain, cos_t, sin_t, seg, *, n_ctx, seq, tm):
    ntok, d = x_all.shape
    n_in = w_in.shape[1]
    nct = n_ctx // tm
    tps = seq // tm

    def mod_idx(i):
        return (jnp.where(i >= nct, 1 + (i - nct) // tps, 0), 0, 0)

    def rope_idx(i):
        return (jnp.where(i >= nct, (i - nct) % tps, 0), 0)

    row = lambda i: (i, 0)
    fixed = lambda i: (0, 0)
    out_shape = (
        jax.ShapeDtypeStruct((ntok, RNN_WIDTH), BF16),
        jax.ShapeDtypeStruct((ntok, RNN_WIDTH), BF16),
        jax.ShapeDtypeStruct((N_Q_HEADS, ntok, LANES), BF16),
        jax.ShapeDtypeStruct((ntok, KV_WIDTH), BF16),
        jax.ShapeDtypeStruct((ntok, KV_WIDTH), BF16),
        jax.ShapeDtypeStruct((N_Q_HEADS, ntok, LANES), BF16),
        jax.ShapeDtypeStruct((ntok, KV_WIDTH), BF16),
        jax.ShapeDtypeStruct((ntok, KV_WIDTH), BF16),
        jax.ShapeDtypeStruct((ntok, N_BRANCHES * d), BF16),
    )
    q_spec = pl.BlockSpec((N_Q_HEADS, tm, LANES), lambda i: (0, i, 0))
    out_specs = (
        pl.BlockSpec((tm, RNN_WIDTH), row), pl.BlockSpec((tm, RNN_WIDTH), row),
        q_spec, pl.BlockSpec((tm, KV_WIDTH), row), pl.BlockSpec((tm, KV_WIDTH), row),
        q_spec, pl.BlockSpec((tm, KV_WIDTH), row), pl.BlockSpec((tm, KV_WIDTH), row),
        pl.BlockSpec((tm, N_BRANCHES * d), row),
    )
    return pl.pallas_call(
        functools.partial(_inproj_kernel, n_ctx_tiles=nct),
        grid=(ntok // tm,),
        in_specs=[
            pl.BlockSpec((tm, d), row),
            pl.BlockSpec((1, 1, d), mod_idx),
            pl.BlockSpec((1, 1, d), mod_idx),
            pl.BlockSpec((1, d), fixed),
            pl.BlockSpec((d, n_in), fixed),
            pl.BlockSpec((1, N_BRANCHES * d), fixed),
            pl.BlockSpec((2, LANES), fixed),
            pl.BlockSpec((2, LANES), fixed),
            pl.BlockSpec((tm, LANES), rope_idx),
            pl.BlockSpec((tm, LANES), rope_idx),
            pl.BlockSpec((LANES, LANES), fixed),
        ],
        out_specs=out_specs,
        out_shape=out_shape,
        compiler_params=_cparams(("arbitrary",)),
        name="input_projection",
    )(x_all, mod_sh, mod_sc, g, w_in, b_merge, qgain, kgain, cos_t, sin_t, seg)


def _rglru_kernel(x_ref, prev_ref, next_ref, cw_ref, cb_ref, wa_ref, wx_ref, ba_ref, bx_ref, lam_ref,
                  o_ref, a_scr, b_scr, h_scr, y_scr, *, n_chunks):
    d = pl.program_id(0)
    j = pl.program_id(2)
    jj = jnp.where(d == 0, j, jnp.where(j == 0, 0, n_chunks - j))
    prev_valid = jj >= 2
    next_valid = (jj >= 1) & (jj <= n_chunks - 2)

    x = x_ref[...].astype(F32)
    tc = x.shape[0]
    hp = prev_ref[...].astype(F32)
    hn = next_ref[...].astype(F32)
    halo = prev_ref.shape[0]
    pm1 = jnp.where(prev_valid, hp[halo - 1:halo, :], 0.0)
    pm2 = jnp.where(prev_valid, hp[halo - 2:halo - 1, :], 0.0)
    nx0 = jnp.where(next_valid, hn[0:1, :], 0.0)
    row = lax.broadcasted_iota(I32, x.shape, 0)
    xm1 = jnp.where(row == 0, pm1, pltpu.roll(x, 1, 0))
    xm2 = jnp.where(row == 0, pm2, jnp.where(row == 1, pm1, pltpu.roll(x, 2, 0)))
    xp1 = jnp.where(row == tc - 1, nx0, pltpu.roll(x, tc - 1, 0))
    xc = (xm2 * cw_ref[0:1, :] + xm1 * cw_ref[1:2, :] + x * cw_ref[2:3, :] + xp1 * cw_ref[3:4, :]) + cb_ref[...]

    xb = xc.astype(BF16)
    r = _sigmoid(jnp.dot(xb, wa_ref[0], preferred_element_type=F32) + ba_ref[0])
    g = _sigmoid(jnp.dot(xb, wx_ref[0], preferred_element_type=F32) + bx_ref[0])
    neg_lam = -lam_ref[0]
    softplus = jnp.maximum(neg_lam, 0.0) + jnp.log(1.0 + jnp.exp(-jnp.abs(neg_lam)))
    a = jnp.exp(-LRU_C * r * softplus)
    a_scr[...] = a
    b_scr[...] = jnp.sqrt(1.0 - a * a) * (g * xc)

    @pl.when(j == 0)
    def _():
        h_scr[...] = jnp.zeros_like(h_scr)

    def step(t, h):
        rr = jnp.where(d == 0, t, tc - 1 - t)
        h = a_scr[pl.ds(rr, 1), :] * h + b_scr[pl.ds(rr, 1), :]
        y_scr[pl.ds(rr, 1), :] = h
        return h

    h_scr[...] = lax.fori_loop(0, tc, step, h_scr[...], unroll=8)
    o_ref[0] = y_scr[...].astype(BF16)


def _rglru(xr, conv_w, conv_b, wa, wx, ba, bx, lam, *, batch, n_ctx_per, seq):
    ntok, width = xr.shape
    tc = n_ctx_per
    halo = 16
    n_chunks = 1 + seq // tc
    lat0 = batch

    def seq_chunk(d, j):
        return jnp.where(d == 0, j, jnp.where(j == 0, 0, n_chunks - j))

    def chunk_blk(d, b, j):
        jj = seq_chunk(d, j)
        return jnp.where(jj == 0, b, lat0 + b * (seq // tc) + jj - 1)

    def x_idx(d, b, j):
        return (chunk_blk(d, b, j), 0)

    def prev_idx(d, b, j):
        return (jnp.maximum(chunk_blk(d, b, j) * (tc // halo) - 1, 0), 0)

    def next_idx(d, b, j):
        return (jnp.minimum((chunk_blk(d, b, j) + 1) * (tc // halo), ntok // halo - 1), 0)

    per_dir = lambda d, b, j: (d, 0, 0)
    fixed = lambda d, b, j: (0, 0)
    return pl.pallas_call(
        functools.partial(_rglru_kernel, n_chunks=n_chunks),
        grid=(2, batch, n_chunks),
        in_specs=[
            pl.BlockSpec((tc, width), x_idx),
            pl.BlockSpec((halo, width), prev_idx),
            pl.BlockSpec((halo, width), next_idx),
            pl.BlockSpec((CONV_WIDTH, width), fixed),
            pl.BlockSpec((1, width), fixed),
            pl.BlockSpec((1, width, width), per_dir),
            pl.BlockSpec((1, width, width), per_dir),
            pl.BlockSpec((1, 1, width), per_dir),
            pl.BlockSpec((1, 1, width), per_dir),
            pl.BlockSpec((1, 1, width), per_dir),
        ],
        out_specs=pl.BlockSpec((1, tc, width), lambda d, b, j: (d, chunk_blk(d, b, j), 0)),
        out_shape=jax.ShapeDtypeStruct((2, ntok, width), BF16),
        scratch_shapes=[
            pltpu.VMEM((tc, width), F32), pltpu.VMEM((tc, width), F32),
            pltpu.VMEM((1, width), F32), pltpu.VMEM((tc, width), F32),
        ],
        compiler_params=_cparams(("arbitrary", "arbitrary", "arbitrary")),
        name="rglru",
    )(xr, xr, xr, conv_w, conv_b, wa, wx, ba, bx, lam)


def _attn_kernel(sink_ref, q_ref, kc_ref, vc_ref, kl_ref, vl_ref, o_ref, *,
                 band, use_sink, tile_off, n_ctx_tiles, tiles_per_seq, tk):
    i = pl.program_id(0) + tile_off
    is_lat = i >= n_ctx_tiles
    qi = jnp.where(is_lat, (i - n_ctx_tiles) % tiles_per_seq, 0)
    tq = q_ref.shape[1]
    rows = GQA_GROUP * tq
    seq = kl_ref.shape[0]
    contract_last = (((1,), (1,)), ((), ()))

    def update(carry, k, v, mask):
        m, l, acc = carry
        s = lax.dot_general(q, k, contract_last, preferred_element_type=F32)
        if mask is not None:
            s = jnp.where(mask, s, NEG_INF)
        m_new = jnp.maximum(m, jnp.max(s, axis=-1, keepdims=True))
        alpha = jnp.exp(m - m_new)
        p = jnp.exp(s - m_new)
        l = alpha * l + jnp.sum(p, axis=-1, keepdims=True)
        acc = alpha * acc + jnp.dot(p.astype(BF16), v, preferred_element_type=F32)
        return m_new, l, acc

    lane = lax.broadcasted_iota(I32, (tq, LANES), 1)
    lo64 = lane < HEAD_DIM
    for h in range(N_KV_HEADS):
        q = q_ref[GQA_GROUP * h:GQA_GROUP * (h + 1)].reshape(rows, LANES)
        if use_sink:
            m0 = jnp.concatenate(
                [jnp.full((tq, 1), sink_ref[GQA_GROUP * h + g], F32) for g in range(GQA_GROUP)], axis=0)
            l0 = jnp.ones((rows, 1), F32)
        else:
            m0 = jnp.full((rows, 1), NEG_INF, F32)
            l0 = jnp.zeros((rows, 1), F32)
        carry = (m0, l0, jnp.zeros((rows, LANES), F32))
        carry = update(carry, kc_ref[...], vc_ref[...], None)
        if band:
            nq = tiles_per_seq
            s_prev = pl.multiple_of(jnp.maximum(qi - 1, 0) * tq, tq)
            s_cur = pl.multiple_of(qi * tq, tq)
            s_next = pl.multiple_of(jnp.minimum(qi + 1, nq - 1) * tq, tq)
            k = jnp.concatenate([kl_ref[pl.ds(s_prev, tq), :], kl_ref[pl.ds(s_cur, tq), :],
                                 kl_ref[pl.ds(s_next, tq), :]], axis=0)
            v = jnp.concatenate([vl_ref[pl.ds(s_prev, tq), :], vl_ref[pl.ds(s_cur, tq), :],
                                 vl_ref[pl.ds(s_next, tq), :]], axis=0)
            qrow = lax.broadcasted_iota(I32, (rows, 3 * tq), 0) & (tq - 1)
            kcol = lax.broadcasted_iota(I32, (rows, 3 * tq), 1)
            rel = kcol - tq - qrow
            kmin = jnp.where(qi > 0, 0, tq)
            kmax = jnp.where(is_lat, jnp.where(qi < nq - 1, 3 * tq, 2 * tq), 0)
            ok = (rel >= -WINDOW) & (rel <= WINDOW) & (kcol >= kmin) & (kcol < kmax)
            carry = update(carry, k, v, ok)
        else:
            n_lat = jnp.where(is_lat, seq // tk, 0)

            def body(c, carry):
                s0 = pl.multiple_of(c * tk, tk)
                return update(carry, kl_ref[pl.ds(s0, tk), :], vl_ref[pl.ds(s0, tk), :], None)

            carry = lax.fori_loop(0, n_lat, body, carry)
        _, l, acc = carry
        o = acc * (1.0 / l)
        og = [o[g * tq:(g + 1) * tq] for g in range(GQA_GROUP)]
        for c in range(GQA_GROUP // 2):
            e, f = og[2 * c], og[2 * c + 1]
            if h == 0:
                chunk = jnp.where(lo64, e, pltpu.roll(f, HEAD_DIM, 1))
            else:
                chunk = jnp.where(lo64, pltpu.roll(e, HEAD_DIM, 1), f)
            col = (h * (GQA_GROUP // 2) + c) * LANES
            o_ref[:, col:col + LANES] = chunk.astype(BF16)


def _attention(sink, q, k, v, *, band, use_sink, batch, n_ctx_per, seq, with_ctx_queries, tq, tk):
    ntok = k.shape[0]
    n_ctx = batch * n_ctx_per
    assert n_ctx % seq == 0, "context rows must cover whole latent-sequence blocks"
    nct = n_ctx // tq
    tps = seq // tq
    tile_off = 0 if with_ctx_queries else nct
    n_tiles = ntok // tq - tile_off

    def batch_of(t):
        i = t + tile_off
        return jnp.where(i >= nct, (i - nct) // tps, (i * tq) // n_ctx_per)

    ctx_idx = lambda t, *_: (batch_of(t), 0)
    lat_idx = lambda t, *_: (n_ctx // seq + batch_of(t), 0)
    grid_spec = pltpu.PrefetchScalarGridSpec(
        num_scalar_prefetch=1,
        grid=(n_tiles,),
        in_specs=[
            pl.BlockSpec((N_Q_HEADS, tq, LANES), lambda t, *_: (0, t + tile_off, 0)),
            pl.BlockSpec((n_ctx_per, KV_WIDTH), ctx_idx),
            pl.BlockSpec((n_ctx_per, KV_WIDTH), ctx_idx),
            pl.BlockSpec((seq, KV_WIDTH), lat_idx),
            pl.BlockSpec((seq, KV_WIDTH), lat_idx),
        ],
        out_specs=pl.BlockSpec((tq, ATTN_WIDTH), lambda t, *_: (t, 0)),
    )
    return pl.pallas_call(
        functools.partial(_attn_kernel, band=band, use_sink=use_sink, tile_off=tile_off,
                          n_ctx_tiles=nct, tiles_per_seq=tps, tk=tk),
        grid_spec=grid_spec,
        out_shape=jax.ShapeDtypeStruct((n_tiles * tq, ATTN_WIDTH), BF16),
        compiler_params=_cparams(("arbitrary",)),
        name="window_attention" if band else "global_attention",
    )(sink, q, k, v, k, v)


def _merge_kernel(x_ref, h_ref, gr_ref, yg_ref, yw_ref, mg_ref, wb_ref, wo_ref, g1_ref, n2_ref, sh2_ref, sc2_ref,
                  rw_ref, x1_o, h2_o, lg_o):
    d = x_ref.shape[1]
    y_rnn = ((h_ref[0].astype(F32) + h_ref[1].astype(F32)) * gr_ref[...].astype(F32)).astype(BF16)
    ys = (y_rnn, yg_ref[...], yw_ref[...])
    merged = None
    for b in range(N_BRANCHES):
        t = mg_ref[:, b * d:(b + 1) * d].astype(F32) * jnp.dot(ys[b], wb_ref[b], preferred_element_type=F32)
        merged = t if merged is None else merged + t
    out = jnp.dot(merged.astype(BF16), wo_ref[...], preferred_element_type=F32)
    x1 = x_ref[...] + g1_ref[0] * out
    x1_o[...] = x1
    h2 = _rms_mod(x1, n2_ref[...], sc2_ref[0], sh2_ref[0])
    h2_o[...] = h2
    lg_o[...] = lax.dot_general(rw_ref[...], h2, (((1,), (1,)), ((), ())), precision=HIGHEST,
                                preferred_element_type=F32)


def _merge(x_all, hfb, gr, yg, yw, mg, wb, wo, mod_g1, n2, mod_sh2, mod_sc2, rw_t, *, n_ctx, seq, with_ctx, tm):
    ntok, d = x_all.shape
    nct = n_ctx // tm
    tps = seq // tm
    off = 0 if with_ctx else nct
    n_tiles = ntok // tm - off
    n_out = n_tiles * tm

    def mod_idx(t):
        i = t + off
        return (jnp.where(i >= nct, 1 + (i - nct) // tps, 0), 0, 0)

    row_in = lambda t: (t + off, 0)
    row_out = lambda t: (t, 0)
    fixed2 = lambda t: (0, 0)
    return pl.pallas_call(
        _merge_kernel,
        grid=(n_tiles,),
        in_specs=[
            pl.BlockSpec((tm, d), row_in),
            pl.BlockSpec((2, tm, RNN_WIDTH), lambda t: (0, t + off, 0)),
            pl.BlockSpec((tm, RNN_WIDTH), row_in),
            pl.BlockSpec((tm, ATTN_WIDTH), row_out),
            pl.BlockSpec((tm, ATTN_WIDTH), row_out),
            pl.BlockSpec((tm, N_BRANCHES * d), row_in),
            pl.BlockSpec((N_BRANCHES, RNN_WIDTH, d), lambda t: (0, 0, 0)),
            pl.BlockSpec((d, d), fixed2),
            pl.BlockSpec((1, 1, d), mod_idx),
            pl.BlockSpec((1, d), fixed2),
            pl.BlockSpec((1, 1, d), mod_idx),
            pl.BlockSpec((1, 1, d), mod_idx),
            pl.BlockSpec((N_EXPERTS, d), fixed2),
        ],
        out_specs=(
            pl.BlockSpec((tm, d), row_out),
            pl.BlockSpec((tm, d), row_out),
            pl.BlockSpec((N_EXPERTS, tm), lambda t: (0, t)),
        ),
        out_shape=(
            jax.ShapeDtypeStruct((n_out, d), F32),
            jax.ShapeDtypeStruct((n_out, d), F32),
            jax.ShapeDtypeStruct((N_EXPERTS, n_out), F32),
        ),
        compiler_params=_cparams(("arbitrary",)),
        name="merge",
    )(x_all, hfb, gr, yg, yw, mg, wb, wo, mod_g1, n2, mod_sh2, mod_sc2, rw_t)


def _first_argmax(vals):
    best, idx = vals[0], jnp.zeros(vals[0].shape, I32)
    for k in range(1, len(vals)):
        take = vals[k] > best
        best = jnp.where(take, vals[k], best)
        idx = jnp.where(take, k, idx)
    return best, idx


def _routing_kernel(lg_ref, rb_ref, tri_ref, bucket_o, rank_o, count_o, carry_scr):
    t = pl.program_id(0)

    @pl.when(t == 0)
    def _():
        carry_scr[...] = jnp.zeros_like(carry_scr)

    s = _sigmoid(lg_ref[...])
    sel = s + rb_ref[...]
    rows = [sel[e:e + 1, :] for e in range(N_EXPERTS)]
    tn = s.shape[1]

    grp_scores = []
    for g in range(N_GROUPS):
        r = rows[g * EXPERTS_PER_GROUP:(g + 1) * EXPERTS_PER_GROUP]
        best = None
        for a in range(EXPERTS_PER_GROUP):
            for b in range(a + 1, EXPERTS_PER_GROUP):
                pair = r[a] + r[b]
                best = pair if best is None else jnp.maximum(best, pair)
        grp_scores.append(best)
    _, grp = _first_argmax(grp_scores)

    vals = []
    for k in range(EXPERTS_PER_GROUP):
        v = rows[k]
        for g in range(1, N_GROUPS):
            v = jnp.where(grp == g, rows[g * EXPERTS_PER_GROUP + k], v)
        vals.append(v)
    _, i1 = _first_argmax(vals)
    _, i2 = _first_argmax([jnp.where(i1 == k, -jnp.inf, vals[k]) for k in range(EXPERTS_PER_GROUP)])
    lo = jnp.minimum(i1, i2)
    hi = jnp.maximum(i1, i2)
    pair = jnp.where(lo == 0, hi - 1, jnp.where(lo == 1, 6 - hi, 5))
    bucket = grp * N_PAIRS + pair
    bucket_o[...] = bucket

    onehot = (lax.broadcasted_iota(I32, (BUCKET_ROWS, tn), 0) == bucket).astype(F32)
    incl = jnp.dot(onehot.astype(BF16), tri_ref[...], preferred_element_type=F32)
    before = carry_scr[...] + incl - 1.0
    rank_o[...] = jnp.sum(onehot * before, axis=0, keepdims=True).astype(I32)
    carry_scr[...] = carry_scr[...] + jnp.sum(onehot, axis=1, keepdims=True)
    count_o[...] = jnp.broadcast_to(carry_scr[...], count_o.shape).astype(I32)


def _routing(logits_t, router_b, tri, *, tn):
    n = logits_t.shape[1]
    return pl.pallas_call(
        _routing_kernel,
        grid=(n // tn,),
        in_specs=[
            pl.BlockSpec((N_EXPERTS, tn), lambda t: (0, t)),
            pl.BlockSpec((N_EXPERTS, 1), lambda t: (0, 0)),
            pl.BlockSpec((tn, tn), lambda t: (0, 0)),
        ],
        out_specs=(
            pl.BlockSpec((1, tn), lambda t: (0, t)),
            pl.BlockSpec((1, tn), lambda t: (0, t)),
            pl.BlockSpec((BUCKET_ROWS, LANES), lambda t: (0, 0)),
        ),
        out_shape=(
            jax.ShapeDtypeStruct((1, n), I32),
            jax.ShapeDtypeStruct((1, n), I32),
            jax.ShapeDtypeStruct((BUCKET_ROWS, LANES), I32),
        ),
        scratch_shapes=[pltpu.VMEM((BUCKET_ROWS, 1), F32)],
        compiler_params=_cparams(("arbitrary",)),
        name="routing",
    )(logits_t, router_b, tri)


def _position_kernel(bucket_ref, rank_ref, base_ref, pos_o):
    bucket = bucket_ref[...]
    ids = lax.broadcasted_iota(I32, (BUCKET_ROWS, bucket.shape[1]), 0)
    base = jnp.sum(jnp.where(ids == bucket, base_ref[...], 0), axis=0, keepdims=True)
    pos_o[...] = base + rank_ref[...]


def _positions(bucket, rank, base, *, tn):
    n = bucket.shape[1]
    blk = pl.BlockSpec((1, tn), lambda t: (0, t))
    return pl.pallas_call(
        _position_kernel,
        grid=(n // tn,),
        in_specs=[blk, blk, pl.BlockSpec((BUCKET_ROWS, 1), lambda t: (0, 0))],
        out_specs=blk,
        out_shape=jax.ShapeDtypeStruct((1, n), I32),
        compiler_params=_cparams(("arbitrary",)),
        name="positions",
    )(bucket, rank, base)


def _row_copy_wait(src, dst, sem, n):
    def wait(r, _):
        pltpu.make_async_copy(src.at[pl.ds(0, 1), :], dst.at[pl.ds(0, 1), :], sem).wait()
        return 0

    lax.fori_loop(0, n, wait, 0, unroll=8)


def _dispatch_kernel(pos_ref, h_ref, init_ref, xs_ref, sem):
    del init_ref
    tm = h_ref.shape[0]

    def issue(r, _):
        p = pos_ref[0, 0, r]
        pltpu.make_async_copy(h_ref.at[pl.ds(r, 1), :], xs_ref.at[pl.ds(p, 1), :], sem).start()
        return 0

    lax.fori_loop(0, tm, issue, 0, unroll=8)
    _row_copy_wait(h_ref, xs_ref, sem, tm)


def _dispatch(pos3, h2, xs_init, *, tm):
    n, d = h2.shape
    return pl.pallas_call(
        _dispatch_kernel,
        grid=(n // tm,),
        in_specs=[
            pl.BlockSpec((1, 1, tm), lambda t: (t, 0, 0), memory_space=pltpu.SMEM),
            pl.BlockSpec((tm, d), lambda t: (t, 0)),
            pl.BlockSpec(memory_space=pl.ANY),
        ],
        out_specs=pl.BlockSpec(memory_space=pl.ANY),
        out_shape=jax.ShapeDtypeStruct(xs_init.shape, F32),
        scratch_shapes=[pltpu.SemaphoreType.DMA(())],
        input_output_aliases={2: 0},
        compiler_params=_cparams(("arbitrary",)),
        name="dispatch",
    )(pos3, h2, xs_init)


def _moe_kernel(ea_ref, eb_ref, act_ref, xs_ref, w1a, w3a, w2a, w1b, w3b, w2b, rwa, rwb, ys_ref):
    j = pl.program_id(0)

    @pl.when(act_ref[j] == 0)
    def _():
        ys_ref[...] = jnp.zeros_like(ys_ref)

    @pl.when(act_ref[j] != 0)
    def _():
        x = xs_ref[...]
        xb = x.astype(BF16)

        def ffn(w1, w3, w2):
            u = jnp.dot(xb, w1[0], preferred_element_type=F32)
            hid = (u * _sigmoid(u)) * jnp.dot(xb, w3[0], preferred_element_type=F32)
            return jnp.dot(hid.astype(BF16), w2[0], preferred_element_type=F32)

        sa = _sigmoid(jnp.sum(x * rwa[0], axis=-1, keepdims=True))
        sb = _sigmoid(jnp.sum(x * rwb[0], axis=-1, keepdims=True))
        inv = 1.0 / (sa + sb)
        ys_ref[...] = (sa * inv) * ffn(w1a, w3a, w2a) + (sb * inv) * ffn(w1b, w3b, w2b)


def _moe(ea, eb, act, xs, w1, w3, w2, rw3, *, tm):
    npad, d = xs.shape
    de = w1.shape[2]
    row = lambda j, *_: (j, 0)
    wa = lambda j, ea, eb, act: (ea[j], 0, 0)
    wb = lambda j, ea, eb, act: (eb[j], 0, 0)
    grid_spec = pltpu.PrefetchScalarGridSpec(
        num_scalar_prefetch=3,
        grid=(npad // tm,),
        in_specs=[
            pl.BlockSpec((tm, d), row),
            pl.BlockSpec((1, d, de), wa), pl.BlockSpec((1, d, de), wa), pl.BlockSpec((1, de, d), wa),
            pl.BlockSpec((1, d, de), wb), pl.BlockSpec((1, d, de), wb), pl.BlockSpec((1, de, d), wb),
            pl.BlockSpec((1, 1, d), wa), pl.BlockSpec((1, 1, d), wb),
        ],
        out_specs=pl.BlockSpec((tm, d), row),
    )
    return pl.pallas_call(
        _moe_kernel,
        grid_spec=grid_spec,
        out_shape=jax.ShapeDtypeStruct((npad, d), F32),
        compiler_params=_cparams(("arbitrary",)),
        name="expert_ffn",
    )(ea, eb, act, xs, w1, w3, w2, w1, w3, w2, rw3, rw3)


def _combine_kernel(pos_ref, x_ref, g2_ref, ys_ref, o_ref, buf, sem):
    tm = x_ref.shape[0]

    def issue(r, _):
        p = pos_ref[0, 0, r]
        pltpu.make_async_copy(ys_ref.at[pl.ds(p, 1), :], buf.at[pl.ds(r, 1), :], sem).start()
        return 0

    lax.fori_loop(0, tm, issue, 0, unroll=8)
    _row_copy_wait(ys_ref, buf, sem, tm)
    o_ref[...] = x_ref[...] + g2_ref[0] * buf[...]


def _combine(pos3, x1, mod_g2, ys, *, n_ctx, seq, with_ctx, tm):
    n, d = x1.shape
    nct = n_ctx // tm if with_ctx else 0
    tps = seq // tm

    def mod_idx(t):
        return (jnp.where(t >= nct, 1 + (t - nct) // tps, 0), 0, 0)

    return pl.pallas_call(
        _combine_kernel,
        grid=(n // tm,),
        in_specs=[
            pl.BlockSpec((1, 1, tm), lambda t: (t, 0, 0), memory_space=pltpu.SMEM),
            pl.BlockSpec((tm, d), lambda t: (t, 0)),
            pl.BlockSpec((1, 1, d), mod_idx),
            pl.BlockSpec(memory_space=pl.ANY),
        ],
        out_specs=pl.BlockSpec((tm, d), lambda t: (t, 0)),
        out_shape=jax.ShapeDtypeStruct((n, d), F32),
        scratch_shapes=[pltpu.VMEM((tm, d), F32), pltpu.SemaphoreType.DMA(())],
        compiler_params=_cparams(("arbitrary",)),
        name="combine",
    )(pos3, x1, mod_g2, ys)


def _block_diag(w):
    n, d, e = w.shape
    eye = jnp.eye(n, dtype=w.dtype)
    return (eye[:, None, :, None] * w[:, :, None, :]).reshape(n * d, n * e)


def _rope_tables(seq):
    rows = seq // GRID_W
    row = jnp.repeat(jnp.arange(rows, dtype=F32), GRID_W)
    col = jnp.tile(jnp.arange(GRID_W, dtype=F32), rows)
    n_freq = HEAD_DIM // 4
    inv = ROPE_BASE ** (-jnp.arange(n_freq, dtype=F32) / n_freq)
    ang = jnp.concatenate([row[:, None] * inv, col[:, None] * inv], axis=-1)
    cos, sin = jnp.cos(ang), jnp.sin(ang)
    reps = LANES // (HEAD_DIM // 2)
    sign = jnp.tile(jnp.concatenate([-jnp.ones((HEAD_DIM // 2,), F32), jnp.ones((HEAD_DIM // 2,), F32)]),
                    LANES // HEAD_DIM)
    return jnp.tile(cos, (1, reps)), jnp.tile(sin, (1, reps)) * sign


def _tile_plan(counts, n_tokens, tm):
    n_tiles = -(-(n_tokens + N_BUCKETS * (tm - 1)) // tm)
    padded = ((counts + tm - 1) // tm) * tm
    ends = jnp.cumsum(padded)
    base = ends - padded
    tile_start = jnp.arange(n_tiles, dtype=I32) * tm
    tile_bucket = jnp.sum((tile_start[:, None] >= ends[None, :]).astype(I32), axis=1)
    active = (tile_bucket < N_BUCKETS).astype(I32)
    last_used = jnp.maximum(jnp.sum(active) - 1, 0)
    tile_bucket = jnp.where(active == 1, tile_bucket, tile_bucket[last_used])
    tile_bucket = jnp.minimum(tile_bucket, N_BUCKETS - 1)
    grp, pair = tile_bucket // N_PAIRS, tile_bucket % N_PAIRS
    ea = grp * EXPERTS_PER_GROUP + jnp.asarray(PAIR_SLOT_A, I32)[pair]
    eb = grp * EXPERTS_PER_GROUP + jnp.asarray(PAIR_SLOT_B, I32)[pair]
    return base, ea, eb, active, n_tiles


def kernel(x, c, ctx, c_ctx, w_mod, b_mod, norm1_g, norm2_g, w_in, b_merge, conv_w, conv_b, lru_wa, lru_ba,
           lru_wx, lru_bx, lru_lambda, q_norm_g, k_norm_g, sink, w_branch, w_out, router_w, router_b,
           expert_w1, expert_w3, expert_w2):
    batch, seq, d = x.shape
    n_ctx_per = ctx.shape[1]
    n_layers = w_mod.shape[0]
    n_ctx = batch * n_ctx_per
    n_lat = batch * seq
    tm = min(256, n_ctx_per)
    tq = 128
    tk = min(512, seq)
    tm_moe = 256
    tn_route = min(1024, n_ctx_per)

    n_mod_rows = -(-(batch + 1) // 8) * 8
    cvec = jnp.zeros((n_mod_rows, d), F32).at[0].set(c_ctx).at[1:batch + 1].set(c)
    mod = _modulation(cvec, w_mod, b_mod)[:, :batch + 1]
    mod = mod.reshape(n_layers, batch + 1, 6, 1, d)

    cos_t, sin_t = _rope_tables(seq)
    seg = jnp.kron(jnp.eye(LANES // HEAD_DIM, dtype=F32),
                   jnp.full((HEAD_DIM, HEAD_DIM), 1.0 / HEAD_DIM, F32)).astype(BF16)
    tri = (jnp.arange(tn_route)[:, None] <= jnp.arange(tn_route)[None, :]).astype(BF16)
    rw_t = router_w.T
    rw3 = rw_t.reshape(N_EXPERTS, 1, d)
    rb = router_b.reshape(N_EXPERTS, 1)

    x_all = jnp.concatenate([ctx.reshape(n_ctx, d), x.reshape(n_lat, d)], axis=0)
    for l in range(n_layers):
        last = l == n_layers - 1
        sh1, sc1, g1, sh2, sc2, g2 = (mod[l, :, k] for k in range(6))
        qgain = jnp.tile(q_norm_g[l], (1, LANES // HEAD_DIM))
        kgain = jnp.tile(k_norm_g[l], (1, LANES // HEAD_DIM))
        xr, gr, qg, kg, vg, qw, kw, vw, mg = _input_projection(
            x_all, sh1, sc1, norm1_g[l][None], w_in[l].astype(BF16), b_merge[l][None], qgain, kgain,
            cos_t, sin_t, seg, n_ctx=n_ctx, seq=seq, tm=tm)
        hfb = _rglru(xr, conv_w[l], conv_b[l][None],
                     jax.vmap(_block_diag)(lru_wa[l]).astype(BF16), jax.vmap(_block_diag)(lru_wx[l]).astype(BF16),
                     lru_ba[l][:, None], lru_bx[l][:, None], lru_lambda[l][:, None],
                     batch=batch, n_ctx_per=n_ctx_per, seq=seq)
        attn_args = dict(batch=batch, n_ctx_per=n_ctx_per, seq=seq, with_ctx_queries=not last, tq=tq, tk=tk)
        yg = _attention(sink[l], qg, kg, vg, band=False, use_sink=False, **attn_args)
        yw = _attention(sink[l], qw, kw, vw, band=True, use_sink=True, **attn_args)
        x1, h2, logits_t = _merge(x_all, hfb, gr, yg, yw, mg, w_branch[l].astype(BF16), w_out[l].astype(BF16),
                                  g1, norm2_g[l][None], sh2, sc2, rw_t,
                                  n_ctx=n_ctx, seq=seq, with_ctx=not last, tm=tm)
        n_tok = x1.shape[0]
        bucket, rank, counts = _routing(logits_t, rb, tri, tn=tn_route)
        base, ea, eb, active, n_tiles = _tile_plan(counts[:N_BUCKETS, 0], n_tok, tm_moe)
        base_col = jnp.zeros((BUCKET_ROWS, 1), I32).at[:N_BUCKETS, 0].set(base.astype(I32))
        pos = _positions(bucket, rank, base_col, tn=tn_route)
        pos3 = pos.reshape(n_tok // tm, 1, tm)
        xs = _dispatch(pos3, h2, jnp.zeros((n_tiles * tm_moe, d), F32), tm=tm)
        ys = _moe(ea, eb, active, xs, expert_w1[l].astype(BF16), expert_w3[l].astype(BF16),
                  expert_w2[l].astype(BF16), rw3, tm=tm_moe)
        x_all = _combine(pos3, x1, g2, ys, n_ctx=n_ctx, seq=seq, with_ctx=not last, tm=tm)
    return x_all.reshape(batch, seq, d)
```

```python
import functools

import jax
import jax.numpy as jnp
from jax import lax
from jax.experimental import pallas as pl
from jax.experimental.pallas import tpu as pltpu

F32 = jnp.float32
BF16 = jnp.bfloat16
I32 = jnp.int32

HEAD_DIM = 64
N_Q_HEADS = 8
N_KV_HEADS = 2
GQA_GROUP = N_Q_HEADS // N_KV_HEADS
ATTN_WIDTH = N_Q_HEADS * HEAD_DIM
KV_WIDTH = N_KV_HEADS * HEAD_DIM
RNN_WIDTH = 512
RNN_BLOCKS = 8
CONV_WIDTH = 4
LRU_C = 8.0
WINDOW = 128
GRID_W = 64
ROPE_BASE = 10000.0
N_BRANCHES = 3
N_EXPERTS = 16
N_GROUPS = 4
EXPERTS_PER_GROUP = 4
EPS = 1e-6
NEG_INF = -1e30
LOG2E = 1.4426950408889634

PAIR_SLOT_A = (0, 0, 0, 1, 1, 3)
PAIR_SLOT_B = (1, 2, 3, 3, 2, 2)
N_PAIRS = 6
N_BUCKETS = N_GROUPS * N_PAIRS
BUCKET_ROWS = 32

LANES = 128
VMEM_LIMIT = 56 * 1024 * 1024

HIGHEST = lax.Precision.HIGHEST


def _cparams(sem):
    return pltpu.CompilerParams(dimension_semantics=sem, vmem_limit_bytes=VMEM_LIMIT)


def _sigmoid(x):
    return 1.0 / (1.0 + jnp.exp(-x))


def _rms_mod(x, g, scale, shift):
    ms = jnp.mean(x * x, axis=-1, keepdims=True)
    return (x * lax.rsqrt(ms + EPS) * g) * (1.0 + scale) + shift


def _mod_kernel(a_ref, w_ref, b_ref, o_ref):
    a = a_ref[...]
    a = a * _sigmoid(a)
    o_ref[0] = jnp.dot(a, w_ref[0], precision=HIGHEST, preferred_element_type=F32) + b_ref[0]


def _modulation(cvec, w_mod, b_mod):
    rows, d = cvec.shape
    n_layers, _, n_out = w_mod.shape
    tn = 1536
    return pl.pallas_call(
        _mod_kernel,
        grid=(n_layers, n_out // tn),
        in_specs=[
            pl.BlockSpec((rows, d), lambda l, j: (0, 0)),
            pl.BlockSpec((1, d, tn), lambda l, j: (l, 0, j)),
            pl.BlockSpec((1, 1, tn), lambda l, j: (l, 0, j)),
        ],
        out_specs=pl.BlockSpec((1, rows, tn), lambda l, j: (l, 0, j)),
        out_shape=jax.ShapeDtypeStruct((n_layers, rows, n_out), F32),
        compiler_params=_cparams(("arbitrary", "arbitrary")),
        name="modulation",
    )(cvec, w_mod, b_mod.reshape(n_layers, 1, n_out))


def _head_norm_rope(p, gain, cos_t, sin_t, seg, lo32):
    ss = p * p
    hi = ss.astype(BF16)
    lo = (ss - hi.astype(F32)).astype(BF16)
    mean = jnp.dot(hi, seg, preferred_element_type=F32) + jnp.dot(lo, seg, preferred_element_type=F32)
    n = p * lax.rsqrt(mean + EPS) * gain
    partner = jnp.where(lo32, pltpu.roll(n, 96, 1), pltpu.roll(n, 32, 1))
    return n * cos_t + partner * sin_t


def _inproj_kernel(x_ref, sh_ref, sc_ref, g_ref, w_ref, bm_ref, qgain_ref, kgain_ref, cos_ref, sin_ref, seg_ref,
                   eye_ref, xr_o, gr_o, qg_o, kg_o, vg_o, qw_o, kw_o, vw_o, mg_o, *, n_ctx_tiles):
    i = pl.program_id(0)
    is_lat = i >= n_ctx_tiles
    h = _rms_mod(x_ref[...], g_ref[...], sc_ref[0], sh_ref[0])
    hb = h.astype(BF16)
    tm = hb.shape[0]

    def proj(c0, width):
        return jnp.dot(hb, w_ref[:, c0:c0 + width], preferred_element_type=F32)

    xr_o[...] = proj(0, RNN_WIDTH).astype(BF16)
    gr_o[...] = jax.nn.gelu(proj(RNN_WIDTH, RNN_WIDTH)).astype(BF16)

    lane = lax.broadcasted_iota(I32, (tm, LANES), 1)
    lo32 = (lane & (HEAD_DIM - 1)) < (HEAD_DIM // 2)
    lo64 = lane < HEAD_DIM
    cos_t = jnp.where(is_lat, cos_ref[...], 1.0)
    sin_t = jnp.where(is_lat, sin_ref[...], 0.0)
    seg = seg_ref[...]
    zero = jnp.zeros((tm, LANES), F32)

    base = 2 * RNN_WIDTH
    for a, (q_o, k_o, v_o) in enumerate(((qg_o, kg_o, vg_o), (qw_o, kw_o, vw_o))):
        qgain = qgain_ref[a:a + 1, :]
        kgain = kgain_ref[a:a + 1, :]
        for c in range(ATTN_WIDTH // LANES):
            p = proj(base + c * LANES, LANES)
            y = _head_norm_rope(p, qgain, cos_t, sin_t, seg, lo32) * (HEAD_DIM ** -0.5 * LOG2E)
            yr = pltpu.roll(y, HEAD_DIM, 1)
            if c < 2:
                out_a, out_b = jnp.where(lo64, y, zero), jnp.where(lo64, yr, zero)
            else:
                out_a, out_b = jnp.where(lo64, zero, yr), jnp.where(lo64, zero, y)
            q_o[2 * c] = out_a.astype(BF16)
            q_o[2 * c + 1] = out_b.astype(BF16)
        pk = proj(base + ATTN_WIDTH, KV_WIDTH)
        k_o[...] = _head_norm_rope(pk, kgain, cos_t, sin_t, seg, lo32).astype(BF16)
        pv = proj(base + ATTN_WIDTH + KV_WIDTH, KV_WIDTH)
        for hd, vh in enumerate((jnp.where(lo64, pv, 1.0).astype(BF16), jnp.where(lo64, 1.0, pv).astype(BF16))):
            if a == 0:
                vh = lax.dot_general(eye_ref[...], vh, (((1,), (1,)), ((), ())),
                                     preferred_element_type=F32).astype(BF16)
            v_o[hd] = vh
        base += ATTN_WIDTH + 2 * KV_WIDTH

    d = x_ref.shape[1]
    for b in range(N_BRANCHES):
        p = proj(base + b * d, d) + bm_ref[:, b * d:(b + 1) * d]
        mg_o[:, b * d:(b + 1) * d] = _sigmoid(p).astype(BF16)


def _input_projection(x_all, mod_sh, mod_sc, g, w_in, b_merge, qgain, kgain, cos_t, sin_t, seg, eye, *, n_ctx, seq, tm):
    ntok, d = x_all.shape
    n_in = w_in.shape[1]
    nct = n_ctx // tm
    tps = seq // tm

    def mod_idx(i):
        return (jnp.where(i >= nct, 1 + (i - nct) // tps, 0), 0, 0)

    def rope_idx(i):
        return (jnp.where(i >= nct, (i - nct) % tps, 0), 0)

    row = lambda i: (i, 0)
    fixed = lambda i: (0, 0)
    out_shape = (
        jax.ShapeDtypeStruct((ntok, RNN_WIDTH), BF16),
        jax.ShapeDtypeStruct((ntok, RNN_WIDTH), BF16),
        jax.ShapeDtypeStruct((N_Q_HEADS, ntok, LANES), BF16),
        jax.ShapeDtypeStruct((ntok, KV_WIDTH), BF16),
        jax.ShapeDtypeStruct((N_KV_HEADS, LANES, ntok), BF16),
        jax.ShapeDtypeStruct((N_Q_HEADS, ntok, LANES), BF16),
        jax.ShapeDtypeStruct((ntok, KV_WIDTH), BF16),
        jax.ShapeDtypeStruct((N_KV_HEADS, ntok, LANES), BF16),
        jax.ShapeDtypeStruct((ntok, N_BRANCHES * d), BF16),
    )
    q_spec = pl.BlockSpec((N_Q_HEADS, tm, LANES), lambda i: (0, i, 0))
    out_specs = (
        pl.BlockSpec((tm, RNN_WIDTH), row), pl.BlockSpec((tm, RNN_WIDTH), row),
        q_spec, pl.BlockSpec((tm, KV_WIDTH), row), pl.BlockSpec((N_KV_HEADS, LANES, tm), lambda i: (0, 0, i)),
        q_spec, pl.BlockSpec((tm, KV_WIDTH), row), pl.BlockSpec((N_KV_HEADS, tm, LANES), lambda i: (0, i, 0)),
        pl.BlockSpec((tm, N_BRANCHES * d), row),
    )
    return pl.pallas_call(
        functools.partial(_inproj_kernel, n_ctx_tiles=nct),
        grid=(ntok // tm,),
        in_specs=[
            pl.BlockSpec((tm, d), row),
            pl.BlockSpec((1, 1, d), mod_idx),
            pl.BlockSpec((1, 1, d), mod_idx),
            pl.BlockSpec((1, d), fixed),
            pl.BlockSpec((d, n_in), fixed),
            pl.BlockSpec((1, N_BRANCHES * d), fixed),
            pl.BlockSpec((2, LANES), fixed),
            pl.BlockSpec((2, LANES), fixed),
            pl.BlockSpec((tm, LANES), rope_idx),
            pl.BlockSpec((tm, LANES), rope_idx),
            pl.BlockSpec((LANES, LANES), fixed),
            pl.BlockSpec((LANES, LANES), fixed),
        ],
        out_specs=out_specs,
        out_shape=out_shape,
        compiler_params=_cparams(("arbitrary",)),
        name="input_projection",
    )(x_all, mod_sh, mod_sc, g, w_in, b_merge, qgain, kgain, cos_t, sin_t, seg, eye)


def _rglru_kernel(x_ref, prev_ref, next_ref, cw_ref, cb_ref, wa_ref, wx_ref, ba_ref, bx_ref, lam_ref,
                  o_ref, a_scr, b_scr, h_scr, y_scr, *, n_chunks):
    d = pl.program_id(0)
    j = pl.program_id(2)
    jj = jnp.where(d == 0, j, jnp.where(j == 0, 0, n_chunks - j))
    prev_valid = jj >= 2
    next_valid = (jj >= 1) & (jj <= n_chunks - 2)

    x = x_ref[...].astype(F32)
    tc = x.shape[0]
    hp = prev_ref[...].astype(F32)
    hn = next_ref[...].astype(F32)
    halo = prev_ref.shape[0]
    pm1 = jnp.where(prev_valid, hp[halo - 1:halo, :], 0.0)
    pm2 = jnp.where(prev_valid, hp[halo - 2:halo - 1, :], 0.0)
    nx0 = jnp.where(next_valid, hn[0:1, :], 0.0)
    row = lax.broadcasted_iota(I32, x.shape, 0)
    xm1 = jnp.where(row == 0, pm1, pltpu.roll(x, 1, 0))
    xm2 = jnp.where(row == 0, pm2, jnp.where(row == 1, pm1, pltpu.roll(x, 2, 0)))
    xp1 = jnp.where(row == tc - 1, nx0, pltpu.roll(x, tc - 1, 0))
    xc = (xm2 * cw_ref[0:1, :] + xm1 * cw_ref[1:2, :] + x * cw_ref[2:3, :] + xp1 * cw_ref[3:4, :]) + cb_ref[...]

    xb = xc.astype(BF16)
    r = _sigmoid(jnp.dot(xb, wa_ref[0], preferred_element_type=F32) + ba_ref[0])
    g = _sigmoid(jnp.dot(xb, wx_ref[0], preferred_element_type=F32) + bx_ref[0])
    neg_lam = -lam_ref[0]
    softplus = jnp.maximum(neg_lam, 0.0) + jnp.log(1.0 + jnp.exp(-jnp.abs(neg_lam)))
    a = jnp.exp(-LRU_C * r * softplus)
    a_scr[...] = a
    b_scr[...] = jnp.sqrt(1.0 - a * a) * (g * xc)

    @pl.when(j == 0)
    def _():
        h_scr[...] = jnp.zeros_like(h_scr)

    def step(t, h):
        rr = jnp.where(d == 0, t, tc - 1 - t)
        h = a_scr[pl.ds(rr, 1), :] * h + b_scr[pl.ds(rr, 1), :]
        y_scr[pl.ds(rr, 1), :] = h
        return h

    h_scr[...] = lax.fori_loop(0, tc, step, h_scr[...], unroll=8)
    o_ref[0] = y_scr[...].astype(BF16)


def _rglru(xr, conv_w, conv_b, wa, wx, ba, bx, lam, *, batch, n_ctx_per, seq):
    ntok, width = xr.shape
    tc = n_ctx_per
    halo = 16
    n_chunks = 1 + seq // tc
    lat0 = batch

    def seq_chunk(d, j):
        return jnp.where(d == 0, j, jnp.where(j == 0, 0, n_chunks - j))

    def chunk_blk(d, b, j):
        jj = seq_chunk(d, j)
        return jnp.where(jj == 0, b, lat0 + b * (seq // tc) + jj - 1)

    def x_idx(d, b, j):
        return (chunk_blk(d, b, j), 0)

    def prev_idx(d, b, j):
        return (jnp.maximum(chunk_blk(d, b, j) * (tc // halo) - 1, 0), 0)

    def next_idx(d, b, j):
        return (jnp.minimum((chunk_blk(d, b, j) + 1) * (tc // halo), ntok // halo - 1), 0)

    per_dir = lambda d, b, j: (d, 0, 0)
    fixed = lambda d, b, j: (0, 0)
    return pl.pallas_call(
        functools.partial(_rglru_kernel, n_chunks=n_chunks),
        grid=(2, batch, n_chunks),
        in_specs=[
            pl.BlockSpec((tc, width), x_idx),
            pl.BlockSpec((halo, width), prev_idx),
            pl.BlockSpec((halo, width), next_idx),
            pl.BlockSpec((CONV_WIDTH, width), fixed),
            pl.BlockSpec((1, width), fixed),
            pl.BlockSpec((1, width, width), per_dir),
            pl.BlockSpec((1, width, width), per_dir),
            pl.BlockSpec((1, 1, width), per_dir),
            pl.BlockSpec((1, 1, width), per_dir),
            pl.BlockSpec((1, 1, width), per_dir),
        ],
        out_specs=pl.BlockSpec((1, tc, width), lambda d, b, j: (d, chunk_blk(d, b, j), 0)),
        out_shape=jax.ShapeDtypeStruct((2, ntok, width), BF16),
        scratch_shapes=[
            pltpu.VMEM((tc, width), F32), pltpu.VMEM((tc, width), F32),
            pltpu.VMEM((1, width), F32), pltpu.VMEM((tc, width), F32),
        ],
        compiler_params=_cparams(("arbitrary", "arbitrary", "arbitrary")),
        name="rglru",
    )(xr, xr, xr, conv_w, conv_b, wa, wx, ba, bx, lam)


LOOKAHEAD = 2


def _col_max(s):
    parts = [s[r:r + 8] for r in range(0, s.shape[0], 8)]
    while len(parts) > 1:
        nxt = [jnp.maximum(parts[k], parts[k + 1]) for k in range(0, len(parts) - 1, 2)]
        if len(parts) % 2:
            nxt.append(parts[-1])
        parts = nxt
    return jnp.max(parts[0], axis=0, keepdims=True)


def _attn_kernel(sink_ref, q_ref, kc_ref, vc_ref, kl_ref, vl_ref, eye_ref, o_ref, *,
                 band, use_sink, tile_off, n_ctx_tiles, tiles_per_seq, tk):
    i = pl.program_id(0) + tile_off
    is_lat = i >= n_ctx_tiles
    qi = jnp.where(is_lat, (i - n_ctx_tiles) % tiles_per_seq, 0)
    tq = q_ref.shape[1]
    nq_rows = GQA_GROUP * tq
    seq = kl_ref.shape[0]
    contract_last = (((1,), (1,)), ((), ()))
    lo64 = lax.broadcasted_iota(I32, (tq, LANES), 1) < HEAD_DIM
    chan = lax.broadcasted_iota(I32, (LANES, nq_rows), 0)

    def transposed(v):
        return lax.dot_general(eye_ref[...], v, contract_last, preferred_element_type=F32).astype(BF16)

    def scores(q, k, mask):
        s = lax.dot_general(k, q, contract_last, preferred_element_type=F32)
        return s if mask is None else jnp.where(mask, s, NEG_INF)

    def absorb(carry, s, vt):
        m, acc = carry
        m_new = jnp.maximum(m, _col_max(s))
        p = jnp.exp2(s - m_new)
        acc = jnp.exp2(m - m_new) * acc + jnp.dot(vt, p.astype(BF16), preferred_element_type=F32)
        return m_new, acc

    def run(latent):
        if latent and band:
            nq = tiles_per_seq
            starts = (jnp.maximum(qi - 1, 0) * tq, qi * tq, jnp.minimum(qi + 1, nq - 1) * tq)
            starts = [pl.multiple_of(s0, tq) for s0 in starts]
            kpos = lax.broadcasted_iota(I32, (3 * tq, nq_rows), 0)
            qpos = lax.broadcasted_iota(I32, (3 * tq, nq_rows), 1) & (tq - 1)
            rel = kpos - tq - qpos
            kmin = jnp.where(qi > 0, 0, tq)
            kmax = jnp.where(qi < nq - 1, 3 * tq, 2 * tq)
            band_ok = (rel >= -WINDOW) & (rel <= WINDOW) & (kpos >= kmin) & (kpos < kmax)
        qs, carries, blocks = [], [], []
        for h in range(N_KV_HEADS):
            qs.append(q_ref[GQA_GROUP * h:GQA_GROUP * (h + 1)].reshape(nq_rows, LANES))
            ones_rows = (chan >= HEAD_DIM) if h == 0 else (chan < HEAD_DIM)
            if use_sink:
                m0 = jnp.concatenate(
                    [jnp.full((1, tq), sink_ref[GQA_GROUP * h + g] * LOG2E, F32) for g in range(GQA_GROUP)], axis=1)
                acc0 = jnp.where(ones_rows, 1.0, 0.0)
            else:
                m0 = jnp.full((1, nq_rows), NEG_INF, F32)
                acc0 = jnp.zeros((LANES, nq_rows), F32)
            carries.append((m0, acc0))
            blk = [(lambda: kc_ref[...],
                    (lambda h=h: transposed(vc_ref[h])) if band else (lambda h=h: vc_ref[h]), None)]
            if latent and band:
                blk.append((
                    lambda: jnp.concatenate([kl_ref[pl.ds(s0, tq), :] for s0 in starts], axis=0),
                    lambda h=h: transposed(jnp.concatenate([vl_ref[h, pl.ds(s0, tq), :] for s0 in starts], axis=0)),
                    band_ok))
            elif latent:
                for c in range(seq // tk):
                    blk.append((lambda c=c: kl_ref[c * tk:(c + 1) * tk, :],
                                lambda c=c, h=h: vl_ref[h, :, c * tk:(c + 1) * tk], None))
            blocks.append(blk)

        items = [(h, c) for c in range(len(blocks[0])) for h in range(N_KV_HEADS)]

        def issue(item):
            h, c = item
            return scores(qs[h], blocks[h][c][0](), blocks[h][c][2])

        ahead = [issue(it) for it in items[:LOOKAHEAD]]
        for idx, (h, c) in enumerate(items):
            s_cur = ahead.pop(0)
            if idx + LOOKAHEAD < len(items):
                ahead.append(issue(items[idx + LOOKAHEAD]))
            carries[h] = absorb(carries[h], s_cur, blocks[h][c][1]())

        for h in range(N_KV_HEADS):
            _, acc = carries[h]
            denom = acc[HEAD_DIM:HEAD_DIM + 1, :] if h == 0 else acc[0:1, :]
            o = jnp.transpose(acc * (1.0 / denom))
            og = [o[g * tq:(g + 1) * tq] for g in range(GQA_GROUP)]
            for c in range(GQA_GROUP // 2):
                e, f = og[2 * c], og[2 * c + 1]
                if h == 0:
                    chunk = jnp.where(lo64, e, pltpu.roll(f, HEAD_DIM, 1))
                else:
                    chunk = jnp.where(lo64, pltpu.roll(e, HEAD_DIM, 1), f)
                col = (h * (GQA_GROUP // 2) + c) * LANES
                o_ref[:, col:col + LANES] = chunk.astype(BF16)

    if tile_off == 0:
        pl.when(is_lat)(lambda: run(True))
        pl.when(jnp.logical_not(is_lat))(lambda: run(False))
    else:
        run(True)


def _attention(sink, q, k, v, eye, *, band, use_sink, batch, n_ctx_per, seq, with_ctx_queries, tq, tk):
    ntok = k.shape[0]
    n_ctx = batch * n_ctx_per
    assert n_ctx % seq == 0, "context rows must cover whole latent-sequence blocks"
    assert tq == WINDOW
    nct = n_ctx // tq
    tps = seq // tq
    tile_off = 0 if with_ctx_queries else nct
    n_tiles = ntok // tq - tile_off

    def batch_of(t):
        i = t + tile_off
        return jnp.where(i >= nct, (i - nct) // tps, (i * tq) // n_ctx_per)

    ctx_idx = lambda t, *_: (batch_of(t), 0)
    lat_idx = lambda t, *_: (n_ctx // seq + batch_of(t), 0)
    if band:
        vc_spec = pl.BlockSpec((N_KV_HEADS, n_ctx_per, LANES), lambda t, *_: (0, batch_of(t), 0))
        vl_spec = pl.BlockSpec((N_KV_HEADS, seq, LANES), lambda t, *_: (0, n_ctx // seq + batch_of(t), 0))
    else:
        vc_spec = pl.BlockSpec((N_KV_HEADS, LANES, n_ctx_per), lambda t, *_: (0, 0, batch_of(t)))
        vl_spec = pl.BlockSpec((N_KV_HEADS, LANES, seq), lambda t, *_: (0, 0, n_ctx // seq + batch_of(t)))
    grid_spec = pltpu.PrefetchScalarGridSpec(
        num_scalar_prefetch=1,
        grid=(n_tiles,),
        in_specs=[
            pl.BlockSpec((N_Q_HEADS, tq, LANES), lambda t, *_: (0, t + tile_off, 0)),
            pl.BlockSpec((n_ctx_per, KV_WIDTH), ctx_idx),
            vc_spec,
            pl.BlockSpec((seq, KV_WIDTH), lat_idx),
            vl_spec,
            pl.BlockSpec((LANES, LANES), lambda t, *_: (0, 0)),
        ],
        out_specs=pl.BlockSpec((tq, ATTN_WIDTH), lambda t, *_: (t, 0)),
    )
    return pl.pallas_call(
        functools.partial(_attn_kernel, band=band, use_sink=use_sink, tile_off=tile_off,
                          n_ctx_tiles=nct, tiles_per_seq=tps, tk=tk),
        grid_spec=grid_spec,
        out_shape=jax.ShapeDtypeStruct((n_tiles * tq, ATTN_WIDTH), BF16),
        compiler_params=_cparams(("arbitrary",)),
        name="window_attention" if band else "global_attention",
    )(sink, q, k, v, k, v, eye)


def _merge_kernel(x_ref, h_ref, gr_ref, yg_ref, yw_ref, mg_ref, wb_ref, wo_ref, g1_ref, n2_ref, sh2_ref, sc2_ref,
                  rw_ref, x1_o, h2_o, lg_o):
    d = x_ref.shape[1]
    y_rnn = ((h_ref[0].astype(F32) + h_ref[1].astype(F32)) * gr_ref[...].astype(F32)).astype(BF16)
    ys = (y_rnn, yg_ref[...], yw_ref[...])
    merged = None
    for b in range(N_BRANCHES):
        t = mg_ref[:, b * d:(b + 1) * d].astype(F32) * jnp.dot(ys[b], wb_ref[b], preferred_element_type=F32)
        merged = t if merged is None else merged + t
    out = jnp.dot(merged.astype(BF16), wo_ref[...], preferred_element_type=F32)
    x1 = x_ref[...] + g1_ref[0] * out
    x1_o[...] = x1
    h2 = _rms_mod(x1, n2_ref[...], sc2_ref[0], sh2_ref[0])
    h2_o[...] = h2
    lg_o[...] = lax.dot_general(rw_ref[...], h2, (((1,), (1,)), ((), ())), precision=HIGHEST,
                                preferred_element_type=F32)


def _merge(x_all, hfb, gr, yg, yw, mg, wb, wo, mod_g1, n2, mod_sh2, mod_sc2, rw_t, *, n_ctx, seq, with_ctx, tm):
    ntok, d = x_all.shape
    nct = n_ctx // tm
    tps = seq // tm
    off = 0 if with_ctx else nct
    n_tiles = ntok // tm - off
    n_out = n_tiles * tm

    def mod_idx(t):
        i = t + off
        return (jnp.where(i >= nct, 1 + (i - nct) // tps, 0), 0, 0)

    row_in = lambda t: (t + off, 0)
    row_out = lambda t: (t, 0)
    fixed2 = lambda t: (0, 0)
    return pl.pallas_call(
        _merge_kernel,
        grid=(n_tiles,),
        in_specs=[
            pl.BlockSpec((tm, d), row_in),
            pl.BlockSpec((2, tm, RNN_WIDTH), lambda t: (0, t + off, 0)),
            pl.BlockSpec((tm, RNN_WIDTH), row_in),
            pl.BlockSpec((tm, ATTN_WIDTH), row_out),
            pl.BlockSpec((tm, ATTN_WIDTH), row_out),
            pl.BlockSpec((tm, N_BRANCHES * d), row_in),
            pl.BlockSpec((N_BRANCHES, RNN_WIDTH, d), lambda t: (0, 0, 0)),
            pl.BlockSpec((d, d), fixed2),
            pl.BlockSpec((1, 1, d), mod_idx),
            pl.BlockSpec((1, d), fixed2),
            pl.BlockSpec((1, 1, d), mod_idx),
            pl.BlockSpec((1, 1, d), mod_idx),
            pl.BlockSpec((N_EXPERTS, d), fixed2),
        ],
        out_specs=(
            pl.BlockSpec((tm, d), row_out),
            pl.BlockSpec((tm, d), row_out),
            pl.BlockSpec((N_EXPERTS, tm), lambda t: (0, t)),
        ),
        out_shape=(
            jax.ShapeDtypeStruct((n_out, d), F32),
            jax.ShapeDtypeStruct((n_out, d), F32),
            jax.ShapeDtypeStruct((N_EXPERTS, n_out), F32),
        ),
        compiler_params=_cparams(("arbitrary",)),
        name="merge",
    )(x_all, hfb, gr, yg, yw, mg, wb, wo, mod_g1, n2, mod_sh2, mod_sc2, rw_t)


def _first_argmax(vals):
    best, idx = vals[0], jnp.zeros(vals[0].shape, I32)
    for k in range(1, len(vals)):
        take = vals[k] > best
        best = jnp.where(take, vals[k], best)
        idx = jnp.where(take, k, idx)
    return best, idx


def _routing_kernel(lg_ref, rb_ref, tri_ref, bucket_o, rank_o, count_o, carry_scr):
    t = pl.program_id(0)

    @pl.when(t == 0)
    def _():
        carry_scr[...] = jnp.zeros_like(carry_scr)

    s = _sigmoid(lg_ref[...])
    sel = s + rb_ref[...]
    rows = [sel[e:e + 1, :] for e in range(N_EXPERTS)]
    tn = s.shape[1]

    grp_scores = []
    for g in range(N_GROUPS):
        r = rows[g * EXPERTS_PER_GROUP:(g + 1) * EXPERTS_PER_GROUP]
        best = None
        for a in range(EXPERTS_PER_GROUP):
            for b in range(a + 1, EXPERTS_PER_GROUP):
                pair = r[a] + r[b]
                best = pair if best is None else jnp.maximum(best, pair)
        grp_scores.append(best)
    _, grp = _first_argmax(grp_scores)

    vals = []
    for k in range(EXPERTS_PER_GROUP):
        v = rows[k]
        for g in range(1, N_GROUPS):
            v = jnp.where(grp == g, rows[g * EXPERTS_PER_GROUP + k], v)
        vals.append(v)
    _, i1 = _first_argmax(vals)
    _, i2 = _first_argmax([jnp.where(i1 == k, -jnp.inf, vals[k]) for k in range(EXPERTS_PER_GROUP)])
    lo = jnp.minimum(i1, i2)
    hi = jnp.maximum(i1, i2)
    pair = jnp.where(lo == 0, hi - 1, jnp.where(lo == 1, 6 - hi, 5))
    bucket = grp * N_PAIRS + pair
    bucket_o[...] = bucket

    onehot = (lax.broadcasted_iota(I32, (BUCKET_ROWS, tn), 0) == bucket).astype(F32)
    incl = jnp.dot(onehot.astype(BF16), tri_ref[...], preferred_element_type=F32)
    before = carry_scr[...] + incl - 1.0
    rank_o[...] = jnp.sum(onehot * before, axis=0, keepdims=True).astype(I32)
    carry_scr[...] = carry_scr[...] + jnp.sum(onehot, axis=1, keepdims=True)
    count_o[...] = jnp.broadcast_to(carry_scr[...], count_o.shape).astype(I32)


def _routing(logits_t, router_b, tri, *, tn):
    n = logits_t.shape[1]
    return pl.pallas_call(
        _routing_kernel,
        grid=(n // tn,),
        in_specs=[
            pl.BlockSpec((N_EXPERTS, tn), lambda t: (0, t)),
            pl.BlockSpec((N_EXPERTS, 1), lambda t: (0, 0)),
            pl.BlockSpec((tn, tn), lambda t: (0, 0)),
        ],
        out_specs=(
            pl.BlockSpec((1, tn), lambda t: (0, t)),
            pl.BlockSpec((1, tn), lambda t: (0, t)),
            pl.BlockSpec((BUCKET_ROWS, LANES), lambda t: (0, 0)),
        ),
        out_shape=(
            jax.ShapeDtypeStruct((1, n), I32),
            jax.ShapeDtypeStruct((1, n), I32),
            jax.ShapeDtypeStruct((BUCKET_ROWS, LANES), I32),
        ),
        scratch_shapes=[pltpu.VMEM((BUCKET_ROWS, 1), F32)],
        compiler_params=_cparams(("arbitrary",)),
        name="routing",
    )(logits_t, router_b, tri)


def _position_kernel(bucket_ref, rank_ref, base_ref, pos_o):
    bucket = bucket_ref[...]
    ids = lax.broadcasted_iota(I32, (BUCKET_ROWS, bucket.shape[1]), 0)
    base = jnp.sum(jnp.where(ids == bucket, base_ref[...], 0), axis=0, keepdims=True)
    pos_o[...] = base + rank_ref[...]


def _positions(bucket, rank, base, *, tn):
    n = bucket.shape[1]
    blk = pl.BlockSpec((1, tn), lambda t: (0, t))
    return pl.pallas_call(
        _position_kernel,
        grid=(n // tn,),
        in_specs=[blk, blk, pl.BlockSpec((BUCKET_ROWS, 1), lambda t: (0, 0))],
        out_specs=blk,
        out_shape=jax.ShapeDtypeStruct((1, n), I32),
        compiler_params=_cparams(("arbitrary",)),
        name="positions",
    )(bucket, rank, base)


def _row_copy_wait(src, dst, sem, n):
    def wait(r, _):
        pltpu.make_async_copy(src.at[pl.ds(0, 1), :], dst.at[pl.ds(0, 1), :], sem).wait()
        return 0

    lax.fori_loop(0, n, wait, 0, unroll=8)


def _dispatch_kernel(pos_ref, h_ref, init_ref, xs_ref, sem):
    del init_ref
    tm = h_ref.shape[0]

    def issue(r, _):
        p = pos_ref[0, 0, r]
        pltpu.make_async_copy(h_ref.at[pl.ds(r, 1), :], xs_ref.at[pl.ds(p, 1), :], sem).start()
        return 0

    lax.fori_loop(0, tm, issue, 0, unroll=8)
    _row_copy_wait(h_ref, xs_ref, sem, tm)


def _dispatch(pos3, h2, xs_init, *, tm):
    n, d = h2.shape
    return pl.pallas_call(
        _dispatch_kernel,
        grid=(n // tm,),
        in_specs=[
            pl.BlockSpec((1, 1, tm), lambda t: (t, 0, 0), memory_space=pltpu.SMEM),
            pl.BlockSpec((tm, d), lambda t: (t, 0)),
            pl.BlockSpec(memory_space=pl.ANY),
        ],
        out_specs=pl.BlockSpec(memory_space=pl.ANY),
        out_shape=jax.ShapeDtypeStruct(xs_init.shape, F32),
        scratch_shapes=[pltpu.SemaphoreType.DMA(())],
        input_output_aliases={2: 0},
        compiler_params=_cparams(("arbitrary",)),
        name="dispatch",
    )(pos3, h2, xs_init)


def _moe_kernel(ea_ref, eb_ref, act_ref, xs_ref, w1a, w3a, w2a, w1b, w3b, w2b, rwa, rwb, ys_ref):
    j = pl.program_id(0)

    @pl.when(act_ref[j] == 0)
    def _():
        ys_ref[...] = jnp.zeros_like(ys_ref)

    @pl.when(act_ref[j] != 0)
    def _():
        x = xs_ref[...]
        xb = x.astype(BF16)

        def ffn(w1, w3, w2):
            u = jnp.dot(xb, w1[0], preferred_element_type=F32)
            hid = (u * _sigmoid(u)) * jnp.dot(xb, w3[0], preferred_element_type=F32)
            return jnp.dot(hid.astype(BF16), w2[0], preferred_element_type=F32)

        sa = _sigmoid(jnp.sum(x * rwa[0], axis=-1, keepdims=True))
        sb = _sigmoid(jnp.sum(x * rwb[0], axis=-1, keepdims=True))
        inv = 1.0 / (sa + sb)
        ys_ref[...] = (sa * inv) * ffn(w1a, w3a, w2a) + (sb * inv) * ffn(w1b, w3b, w2b)


def _moe(ea, eb, act, xs, w1, w3, w2, rw3, *, tm):
    npad, d = xs.shape
    de = w1.shape[2]
    row = lambda j, *_: (j, 0)
    wa = lambda j, ea, eb, act: (ea[j], 0, 0)
    wb = lambda j, ea, eb, act: (eb[j], 0, 0)
    grid_spec = pltpu.PrefetchScalarGridSpec(
        num_scalar_prefetch=3,
        grid=(npad // tm,),
        in_specs=[
            pl.BlockSpec((tm, d), row),
            pl.BlockSpec((1, d, de), wa), pl.BlockSpec((1, d, de), wa), pl.BlockSpec((1, de, d), wa),
            pl.BlockSpec((1, d, de), wb), pl.BlockSpec((1, d, de), wb), pl.BlockSpec((1, de, d), wb),
            pl.BlockSpec((1, 1, d), wa), pl.BlockSpec((1, 1, d), wb),
        ],
        out_specs=pl.BlockSpec((tm, d), row),
    )
    return pl.pallas_call(
        _moe_kernel,
        grid_spec=grid_spec,
        out_shape=jax.ShapeDtypeStruct((npad, d), F32),
        compiler_params=_cparams(("arbitrary",)),
        name="expert_ffn",
    )(ea, eb, act, xs, w1, w3, w2, w1, w3, w2, rw3, rw3)


def _combine_kernel(pos_ref, x_ref, g2_ref, ys_ref, o_ref, buf, sem):
    tm = x_ref.shape[0]

    def issue(r, _):
        p = pos_ref[0, 0, r]
        pltpu.make_async_copy(ys_ref.at[pl.ds(p, 1), :], buf.at[pl.ds(r, 1), :], sem).start()
        return 0

    lax.fori_loop(0, tm, issue, 0, unroll=8)
    _row_copy_wait(ys_ref, buf, sem, tm)
    o_ref[...] = x_ref[...] + g2_ref[0] * buf[...]


def _combine(pos3, x1, mod_g2, ys, *, n_ctx, seq, with_ctx, tm):
    n, d = x1.shape
    nct = n_ctx // tm if with_ctx else 0
    tps = seq // tm

    def mod_idx(t):
        return (jnp.where(t >= nct, 1 + (t - nct) // tps, 0), 0, 0)

    return pl.pallas_call(
        _combine_kernel,
        grid=(n // tm,),
        in_specs=[
            pl.BlockSpec((1, 1, tm), lambda t: (t, 0, 0), memory_space=pltpu.SMEM),
            pl.BlockSpec((tm, d), lambda t: (t, 0)),
            pl.BlockSpec((1, 1, d), mod_idx),
            pl.BlockSpec(memory_space=pl.ANY),
        ],
        out_specs=pl.BlockSpec((tm, d), lambda t: (t, 0)),
        out_shape=jax.ShapeDtypeStruct((n, d), F32),
        scratch_shapes=[pltpu.VMEM((tm, d), F32), pltpu.SemaphoreType.DMA(())],
        compiler_params=_cparams(("arbitrary",)),
        name="combine",
    )(pos3, x1, mod_g2, ys)


def _block_diag(w):
    n, d, e = w.shape
    eye = jnp.eye(n, dtype=w.dtype)
    return (eye[:, None, :, None] * w[:, :, None, :]).reshape(n * d, n * e)


def _rope_tables(seq):
    rows = seq // GRID_W
    row = jnp.repeat(jnp.arange(rows, dtype=F32), GRID_W)
    col = jnp.tile(jnp.arange(GRID_W, dtype=F32), rows)
    n_freq = HEAD_DIM // 4
    inv = ROPE_BASE ** (-jnp.arange(n_freq, dtype=F32) / n_freq)
    ang = jnp.concatenate([row[:, None] * inv, col[:, None] * inv], axis=-1)
    cos, sin = jnp.cos(ang), jnp.sin(ang)
    reps = LANES // (HEAD_DIM // 2)
    sign = jnp.tile(jnp.concatenate([-jnp.ones((HEAD_DIM // 2,), F32), jnp.ones((HEAD_DIM // 2,), F32)]),
                    LANES // HEAD_DIM)
    return jnp.tile(cos, (1, reps)), jnp.tile(sin, (1, reps)) * sign


def _tile_plan(counts, n_tokens, tm):
    n_tiles = -(-(n_tokens + N_BUCKETS * (tm - 1)) // tm)
    padded = ((counts + tm - 1) // tm) * tm
    ends = jnp.cumsum(padded)
    base = ends - padded
    tile_start = jnp.arange(n_tiles, dtype=I32) * tm
    tile_bucket = jnp.sum((tile_start[:, None] >= ends[None, :]).astype(I32), axis=1)
    active = (tile_bucket < N_BUCKETS).astype(I32)
    last_used = jnp.maximum(jnp.sum(active) - 1, 0)
    tile_bucket = jnp.where(active == 1, tile_bucket, tile_bucket[last_used])
    tile_bucket = jnp.minimum(tile_bucket, N_BUCKETS - 1)
    grp, pair = tile_bucket // N_PAIRS, tile_bucket % N_PAIRS
    ea = grp * EXPERTS_PER_GROUP + jnp.asarray(PAIR_SLOT_A, I32)[pair]
    eb = grp * EXPERTS_PER_GROUP + jnp.asarray(PAIR_SLOT_B, I32)[pair]
    return base, ea, eb, active, n_tiles


def kernel(x, c, ctx, c_ctx, w_mod, b_mod, norm1_g, norm2_g, w_in, b_merge, conv_w, conv_b, lru_wa, lru_ba,
           lru_wx, lru_bx, lru_lambda, q_norm_g, k_norm_g, sink, w_branch, w_out, router_w, router_b,
           expert_w1, expert_w3, expert_w2):
    batch, seq, d = x.shape
    n_ctx_per = ctx.shape[1]
    n_layers = w_mod.shape[0]
    n_ctx = batch * n_ctx_per
    n_lat = batch * seq
    tm = min(256, n_ctx_per)
    tq = 128
    tk = min(512, seq)
    tm_moe = 256
    tn_route = min(1024, n_ctx_per)

    n_mod_rows = -(-(batch + 1) // 8) * 8
    cvec = jnp.zeros((n_mod_rows, d), F32).at[0].set(c_ctx).at[1:batch + 1].set(c)
    mod = _modulation(cvec, w_mod, b_mod)[:, :batch + 1]
    mod = mod.reshape(n_layers, batch + 1, 6, 1, d)

    cos_t, sin_t = _rope_tables(seq)
    seg = jnp.kron(jnp.eye(LANES // HEAD_DIM, dtype=F32),
                   jnp.full((HEAD_DIM, HEAD_DIM), 1.0 / HEAD_DIM, F32)).astype(BF16)
    eye = jnp.eye(LANES, dtype=BF16)
    tri = (jnp.arange(tn_route)[:, None] <= jnp.arange(tn_route)[None, :]).astype(BF16)
    rw_t = router_w.T
    rw3 = rw_t.reshape(N_EXPERTS, 1, d)
    rb = router_b.reshape(N_EXPERTS, 1)

    x_all = jnp.concatenate([ctx.reshape(n_ctx, d), x.reshape(n_lat, d)], axis=0)
    for l in range(n_layers):
        last = l == n_layers - 1
        sh1, sc1, g1, sh2, sc2, g2 = (mod[l, :, k] for k in range(6))
        qgain = jnp.tile(q_norm_g[l], (1, LANES // HEAD_DIM))
        kgain = jnp.tile(k_norm_g[l], (1, LANES // HEAD_DIM))
        xr, gr, qg, kg, vg, qw, kw, vw, mg = _input_projection(
            x_all, sh1, sc1, norm1_g[l][None], w_in[l].astype(BF16), b_merge[l][None], qgain, kgain,
            cos_t, sin_t, seg, eye, n_ctx=n_ctx, seq=seq, tm=tm)
        hfb = _rglru(xr, conv_w[l], conv_b[l][None],
                     jax.vmap(_block_diag)(lru_wa[l]).astype(BF16), jax.vmap(_block_diag)(lru_wx[l]).astype(BF16),
                     lru_ba[l][:, None], lru_bx[l][:, None], lru_lambda[l][:, None],
                     batch=batch, n_ctx_per=n_ctx_per, seq=seq)
        attn_args = dict(batch=batch, n_ctx_per=n_ctx_per, seq=seq, with_ctx_queries=not last, tq=tq, tk=tk)
        yg = _attention(sink[l], qg, kg, vg, eye, band=False, use_sink=False, **attn_args)
        yw = _attention(sink[l], qw, kw, vw, eye, band=True, use_sink=True, **attn_args)
        x1, h2, logits_t = _merge(x_all, hfb, gr, yg, yw, mg, w_branch[l].astype(BF16), w_out[l].astype(BF16),
                                  g1, norm2_g[l][None], sh2, sc2, rw_t,
                                  n_ctx=n_ctx, seq=seq, with_ctx=not last, tm=tm)
        n_tok = x1.shape[0]
        bucket, rank, counts = _routing(logits_t, rb, tri, tn=tn_route)
        base, ea, eb, active, n_tiles = _tile_plan(counts[:N_BUCKETS, 0], n_tok, tm_moe)
        base_col = jnp.zeros((BUCKET_ROWS, 1), I32).at[:N_BUCKETS, 0].set(base.astype(I32))
        pos = _positions(bucket, rank, base_col, tn=tn_route)
        pos3 = pos.reshape(n_tok // tm, 1, tm)
        xs = _dispatch(pos3, h2, jnp.zeros((n_tiles * tm_moe, d), F32), tm=tm)
        ys = _moe(ea, eb, active, xs, expert_w1[l].astype(BF16), expert_w3[l].astype(BF16),
                  expert_w2[l].astype(BF16), rw3, tm=tm_moe)
        x_all = _combine(pos3, x1, g2, ys, n_ctx=n_ctx, seq=seq, with_ctx=not last, tm=tm)
    return x_all.reshape(batch, seq, d)
```

```python
import functools

import jax
import jax.numpy as jnp
from jax import lax
from jax.experimental import pallas as pl
from jax.experimental.pallas import tpu as pltpu

F32 = jnp.float32
BF16 = jnp.bfloat16
I32 = jnp.int32

HEAD_DIM = 64
N_Q_HEADS = 8
N_KV_HEADS = 2
GQA_GROUP = N_Q_HEADS // N_KV_HEADS
ATTN_WIDTH = N_Q_HEADS * HEAD_DIM
KV_WIDTH = N_KV_HEADS * HEAD_DIM
RNN_WIDTH = 512
RNN_BLOCKS = 8
CONV_WIDTH = 4
LRU_C = 8.0
WINDOW = 128
GRID_W = 64
ROPE_BASE = 10000.0
N_BRANCHES = 3
N_EXPERTS = 16
N_GROUPS = 4
EXPERTS_PER_GROUP = 4
EPS = 1e-6
NEG_INF = -1e30
LOG2E = 1.4426950408889634

PAIR_SLOT_A = (0, 0, 0, 1, 1, 3)
PAIR_SLOT_B = (1, 2, 3, 3, 2, 2)
N_PAIRS = 6
N_BUCKETS = N_GROUPS * N_PAIRS
BUCKET_ROWS = 32

LANES = 128
SUBLANES = 8
VMEM_LIMIT = 56 * 1024 * 1024

HIGHEST = lax.Precision.HIGHEST


def _cparams(sem):
    return pltpu.CompilerParams(dimension_semantics=sem, vmem_limit_bytes=VMEM_LIMIT)


def _sigmoid(x):
    return 1.0 / (1.0 + jnp.exp(-x))


def _rms_mod(x, g, scale, shift):
    ms = jnp.mean(x * x, axis=-1, keepdims=True)
    return (x * lax.rsqrt(ms + EPS) * g) * (1.0 + scale) + shift


def _mod_kernel(a_ref, w_ref, b_ref, o_ref):
    a = a_ref[...]
    a = a * _sigmoid(a)
    o_ref[0] = jnp.dot(a, w_ref[0], precision=HIGHEST, preferred_element_type=F32) + b_ref[0]


def _modulation(cvec, w_mod, b_mod):
    rows, d = cvec.shape
    n_layers, _, n_out = w_mod.shape
    tn = 1536
    return pl.pallas_call(
        _mod_kernel,
        grid=(n_layers, n_out // tn),
        in_specs=[
            pl.BlockSpec((rows, d), lambda l, j: (0, 0)),
            pl.BlockSpec((1, d, tn), lambda l, j: (l, 0, j)),
            pl.BlockSpec((1, 1, tn), lambda l, j: (l, 0, j)),
        ],
        out_specs=pl.BlockSpec((1, rows, tn), lambda l, j: (l, 0, j)),
        out_shape=jax.ShapeDtypeStruct((n_layers, rows, n_out), F32),
        compiler_params=_cparams(("arbitrary", "arbitrary")),
        name="modulation",
    )(cvec, w_mod, b_mod.reshape(n_layers, 1, n_out))


def _head_norm_rope(p, gain, cos_t, sin_t, seg, lo32):
    ss = p * p
    hi = ss.astype(BF16)
    lo = (ss - hi.astype(F32)).astype(BF16)
    mean = jnp.dot(hi, seg, preferred_element_type=F32) + jnp.dot(lo, seg, preferred_element_type=F32)
    n = p * lax.rsqrt(mean + EPS) * gain
    partner = jnp.where(lo32, pltpu.roll(n, 96, 1), pltpu.roll(n, 32, 1))
    return n * cos_t + partner * sin_t


def _inproj_kernel(x_ref, sh_ref, sc_ref, g_ref, w_ref, bm_ref, qgain_ref, kgain_ref, cos_ref, sin_ref, seg_ref,
                   eye_ref, xr_o, gr_o, qg_o, kg_o, vg_o, qw_o, kw_o, vw_o, mg_o, *, n_ctx_tiles):
    i = pl.program_id(0)
    is_lat = i >= n_ctx_tiles
    h = _rms_mod(x_ref[...], g_ref[...], sc_ref[0], sh_ref[0])
    hb = h.astype(BF16)
    tm = hb.shape[0]

    def proj(c0, width):
        return jnp.dot(hb, w_ref[:, c0:c0 + width], preferred_element_type=F32)

    lane = lax.broadcasted_iota(I32, (tm, LANES), 1)
    lo32 = (lane & (HEAD_DIM - 1)) < (HEAD_DIM // 2)
    lo64 = lane < HEAD_DIM
    cos_t = jnp.where(is_lat, cos_ref[...], 1.0)
    sin_t = jnp.where(is_lat, sin_ref[...], 0.0)
    seg = seg_ref[...]
    zero = jnp.zeros((tm, LANES), F32)
    d = x_ref.shape[1]

    def rnn_epilogue(p):
        xr_o[...] = p[:, :RNN_WIDTH].astype(BF16)
        gr_o[...] = jax.nn.gelu(p[:, RNN_WIDTH:]).astype(BF16)

    def q_epilogue(a, q_o, p):
        qgain = qgain_ref[a:a + 1, :]
        for c in range(ATTN_WIDTH // LANES):
            y = _head_norm_rope(p[:, c * LANES:(c + 1) * LANES], qgain, cos_t, sin_t, seg, lo32)
            y = y * (HEAD_DIM ** -0.5 * LOG2E)
            yr = pltpu.roll(y, HEAD_DIM, 1)
            if c < 2:
                out_a, out_b = jnp.where(lo64, y, zero), jnp.where(lo64, yr, zero)
            else:
                out_a, out_b = jnp.where(lo64, zero, yr), jnp.where(lo64, zero, y)
            q_o[2 * c] = out_a.astype(BF16)
            q_o[2 * c + 1] = out_b.astype(BF16)

    def kv_epilogue(a, k_o, v_o, p):
        k_o[...] = _head_norm_rope(p[:, :KV_WIDTH], kgain_ref[a:a + 1, :], cos_t, sin_t, seg, lo32).astype(BF16)
        pv = p[:, KV_WIDTH:]
        for hd, vh in enumerate((jnp.where(lo64, pv, 1.0).astype(BF16), jnp.where(lo64, 1.0, pv).astype(BF16))):
            if a == 0:
                vh = lax.dot_general(eye_ref[...], vh, (((1,), (1,)), ((), ())),
                                     preferred_element_type=F32).astype(BF16)
            v_o[hd] = vh

    def gate_epilogue(b, p):
        mg_o[:, b * d:(b + 1) * d] = _sigmoid(p + bm_ref[:, b * d:(b + 1) * d]).astype(BF16)

    groups = [(0, 2 * RNN_WIDTH, rnn_epilogue)]
    base = 2 * RNN_WIDTH
    for a, (q_o, k_o, v_o) in enumerate(((qg_o, kg_o, vg_o), (qw_o, kw_o, vw_o))):
        groups.append((base, ATTN_WIDTH, functools.partial(q_epilogue, a, q_o)))
        groups.append((base + ATTN_WIDTH, 2 * KV_WIDTH, functools.partial(kv_epilogue, a, k_o, v_o)))
        base += ATTN_WIDTH + 2 * KV_WIDTH
    for b in range(N_BRANCHES):
        groups.append((base + b * d, d, functools.partial(gate_epilogue, b)))

    p_next = proj(groups[0][0], groups[0][1])
    for n, (_, _, epilogue) in enumerate(groups):
        p_cur = p_next
        if n + 1 < len(groups):
            p_next = proj(groups[n + 1][0], groups[n + 1][1])
        epilogue(p_cur)


def _input_projection(x_all, mod_sh, mod_sc, g, w_in, b_merge, qgain, kgain, cos_t, sin_t, seg, eye, *, n_ctx, seq, tm):
    ntok, d = x_all.shape
    n_in = w_in.shape[1]
    nct = n_ctx // tm
    tps = seq // tm

    def mod_idx(i):
        return (jnp.where(i >= nct, 1 + (i - nct) // tps, 0), 0, 0)

    def rope_idx(i):
        return (jnp.where(i >= nct, (i - nct) % tps, 0), 0)

    row = lambda i: (i, 0)
    fixed = lambda i: (0, 0)
    out_shape = (
        jax.ShapeDtypeStruct((ntok, RNN_WIDTH), BF16),
        jax.ShapeDtypeStruct((ntok, RNN_WIDTH), BF16),
        jax.ShapeDtypeStruct((N_Q_HEADS, ntok, LANES), BF16),
        jax.ShapeDtypeStruct((ntok, KV_WIDTH), BF16),
        jax.ShapeDtypeStruct((N_KV_HEADS, LANES, ntok), BF16),
        jax.ShapeDtypeStruct((N_Q_HEADS, ntok, LANES), BF16),
        jax.ShapeDtypeStruct((ntok, KV_WIDTH), BF16),
        jax.ShapeDtypeStruct((N_KV_HEADS, ntok, LANES), BF16),
        jax.ShapeDtypeStruct((ntok, N_BRANCHES * d), BF16),
    )
    q_spec = pl.BlockSpec((N_Q_HEADS, tm, LANES), lambda i: (0, i, 0))
    out_specs = (
        pl.BlockSpec((tm, RNN_WIDTH), row), pl.BlockSpec((tm, RNN_WIDTH), row),
        q_spec, pl.BlockSpec((tm, KV_WIDTH), row), pl.BlockSpec((N_KV_HEADS, LANES, tm), lambda i: (0, 0, i)),
        q_spec, pl.BlockSpec((tm, KV_WIDTH), row), pl.BlockSpec((N_KV_HEADS, tm, LANES), lambda i: (0, i, 0)),
        pl.BlockSpec((tm, N_BRANCHES * d), row),
    )
    return pl.pallas_call(
        functools.partial(_inproj_kernel, n_ctx_tiles=nct),
        grid=(ntok // tm,),
        in_specs=[
            pl.BlockSpec((tm, d), row),
            pl.BlockSpec((1, 1, d), mod_idx),
            pl.BlockSpec((1, 1, d), mod_idx),
            pl.BlockSpec((1, d), fixed),
            pl.BlockSpec((d, n_in), fixed, pipeline_mode=pl.Buffered(1)),
            pl.BlockSpec((1, N_BRANCHES * d), fixed),
            pl.BlockSpec((2, LANES), fixed),
            pl.BlockSpec((2, LANES), fixed),
            pl.BlockSpec((tm, LANES), rope_idx),
            pl.BlockSpec((tm, LANES), rope_idx),
            pl.BlockSpec((LANES, LANES), fixed),
            pl.BlockSpec((LANES, LANES), fixed),
        ],
        out_specs=out_specs,
        out_shape=out_shape,
        compiler_params=_cparams(("arbitrary",)),
        name="input_projection",
    )(x_all, mod_sh, mod_sc, g, w_in, b_merge, qgain, kgain, cos_t, sin_t, seg, eye)


def _rglru_kernel(x_ref, prev_ref, next_ref, cw_ref, cb_ref, wa_ref, wx_ref, ba_ref, bx_ref, lam_ref,
                  o_ref, a_scr, b_scr, h_scr, y_scr, pe_scr, he_scr, c_scr, *, n_chunks):
    d = pl.program_id(0)
    j = pl.program_id(2)
    jj = jnp.where(d == 0, j, jnp.where(j == 0, 0, n_chunks - j))
    prev_valid = jj >= 2
    next_valid = (jj >= 1) & (jj <= n_chunks - 2)

    x = x_ref[...].astype(F32)
    tc = x.shape[0]
    hp = prev_ref[...].astype(F32)
    hn = next_ref[...].astype(F32)
    halo = prev_ref.shape[0]
    pm1 = jnp.where(prev_valid, hp[halo - 1:halo, :], 0.0)
    pm2 = jnp.where(prev_valid, hp[halo - 2:halo - 1, :], 0.0)
    nx0 = jnp.where(next_valid, hn[0:1, :], 0.0)
    row = lax.broadcasted_iota(I32, x.shape, 0)
    xm1 = jnp.where(row == 0, pm1, pltpu.roll(x, 1, 0))
    xm2 = jnp.where(row == 0, pm2, jnp.where(row == 1, pm1, pltpu.roll(x, 2, 0)))
    xp1 = jnp.where(row == tc - 1, nx0, pltpu.roll(x, tc - 1, 0))
    xc = (xm2 * cw_ref[0:1, :] + xm1 * cw_ref[1:2, :] + x * cw_ref[2:3, :] + xp1 * cw_ref[3:4, :]) + cb_ref[...]

    xb = xc.astype(BF16)
    r = _sigmoid(jnp.dot(xb, wa_ref[0], preferred_element_type=F32) + ba_ref[0])
    g = _sigmoid(jnp.dot(xb, wx_ref[0], preferred_element_type=F32) + bx_ref[0])
    neg_lam = -lam_ref[0]
    softplus = jnp.maximum(neg_lam, 0.0) + jnp.log(1.0 + jnp.exp(-jnp.abs(neg_lam)))
    a = jnp.exp(-LRU_C * r * softplus)
    bb = jnp.sqrt(1.0 - a * a) * (g * xc)
    n_lt = a.shape[1] // LANES
    for lt in range(n_lt):
        a_scr[lt * tc:(lt + 1) * tc, :] = a[:, lt * LANES:(lt + 1) * LANES]
        b_scr[lt * tc:(lt + 1) * tc, :] = bb[:, lt * LANES:(lt + 1) * LANES]

    @pl.when(j == 0)
    def _():
        h_scr[...] = jnp.zeros_like(h_scr)

    ng = tc // SUBLANES

    def scan(reverse):
        order = range(SUBLANES - 1, -1, -1) if reverse else range(SUBLANES)
        prod, loc = {}, {}
        for lt in range(n_lt):
            p = hl = None
            for k in order:
                ak = a_scr[pl.ds(lt * tc + k, ng, stride=SUBLANES), :]
                bk = b_scr[pl.ds(lt * tc + k, ng, stride=SUBLANES), :]
                p, hl = (ak, bk) if p is None else (ak * p, ak * hl + bk)
                prod[lt, k], loc[lt, k] = p, hl
            pe_scr[lt * ng:(lt + 1) * ng, :] = p
            he_scr[lt * ng:(lt + 1) * ng, :] = hl
        c = [h_scr[lt:lt + 1, :] for lt in range(n_lt)]
        for g in (range(ng - 1, -1, -1) if reverse else range(ng)):
            for lt in range(n_lt):
                r = lt * ng + g
                c_scr[r:r + 1, :] = c[lt]
                c[lt] = pe_scr[r:r + 1, :] * c[lt] + he_scr[r:r + 1, :]
        for lt in range(n_lt):
            h_scr[lt:lt + 1, :] = c[lt]
            carry_in = c_scr[lt * ng:(lt + 1) * ng, :]
            for k in order:
                y_scr[pl.ds(lt * tc + k, ng, stride=SUBLANES), :] = loc[lt, k] + prod[lt, k] * carry_in

    pl.when(d == 0)(lambda: scan(False))
    pl.when(d != 0)(lambda: scan(True))
    for lt in range(n_lt):
        o_ref[0, :, lt * LANES:(lt + 1) * LANES] = y_scr[lt * tc:(lt + 1) * tc, :].astype(BF16)


def _rglru(xr, conv_w, conv_b, wa, wx, ba, bx, lam, *, batch, n_ctx_per, seq):
    ntok, width = xr.shape
    tc = n_ctx_per
    halo = 16
    n_chunks = 1 + seq // tc
    n_lt = width // LANES
    lat0 = batch

    def seq_chunk(d, j):
        return jnp.where(d == 0, j, jnp.where(j == 0, 0, n_chunks - j))

    def chunk_blk(d, b, j):
        jj = seq_chunk(d, j)
        return jnp.where(jj == 0, b, lat0 + b * (seq // tc) + jj - 1)

    def x_idx(d, b, j):
        return (chunk_blk(d, b, j), 0)

    def prev_idx(d, b, j):
        return (jnp.maximum(chunk_blk(d, b, j) * (tc // halo) - 1, 0), 0)

    def next_idx(d, b, j):
        return (jnp.minimum((chunk_blk(d, b, j) + 1) * (tc // halo), ntok // halo - 1), 0)

    per_dir = lambda d, b, j: (d, 0, 0)
    fixed = lambda d, b, j: (0, 0)
    return pl.pallas_call(
        functools.partial(_rglru_kernel, n_chunks=n_chunks),
        grid=(2, batch, n_chunks),
        in_specs=[
            pl.BlockSpec((tc, width), x_idx),
            pl.BlockSpec((halo, width), prev_idx),
            pl.BlockSpec((halo, width), next_idx),
            pl.BlockSpec((CONV_WIDTH, width), fixed),
            pl.BlockSpec((1, width), fixed),
            pl.BlockSpec((1, width, width), per_dir),
            pl.BlockSpec((1, width, width), per_dir),
            pl.BlockSpec((1, 1, width), per_dir),
            pl.BlockSpec((1, 1, width), per_dir),
            pl.BlockSpec((1, 1, width), per_dir),
        ],
        out_specs=pl.BlockSpec((1, tc, width), lambda d, b, j: (d, chunk_blk(d, b, j), 0)),
        out_shape=jax.ShapeDtypeStruct((2, ntok, width), BF16),
        scratch_shapes=[
            pltpu.VMEM((n_lt * tc, LANES), F32), pltpu.VMEM((n_lt * tc, LANES), F32),
            pltpu.VMEM((n_lt, LANES), F32), pltpu.VMEM((n_lt * tc, LANES), F32),
            pltpu.VMEM((n_lt * tc // SUBLANES, LANES), F32), pltpu.VMEM((n_lt * tc // SUBLANES, LANES), F32),
            pltpu.VMEM((n_lt * tc // SUBLANES, LANES), F32),
        ],
        compiler_params=_cparams(("arbitrary", "arbitrary", "arbitrary")),
        name="rglru",
    )(xr, xr, xr, conv_w, conv_b, wa, wx, ba, bx, lam)


LOOKAHEAD = 2


def _col_max(s):
    parts = [s[r:r + 8] for r in range(0, s.shape[0], 8)]
    while len(parts) > 1:
        nxt = [jnp.maximum(parts[k], parts[k + 1]) for k in range(0, len(parts) - 1, 2)]
        if len(parts) % 2:
            nxt.append(parts[-1])
        parts = nxt
    return jnp.max(parts[0], axis=0, keepdims=True)


def _attn_kernel(sink_ref, q_ref, kc_ref, vc_ref, kl_ref, vl_ref, eye_ref, o_ref, *,
                 band, use_sink, tile_off, n_ctx_tiles, tiles_per_seq, tk):
    i = pl.program_id(0) + tile_off
    is_lat = i >= n_ctx_tiles
    qi = jnp.where(is_lat, (i - n_ctx_tiles) % tiles_per_seq, 0)
    tq = q_ref.shape[1]
    nq_rows = GQA_GROUP * tq
    seq = kl_ref.shape[0]
    contract_last = (((1,), (1,)), ((), ()))
    lo64 = lax.broadcasted_iota(I32, (tq, LANES), 1) < HEAD_DIM
    chan = lax.broadcasted_iota(I32, (LANES, nq_rows), 0)

    def transposed(v):
        return lax.dot_general(eye_ref[...], v, contract_last, preferred_element_type=F32).astype(BF16)

    def scores(q, k, mask):
        s = lax.dot_general(k, q, contract_last, preferred_element_type=F32)
        return s if mask is None else jnp.where(mask, s, NEG_INF)

    def absorb(carry, s, vt):
        m, acc = carry
        m_new = jnp.maximum(m, _col_max(s))
        p = jnp.exp2(s - m_new)
        acc = jnp.exp2(m - m_new) * acc + jnp.dot(vt, p.astype(BF16), preferred_element_type=F32)
        return m_new, acc

    def run(latent):
        if latent and band:
            nq = tiles_per_seq
            starts = (jnp.maximum(qi - 1, 0) * tq, qi * tq, jnp.minimum(qi + 1, nq - 1) * tq)
            starts = [pl.multiple_of(s0, tq) for s0 in starts]
            kpos = lax.broadcasted_iota(I32, (3 * tq, nq_rows), 0)
            qpos = lax.broadcasted_iota(I32, (3 * tq, nq_rows), 1) & (tq - 1)
            rel = kpos - tq - qpos
            kmin = jnp.where(qi > 0, 0, tq)
            kmax = jnp.where(qi < nq - 1, 3 * tq, 2 * tq)
            band_ok = (rel >= -WINDOW) & (rel <= WINDOW) & (kpos >= kmin) & (kpos < kmax)
        qs, carries, blocks = [], [], []
        for h in range(N_KV_HEADS):
            qs.append(q_ref[GQA_GROUP * h:GQA_GROUP * (h + 1)].reshape(nq_rows, LANES))
            ones_rows = (chan >= HEAD_DIM) if h == 0 else (chan < HEAD_DIM)
            if use_sink:
                m0 = jnp.concatenate(
                    [jnp.full((1, tq), sink_ref[GQA_GROUP * h + g] * LOG2E, F32) for g in range(GQA_GROUP)], axis=1)
                acc0 = jnp.where(ones_rows, 1.0, 0.0)
            else:
                m0 = jnp.full((1, nq_rows), NEG_INF, F32)
                acc0 = jnp.zeros((LANES, nq_rows), F32)
            carries.append((m0, acc0))
            blk = [(lambda: kc_ref[...],
                    (lambda h=h: transposed(vc_ref[h])) if band else (lambda h=h: vc_ref[h]), None)]
            if latent and band:
                blk.append((
                    lambda: jnp.concatenate([kl_ref[pl.ds(s0, tq), :] for s0 in starts], axis=0),
                    lambda h=h: transposed(jnp.concatenate([vl_ref[h, pl.ds(s0, tq), :] for s0 in starts], axis=0)),
                    band_ok))
            elif latent:
                for c in range(seq // tk):
                    blk.append((lambda c=c: kl_ref[c * tk:(c + 1) * tk, :],
                                lambda c=c, h=h: vl_ref[h, :, c * tk:(c + 1) * tk], None))
            blocks.append(blk)

        items = [(h, c) for c in range(len(blocks[0])) for h in range(N_KV_HEADS)]

        def issue(item):
            h, c = item
            return scores(qs[h], blocks[h][c][0](), blocks[h][c][2])

        ahead = [issue(it) for it in items[:LOOKAHEAD]]
        for idx, (h, c) in enumerate(items):
            s_cur = ahead.pop(0)
            if idx + LOOKAHEAD < len(items):
                ahead.append(issue(items[idx + LOOKAHEAD]))
            carries[h] = absorb(carries[h], s_cur, blocks[h][c][1]())

        for h in range(N_KV_HEADS):
            _, acc = carries[h]
            denom = acc[HEAD_DIM:HEAD_DIM + 1, :] if h == 0 else acc[0:1, :]
            o = jnp.transpose(acc * (1.0 / denom))
            og = [o[g * tq:(g + 1) * tq] for g in range(GQA_GROUP)]
            for c in range(GQA_GROUP // 2):
                e, f = og[2 * c], og[2 * c + 1]
                if h == 0:
                    chunk = jnp.where(lo64, e, pltpu.roll(f, HEAD_DIM, 1))
                else:
                    chunk = jnp.where(lo64, pltpu.roll(e, HEAD_DIM, 1), f)
                col = (h * (GQA_GROUP // 2) + c) * LANES
                o_ref[:, col:col + LANES] = chunk.astype(BF16)

    if tile_off == 0:
        pl.when(is_lat)(lambda: run(True))
        pl.when(jnp.logical_not(is_lat))(lambda: run(False))
    else:
        run(True)


def _attention(sink, q, k, v, eye, *, band, use_sink, batch, n_ctx_per, seq, with_ctx_queries, tq, tk):
    ntok = k.shape[0]
    n_ctx = batch * n_ctx_per
    assert n_ctx % seq == 0, "context rows must cover whole latent-sequence blocks"
    assert tq == WINDOW
    nct = n_ctx // tq
    tps = seq // tq
    tile_off = 0 if with_ctx_queries else nct
    n_tiles = ntok // tq - tile_off

    def batch_of(t):
        i = t + tile_off
        return jnp.where(i >= nct, (i - nct) // tps, (i * tq) // n_ctx_per)

    ctx_idx = lambda t, *_: (batch_of(t), 0)
    lat_idx = lambda t, *_: (n_ctx // seq + batch_of(t), 0)
    if band:
        vc_spec = pl.BlockSpec((N_KV_HEADS, n_ctx_per, LANES), lambda t, *_: (0, batch_of(t), 0))
        vl_spec = pl.BlockSpec((N_KV_HEADS, seq, LANES), lambda t, *_: (0, n_ctx // seq + batch_of(t), 0))
    else:
        vc_spec = pl.BlockSpec((N_KV_HEADS, LANES, n_ctx_per), lambda t, *_: (0, 0, batch_of(t)))
        vl_spec = pl.BlockSpec((N_KV_HEADS, LANES, seq), lambda t, *_: (0, 0, n_ctx // seq + batch_of(t)))
    grid_spec = pltpu.PrefetchScalarGridSpec(
        num_scalar_prefetch=1,
        grid=(n_tiles,),
        in_specs=[
            pl.BlockSpec((N_Q_HEADS, tq, LANES), lambda t, *_: (0, t + tile_off, 0)),
            pl.BlockSpec((n_ctx_per, KV_WIDTH), ctx_idx),
            vc_spec,
            pl.BlockSpec((seq, KV_WIDTH), lat_idx),
            vl_spec,
            pl.BlockSpec((LANES, LANES), lambda t, *_: (0, 0)),
        ],
        out_specs=pl.BlockSpec((tq, ATTN_WIDTH), lambda t, *_: (t, 0)),
    )
    return pl.pallas_call(
        functools.partial(_attn_kernel, band=band, use_sink=use_sink, tile_off=tile_off,
                          n_ctx_tiles=nct, tiles_per_seq=tps, tk=tk),
        grid_spec=grid_spec,
        out_shape=jax.ShapeDtypeStruct((n_tiles * tq, ATTN_WIDTH), BF16),
        compiler_params=_cparams(("arbitrary",)),
        name="window_attention" if band else "global_attention",
    )(sink, q, k, v, k, v, eye)


def _merge_kernel(x_ref, h_ref, gr_ref, yg_ref, yw_ref, mg_ref, wb_ref, wo_ref, g1_ref, n2_ref, sh2_ref, sc2_ref,
                  rwh_ref, rwl_ref, x1_o, h2_o, lg_o):
    tm, d = x_ref.shape
    n_parts = 2 if tm % 256 == 0 else 1
    part = tm // n_parts
    rows = [slice(k * part, (k + 1) * part) for k in range(n_parts)]

    def branches(r):
        y_rnn = ((h_ref[0, r, :].astype(F32) + h_ref[1, r, :].astype(F32)) * gr_ref[r, :].astype(F32)).astype(BF16)
        ys = (y_rnn, yg_ref[r, :], yw_ref[r, :])
        merged = None
        for b in range(N_BRANCHES):
            t = mg_ref[r, b * d:(b + 1) * d].astype(F32) * jnp.dot(ys[b], wb_ref[b], preferred_element_type=F32)
            merged = t if merged is None else merged + t
        return merged.astype(BF16)

    merged = [branches(r) for r in rows]
    outs = [jnp.dot(m, wo_ref[...], preferred_element_type=F32) for m in merged]
    h2s = []
    for r, out in zip(rows, outs):
        x1 = x_ref[r, :] + g1_ref[0] * out
        x1_o[r, :] = x1
        h2 = _rms_mod(x1, n2_ref[...], sc2_ref[0], sh2_ref[0])
        h2_o[r, :] = h2
        h2s.append(h2)
    for r, h2 in zip(rows, h2s):
        hi = h2.astype(BF16)
        lo = (h2 - hi.astype(F32)).astype(BF16)
        lg = jnp.dot(hi, rwh_ref[...], preferred_element_type=F32) + (
            jnp.dot(lo, rwh_ref[...], preferred_element_type=F32) + jnp.dot(hi, rwl_ref[...], preferred_element_type=F32))
        lg_o[:, r] = jnp.transpose(lg)[:N_EXPERTS, :]


def _merge(x_all, hfb, gr, yg, yw, mg, wb, wo, mod_g1, n2, mod_sh2, mod_sc2, rw_hi, rw_lo, *, n_ctx, seq, with_ctx, tm):
    ntok, d = x_all.shape
    nct = n_ctx // tm
    tps = seq // tm
    off = 0 if with_ctx else nct
    n_tiles = ntok // tm - off
    n_out = n_tiles * tm

    def mod_idx(t):
        i = t + off
        return (jnp.where(i >= nct, 1 + (i - nct) // tps, 0), 0, 0)

    row_in = lambda t: (t + off, 0)
    row_out = lambda t: (t, 0)
    fixed2 = lambda t: (0, 0)
    return pl.pallas_call(
        _merge_kernel,
        grid=(n_tiles,),
        in_specs=[
            pl.BlockSpec((tm, d), row_in),
            pl.BlockSpec((2, tm, RNN_WIDTH), lambda t: (0, t + off, 0)),
            pl.BlockSpec((tm, RNN_WIDTH), row_in),
            pl.BlockSpec((tm, ATTN_WIDTH), row_out),
            pl.BlockSpec((tm, ATTN_WIDTH), row_out),
            pl.BlockSpec((tm, N_BRANCHES * d), row_in),
            pl.BlockSpec((N_BRANCHES, RNN_WIDTH, d), lambda t: (0, 0, 0)),
            pl.BlockSpec((d, d), fixed2),
            pl.BlockSpec((1, 1, d), mod_idx),
            pl.BlockSpec((1, d), fixed2),
            pl.BlockSpec((1, 1, d), mod_idx),
            pl.BlockSpec((1, 1, d), mod_idx),
            pl.BlockSpec((d, LANES), fixed2),
            pl.BlockSpec((d, LANES), fixed2),
        ],
        out_specs=(
            pl.BlockSpec((tm, d), row_out),
            pl.BlockSpec((tm, d), row_out),
            pl.BlockSpec((N_EXPERTS, tm), lambda t: (0, t)),
        ),
        out_shape=(
            jax.ShapeDtypeStruct((n_out, d), F32),
            jax.ShapeDtypeStruct((n_out, d), F32),
            jax.ShapeDtypeStruct((N_EXPERTS, n_out), F32),
        ),
        compiler_params=_cparams(("arbitrary",)),
        name="merge",
    )(x_all, hfb, gr, yg, yw, mg, wb, wo, mod_g1, n2, mod_sh2, mod_sc2, rw_hi, rw_lo)


def _first_argmax(vals):
    best, idx = vals[0], jnp.zeros(vals[0].shape, I32)
    for k in range(1, len(vals)):
        take = vals[k] > best
        best = jnp.where(take, vals[k], best)
        idx = jnp.where(take, k, idx)
    return best, idx


def _routing_kernel(lg_ref, rb_ref, tri_ref, bucket_o, rank_o, count_o, carry_scr):
    t = pl.program_id(0)

    @pl.when(t == 0)
    def _():
        carry_scr[...] = jnp.zeros_like(carry_scr)

    s = _sigmoid(lg_ref[...])
    sel = s + rb_ref[...]
    rows = [sel[e:e + 1, :] for e in range(N_EXPERTS)]
    tn = s.shape[1]

    grp_scores = []
    for g in range(N_GROUPS):
        r = rows[g * EXPERTS_PER_GROUP:(g + 1) * EXPERTS_PER_GROUP]
        best = None
        for a in range(EXPERTS_PER_GROUP):
            for b in range(a + 1, EXPERTS_PER_GROUP):
                pair = r[a] + r[b]
                best = pair if best is None else jnp.maximum(best, pair)
        grp_scores.append(best)
    _, grp = _first_argmax(grp_scores)

    vals = []
    for k in range(EXPERTS_PER_GROUP):
        v = rows[k]
        for g in range(1, N_GROUPS):
            v = jnp.where(grp == g, rows[g * EXPERTS_PER_GROUP + k], v)
        vals.append(v)
    _, i1 = _first_argmax(vals)
    _, i2 = _first_argmax([jnp.where(i1 == k, -jnp.inf, vals[k]) for k in range(EXPERTS_PER_GROUP)])
    lo = jnp.minimum(i1, i2)
    hi = jnp.maximum(i1, i2)
    pair = jnp.where(lo == 0, hi - 1, jnp.where(lo == 1, 6 - hi, 5))
    bucket = grp * N_PAIRS + pair
    bucket_o[...] = bucket

    onehot = (lax.broadcasted_iota(I32, (BUCKET_ROWS, tn), 0) == bucket).astype(F32)
    incl = jnp.dot(onehot.astype(BF16), tri_ref[...], preferred_element_type=F32)
    before = carry_scr[...] + incl - 1.0
    rank_o[...] = jnp.sum(onehot * before, axis=0, keepdims=True).astype(I32)
    carry_scr[...] = carry_scr[...] + jnp.sum(onehot, axis=1, keepdims=True)
    count_o[...] = jnp.broadcast_to(carry_scr[...], count_o.shape).astype(I32)


def _routing(logits_t, router_b, tri, *, tn):
    n = logits_t.shape[1]
    return pl.pallas_call(
        _routing_kernel,
        grid=(n // tn,),
        in_specs=[
            pl.BlockSpec((N_EXPERTS, tn), lambda t: (0, t)),
            pl.BlockSpec((N_EXPERTS, 1), lambda t: (0, 0)),
            pl.BlockSpec((tn, tn), lambda t: (0, 0)),
        ],
        out_specs=(
            pl.BlockSpec((1, tn), lambda t: (0, t)),
            pl.BlockSpec((1, tn), lambda t: (0, t)),
            pl.BlockSpec((BUCKET_ROWS, LANES), lambda t: (0, 0)),
        ),
        out_shape=(
            jax.ShapeDtypeStruct((1, n), I32),
            jax.ShapeDtypeStruct((1, n), I32),
            jax.ShapeDtypeStruct((BUCKET_ROWS, LANES), I32),
        ),
        scratch_shapes=[pltpu.VMEM((BUCKET_ROWS, 1), F32)],
        compiler_params=_cparams(("arbitrary",)),
        name="routing",
    )(logits_t, router_b, tri)


def _position_kernel(bucket_ref, rank_ref, base_ref, pos_o):
    bucket = bucket_ref[...]
    ids = lax.broadcasted_iota(I32, (BUCKET_ROWS, bucket.shape[1]), 0)
    base = jnp.sum(jnp.where(ids == bucket, base_ref[...], 0), axis=0, keepdims=True)
    pos_o[...] = base + rank_ref[...]


def _positions(bucket, rank, base, *, tn):
    n = bucket.shape[1]
    blk = pl.BlockSpec((1, tn), lambda t: (0, t))
    return pl.pallas_call(
        _position_kernel,
        grid=(n // tn,),
        in_specs=[blk, blk, pl.BlockSpec((BUCKET_ROWS, 1), lambda t: (0, 0))],
        out_specs=blk,
        out_shape=jax.ShapeDtypeStruct((1, n), I32),
        compiler_params=_cparams(("arbitrary",)),
        name="positions",
    )(bucket, rank, base)


def _row_copy_wait(src, dst, sem, n):
    def wait(r, _):
        pltpu.make_async_copy(src.at[pl.ds(0, 1), :], dst.at[pl.ds(0, 1), :], sem).wait()
        return 0

    lax.fori_loop(0, n, wait, 0, unroll=8)


def _dispatch_kernel(pos_ref, h_ref, init_ref, xs_ref, sem):
    del init_ref
    tm = h_ref.shape[0]

    def issue(r, _):
        p = pos_ref[0, 0, r]
        pltpu.make_async_copy(h_ref.at[pl.ds(r, 1), :], xs_ref.at[pl.ds(p, 1), :], sem).start()
        return 0

    lax.fori_loop(0, tm, issue, 0, unroll=8)
    _row_copy_wait(h_ref, xs_ref, sem, tm)


def _dispatch(pos3, h2, xs_init, *, tm):
    n, d = h2.shape
    return pl.pallas_call(
        _dispatch_kernel,
        grid=(n // tm,),
        in_specs=[
            pl.BlockSpec((1, 1, tm), lambda t: (t, 0, 0), memory_space=pltpu.SMEM),
            pl.BlockSpec((tm, d), lambda t: (t, 0)),
            pl.BlockSpec(memory_space=pl.ANY),
        ],
        out_specs=pl.BlockSpec(memory_space=pl.ANY),
        out_shape=jax.ShapeDtypeStruct(xs_init.shape, F32),
        scratch_shapes=[pltpu.SemaphoreType.DMA(())],
        input_output_aliases={2: 0},
        compiler_params=_cparams(("arbitrary",)),
        name="dispatch",
    )(pos3, h2, xs_init)


def _moe_kernel(ea_ref, eb_ref, act_ref, xs_ref, w1a, w3a, w2a, w1b, w3b, w2b, rwa, rwb, ys_ref):
    j = pl.program_id(0)

    @pl.when(act_ref[j] == 0)
    def _():
        ys_ref[...] = jnp.zeros_like(ys_ref)

    @pl.when(act_ref[j] != 0)
    def _():
        x = xs_ref[...]
        xb = x.astype(BF16)

        def gated(u, g):
            return ((u * _sigmoid(u)) * g).astype(BF16)

        ua = jnp.dot(xb, w1a[0], preferred_element_type=F32)
        ga = jnp.dot(xb, w3a[0], preferred_element_type=F32)
        ub = jnp.dot(xb, w1b[0], preferred_element_type=F32)
        gb = jnp.dot(xb, w3b[0], preferred_element_type=F32)
        ya = jnp.dot(gated(ua, ga), w2a[0], preferred_element_type=F32)
        yb = jnp.dot(gated(ub, gb), w2b[0], preferred_element_type=F32)
        sa = _sigmoid(jnp.sum(x * rwa[0], axis=-1, keepdims=True))
        sb = _sigmoid(jnp.sum(x * rwb[0], axis=-1, keepdims=True))
        inv = 1.0 / (sa + sb)
        ys_ref[...] = (sa * inv) * ya + (sb * inv) * yb


def _moe(ea, eb, act, xs, w1, w3, w2, rw3, *, tm):
    npad, d = xs.shape
    de = w1.shape[2]
    row = lambda j, *_: (j, 0)
    wa = lambda j, ea, eb, act: (ea[j], 0, 0)
    wb = lambda j, ea, eb, act: (eb[j], 0, 0)
    grid_spec = pltpu.PrefetchScalarGridSpec(
        num_scalar_prefetch=3,
        grid=(npad // tm,),
        in_specs=[
            pl.BlockSpec((tm, d), row),
            pl.BlockSpec((1, d, de), wa), pl.BlockSpec((1, d, de), wa), pl.BlockSpec((1, de, d), wa),
            pl.BlockSpec((1, d, de), wb), pl.BlockSpec((1, d, de), wb), pl.BlockSpec((1, de, d), wb),
            pl.BlockSpec((1, 1, d), wa), pl.BlockSpec((1, 1, d), wb),
        ],
        out_specs=pl.BlockSpec((tm, d), row),
    )
    return pl.pallas_call(
        _moe_kernel,
        grid_spec=grid_spec,
        out_shape=jax.ShapeDtypeStruct((npad, d), F32),
        compiler_params=_cparams(("arbitrary",)),
        name="expert_ffn",
    )(ea, eb, act, xs, w1, w3, w2, w1, w3, w2, rw3, rw3)


def _combine_kernel(pos_ref, x_ref, g2_ref, ys_ref, o_ref, buf, sem):
    tm = x_ref.shape[0]

    def issue(r, _):
        p = pos_ref[0, 0, r]
        pltpu.make_async_copy(ys_ref.at[pl.ds(p, 1), :], buf.at[pl.ds(r, 1), :], sem).start()
        return 0

    lax.fori_loop(0, tm, issue, 0, unroll=8)
    _row_copy_wait(ys_ref, buf, sem, tm)
    o_ref[...] = x_ref[...] + g2_ref[0] * buf[...]


def _combine(pos3, x1, mod_g2, ys, *, n_ctx, seq, with_ctx, tm):
    n, d = x1.shape
    nct = n_ctx // tm if with_ctx else 0
    tps = seq // tm

    def mod_idx(t):
        return (jnp.where(t >= nct, 1 + (t - nct) // tps, 0), 0, 0)

    return pl.pallas_call(
        _combine_kernel,
        grid=(n // tm,),
        in_specs=[
            pl.BlockSpec((1, 1, tm), lambda t: (t, 0, 0), memory_space=pltpu.SMEM),
            pl.BlockSpec((tm, d), lambda t: (t, 0)),
            pl.BlockSpec((1, 1, d), mod_idx),
            pl.BlockSpec(memory_space=pl.ANY),
        ],
        out_specs=pl.BlockSpec((tm, d), lambda t: (t, 0)),
        out_shape=jax.ShapeDtypeStruct((n, d), F32),
        scratch_shapes=[pltpu.VMEM((tm, d), F32), pltpu.SemaphoreType.DMA(())],
        compiler_params=_cparams(("arbitrary",)),
        name="combine",
    )(pos3, x1, mod_g2, ys)


def _block_diag(w):
    n, d, e = w.shape
    eye = jnp.eye(n, dtype=w.dtype)
    return (eye[:, None, :, None] * w[:, :, None, :]).reshape(n * d, n * e)


def _rope_tables(seq):
    rows = seq // GRID_W
    row = jnp.repeat(jnp.arange(rows, dtype=F32), GRID_W)
    col = jnp.tile(jnp.arange(GRID_W, dtype=F32), rows)
    n_freq = HEAD_DIM // 4
    inv = ROPE_BASE ** (-jnp.arange(n_freq, dtype=F32) / n_freq)
    ang = jnp.concatenate([row[:, None] * inv, col[:, None] * inv], axis=-1)
    cos, sin = jnp.cos(ang), jnp.sin(ang)
    reps = LANES // (HEAD_DIM // 2)
    sign = jnp.tile(jnp.concatenate([-jnp.ones((HEAD_DIM // 2,), F32), jnp.ones((HEAD_DIM // 2,), F32)]),
                    LANES // HEAD_DIM)
    return jnp.tile(cos, (1, reps)), jnp.tile(sin, (1, reps)) * sign


def _tile_plan(counts, n_tokens, tm):
    n_tiles = -(-(n_tokens + N_BUCKETS * (tm - 1)) // tm)
    padded = ((counts + tm - 1) // tm) * tm
    ends = jnp.cumsum(padded)
    base = ends - padded
    tile_start = jnp.arange(n_tiles, dtype=I32) * tm
    tile_bucket = jnp.sum((tile_start[:, None] >= ends[None, :]).astype(I32), axis=1)
    active = (tile_bucket < N_BUCKETS).astype(I32)
    last_used = jnp.maximum(jnp.sum(active) - 1, 0)
    tile_bucket = jnp.where(active == 1, tile_bucket, tile_bucket[last_used])
    tile_bucket = jnp.minimum(tile_bucket, N_BUCKETS - 1)
    grp, pair = tile_bucket // N_PAIRS, tile_bucket % N_PAIRS
    ea = grp * EXPERTS_PER_GROUP + jnp.asarray(PAIR_SLOT_A, I32)[pair]
    eb = grp * EXPERTS_PER_GROUP + jnp.asarray(PAIR_SLOT_B, I32)[pair]
    return base, ea, eb, active, n_tiles


def kernel(x, c, ctx, c_ctx, w_mod, b_mod, norm1_g, norm2_g, w_in, b_merge, conv_w, conv_b, lru_wa, lru_ba,
           lru_wx, lru_bx, lru_lambda, q_norm_g, k_norm_g, sink, w_branch, w_out, router_w, router_b,
           expert_w1, expert_w3, expert_w2):
    batch, seq, d = x.shape
    n_ctx_per = ctx.shape[1]
    n_layers = w_mod.shape[0]
    n_ctx = batch * n_ctx_per
    n_lat = batch * seq
    tm = min(256, n_ctx_per)
    tm_mxu = min(512, n_ctx, seq)
    tq = 128
    tk = min(512, seq)
    tm_moe = 256
    tn_route = min(1024, n_ctx_per)

    n_mod_rows = -(-(batch + 1) // 8) * 8
    cvec = jnp.zeros((n_mod_rows, d), F32).at[0].set(c_ctx).at[1:batch + 1].set(c)
    mod = _modulation(cvec, w_mod, b_mod)[:, :batch + 1]
    mod = mod.reshape(n_layers, batch + 1, 6, 1, d)

    cos_t, sin_t = _rope_tables(seq)
    seg = jnp.kron(jnp.eye(LANES // HEAD_DIM, dtype=F32),
                   jnp.full((HEAD_DIM, HEAD_DIM), 1.0 / HEAD_DIM, F32)).astype(BF16)
    eye = jnp.eye(LANES, dtype=BF16)
    tri = (jnp.arange(tn_route)[:, None] <= jnp.arange(tn_route)[None, :]).astype(BF16)
    rw_t = router_w.T
    rw_pad = jnp.zeros((d, LANES), F32).at[:, :N_EXPERTS].set(router_w)
    rw_hi = rw_pad.astype(BF16)
    rw_lo = (rw_pad - rw_hi.astype(F32)).astype(BF16)
    rw3 = rw_t.reshape(N_EXPERTS, 1, d)
    rb = router_b.reshape(N_EXPERTS, 1)

    x_all = jnp.concatenate([ctx.reshape(n_ctx, d), x.reshape(n_lat, d)], axis=0)
    for l in range(n_layers):
        last = l == n_layers - 1
        sh1, sc1, g1, sh2, sc2, g2 = (mod[l, :, k] for k in range(6))
        qgain = jnp.tile(q_norm_g[l], (1, LANES // HEAD_DIM))
        kgain = jnp.tile(k_norm_g[l], (1, LANES // HEAD_DIM))
        xr, gr, qg, kg, vg, qw, kw, vw, mg = _input_projection(
            x_all, sh1, sc1, norm1_g[l][None], w_in[l].astype(BF16), b_merge[l][None], qgain, kgain,
            cos_t, sin_t, seg, eye, n_ctx=n_ctx, seq=seq, tm=tm_mxu)
        hfb = _rglru(xr, conv_w[l], conv_b[l][None],
                     jax.vmap(_block_diag)(lru_wa[l]).astype(BF16), jax.vmap(_block_diag)(lru_wx[l]).astype(BF16),
                     lru_ba[l][:, None], lru_bx[l][:, None], lru_lambda[l][:, None],
                     batch=batch, n_ctx_per=n_ctx_per, seq=seq)
        attn_args = dict(batch=batch, n_ctx_per=n_ctx_per, seq=seq, with_ctx_queries=not last, tq=tq, tk=tk)
        yg = _attention(sink[l], qg, kg, vg, eye, band=False, use_sink=False, **attn_args)
        yw = _attention(sink[l], qw, kw, vw, eye, band=True, use_sink=True, **attn_args)
        x1, h2, logits_t = _merge(x_all, hfb, gr, yg, yw, mg, w_branch[l].astype(BF16), w_out[l].astype(BF16),
                                  g1, norm2_g[l][None], sh2, sc2, rw_hi, rw_lo,
                                  n_ctx=n_ctx, seq=seq, with_ctx=not last, tm=tm_mxu)
        n_tok = x1.shape[0]
        bucket, rank, counts = _routing(logits_t, rb, tri, tn=tn_route)
        base, ea, eb, active, n_tiles = _tile_plan(counts[:N_BUCKETS, 0], n_tok, tm_moe)
        base_col = jnp.zeros((BUCKET_ROWS, 1), I32).at[:N_BUCKETS, 0].set(base.astype(I32))
        pos = _positions(bucket, rank, base_col, tn=tn_route)
        pos3 = pos.reshape(n_tok // tm, 1, tm)
        xs = _dispatch(pos3, h2, jnp.zeros((n_tiles * tm_moe, d), F32), tm=tm)
        ys = _moe(ea, eb, active, xs, expert_w1[l].astype(BF16), expert_w3[l].astype(BF16),
                  expert_w2[l].astype(BF16), rw3, tm=tm_moe)
        x_all = _combine(pos3, x1, g2, ys, n_ctx=n_ctx, seq=seq, with_ctx=not last, tm=tm)
    return x_all.reshape(batch, seq, d)
```

```python
import functools

import jax
import jax.numpy as jnp
from jax import lax
from jax.experimental import pallas as pl
from jax.experimental.pallas import tpu as pltpu

F32 = jnp.float32
BF16 = jnp.bfloat16
I32 = jnp.int32

HEAD_DIM = 64
N_Q_HEADS = 8
N_KV_HEADS = 2
GQA_GROUP = N_Q_HEADS // N_KV_HEADS
ATTN_WIDTH = N_Q_HEADS * HEAD_DIM
KV_WIDTH = N_KV_HEADS * HEAD_DIM
RNN_WIDTH = 512
RNN_BLOCKS = 8
CONV_WIDTH = 4
LRU_C = 8.0
WINDOW = 128
GRID_W = 64
ROPE_BASE = 10000.0
N_BRANCHES = 3
N_EXPERTS = 16
N_GROUPS = 4
EXPERTS_PER_GROUP = 4
EPS = 1e-6
NEG_INF = -1e30
LOG2E = 1.4426950408889634

PAIR_SLOT_A = (0, 0, 0, 1, 1, 3)
PAIR_SLOT_B = (1, 2, 3, 3, 2, 2)
N_PAIRS = 6
N_BUCKETS = N_GROUPS * N_PAIRS
BUCKET_ROWS = 32

LANES = 128
SUBLANES = 8
VMEM_LIMIT = 56 * 1024 * 1024

HIGHEST = lax.Precision.HIGHEST


def _cparams(sem):
    return pltpu.CompilerParams(dimension_semantics=sem, vmem_limit_bytes=VMEM_LIMIT)


def _sigmoid(x):
    return 1.0 / (1.0 + jnp.exp(-x))


def _rms_mod(x, g, scale, shift):
    ms = jnp.mean(x * x, axis=-1, keepdims=True)
    return (x * lax.rsqrt(ms + EPS) * g) * (1.0 + scale) + shift


def _mod_kernel(a_ref, w_ref, b_ref, o_ref):
    a = a_ref[...]
    a = a * _sigmoid(a)
    o_ref[0] = jnp.dot(a, w_ref[0], precision=HIGHEST, preferred_element_type=F32) + b_ref[0]


def _modulation(cvec, w_mod, b_mod):
    rows, d = cvec.shape
    n_layers, _, n_out = w_mod.shape
    tn = 1536
    return pl.pallas_call(
        _mod_kernel,
        grid=(n_layers, n_out // tn),
        in_specs=[
            pl.BlockSpec((rows, d), lambda l, j: (0, 0)),
            pl.BlockSpec((1, d, tn), lambda l, j: (l, 0, j)),
            pl.BlockSpec((1, 1, tn), lambda l, j: (l, 0, j)),
        ],
        out_specs=pl.BlockSpec((1, rows, tn), lambda l, j: (l, 0, j)),
        out_shape=jax.ShapeDtypeStruct((n_layers, rows, n_out), F32),
        compiler_params=_cparams(("arbitrary", "arbitrary")),
        name="modulation",
    )(cvec, w_mod, b_mod.reshape(n_layers, 1, n_out))


def _head_norm_rope(p, gain, cos_t, sin_t, seg, lo32):
    ss = p * p
    hi = ss.astype(BF16)
    lo = (ss - hi.astype(F32)).astype(BF16)
    mean = jnp.dot(hi, seg, preferred_element_type=F32) + jnp.dot(lo, seg, preferred_element_type=F32)
    n = p * lax.rsqrt(mean + EPS) * gain
    partner = jnp.where(lo32, pltpu.roll(n, 96, 1), pltpu.roll(n, 32, 1))
    return n * cos_t + partner * sin_t


def _inproj_kernel(x_ref, sh_ref, sc_ref, g_ref, w_ref, bm_ref, qgain_ref, kgain_ref, cos_ref, sin_ref, seg_ref,
                   eye_ref, xr_o, gr_o, qg_o, kg_o, vg_o, qw_o, kw_o, vw_o, mg_o, *, n_ctx_tiles):
    i = pl.program_id(0)
    is_lat = i >= n_ctx_tiles
    h = _rms_mod(x_ref[...], g_ref[...], sc_ref[0], sh_ref[0])
    hb = h.astype(BF16)
    tm = hb.shape[0]

    def proj(c0, width):
        return jnp.dot(hb, w_ref[:, c0:c0 + width], preferred_element_type=F32)

    lane = lax.broadcasted_iota(I32, (tm, LANES), 1)
    lo32 = (lane & (HEAD_DIM - 1)) < (HEAD_DIM // 2)
    lo64 = lane < HEAD_DIM
    cos_t = jnp.where(is_lat, cos_ref[...], 1.0)
    sin_t = jnp.where(is_lat, sin_ref[...], 0.0)
    seg = seg_ref[...]
    zero = jnp.zeros((tm, LANES), F32)
    d = x_ref.shape[1]

    def rnn_epilogue(p):
        xr_o[...] = p[:, :RNN_WIDTH].astype(BF16)
        gr_o[...] = jax.nn.gelu(p[:, RNN_WIDTH:]).astype(BF16)

    def q_epilogue(a, q_o, p):
        qgain = qgain_ref[a:a + 1, :]
        for c in range(ATTN_WIDTH // LANES):
            y = _head_norm_rope(p[:, c * LANES:(c + 1) * LANES], qgain, cos_t, sin_t, seg, lo32)
            y = y * (HEAD_DIM ** -0.5 * LOG2E)
            yr = pltpu.roll(y, HEAD_DIM, 1)
            if c < 2:
                out_a, out_b = jnp.where(lo64, y, zero), jnp.where(lo64, yr, zero)
            else:
                out_a, out_b = jnp.where(lo64, zero, yr), jnp.where(lo64, zero, y)
            q_o[2 * c] = out_a.astype(BF16)
            q_o[2 * c + 1] = out_b.astype(BF16)

    def kv_epilogue(a, k_o, v_o, p):
        k_o[...] = _head_norm_rope(p[:, :KV_WIDTH], kgain_ref[a:a + 1, :], cos_t, sin_t, seg, lo32).astype(BF16)
        pv = p[:, KV_WIDTH:]
        for hd, vh in enumerate((jnp.where(lo64, pv, 1.0).astype(BF16), jnp.where(lo64, 1.0, pv).astype(BF16))):
            if a == 0:
                vh = lax.dot_general(eye_ref[...], vh, (((1,), (1,)), ((), ())),
                                     preferred_element_type=F32).astype(BF16)
            v_o[hd] = vh

    def gate_epilogue(b, p):
        mg_o[:, b * d:(b + 1) * d] = _sigmoid(p + bm_ref[:, b * d:(b + 1) * d]).astype(BF16)

    groups = [(0, 2 * RNN_WIDTH, rnn_epilogue)]
    base = 2 * RNN_WIDTH
    for a, (q_o, k_o, v_o) in enumerate(((qg_o, kg_o, vg_o), (qw_o, kw_o, vw_o))):
        groups.append((base, ATTN_WIDTH, functools.partial(q_epilogue, a, q_o)))
        groups.append((base + ATTN_WIDTH, 2 * KV_WIDTH, functools.partial(kv_epilogue, a, k_o, v_o)))
        base += ATTN_WIDTH + 2 * KV_WIDTH
    for b in range(N_BRANCHES):
        groups.append((base + b * d, d, functools.partial(gate_epilogue, b)))

    p_next = proj(groups[0][0], groups[0][1])
    for n, (_, _, epilogue) in enumerate(groups):
        p_cur = p_next
        if n + 1 < len(groups):
            p_next = proj(groups[n + 1][0], groups[n + 1][1])
        epilogue(p_cur)


def _input_projection(x_all, mod_sh, mod_sc, g, w_in, b_merge, qgain, kgain, cos_t, sin_t, seg, eye, *, n_ctx, seq, tm):
    ntok, d = x_all.shape
    n_in = w_in.shape[1]
    nct = n_ctx // tm
    tps = seq // tm

    def mod_idx(i):
        return (jnp.where(i >= nct, 1 + (i - nct) // tps, 0), 0, 0)

    def rope_idx(i):
        return (jnp.where(i >= nct, (i - nct) % tps, 0), 0)

    row = lambda i: (i, 0)
    fixed = lambda i: (0, 0)
    out_shape = (
        jax.ShapeDtypeStruct((ntok, RNN_WIDTH), BF16),
        jax.ShapeDtypeStruct((ntok, RNN_WIDTH), BF16),
        jax.ShapeDtypeStruct((N_Q_HEADS, ntok, LANES), BF16),
        jax.ShapeDtypeStruct((ntok, KV_WIDTH), BF16),
        jax.ShapeDtypeStruct((N_KV_HEADS, LANES, ntok), BF16),
        jax.ShapeDtypeStruct((N_Q_HEADS, ntok, LANES), BF16),
        jax.ShapeDtypeStruct((ntok, KV_WIDTH), BF16),
        jax.ShapeDtypeStruct((N_KV_HEADS, ntok, LANES), BF16),
        jax.ShapeDtypeStruct((ntok, N_BRANCHES * d), BF16),
    )
    q_spec = pl.BlockSpec((N_Q_HEADS, tm, LANES), lambda i: (0, i, 0))
    out_specs = (
        pl.BlockSpec((tm, RNN_WIDTH), row), pl.BlockSpec((tm, RNN_WIDTH), row),
        q_spec, pl.BlockSpec((tm, KV_WIDTH), row), pl.BlockSpec((N_KV_HEADS, LANES, tm), lambda i: (0, 0, i)),
        q_spec, pl.BlockSpec((tm, KV_WIDTH), row), pl.BlockSpec((N_KV_HEADS, tm, LANES), lambda i: (0, i, 0)),
        pl.BlockSpec((tm, N_BRANCHES * d), row),
    )
    return pl.pallas_call(
        functools.partial(_inproj_kernel, n_ctx_tiles=nct),
        grid=(ntok // tm,),
        in_specs=[
            pl.BlockSpec((tm, d), row),
            pl.BlockSpec((1, 1, d), mod_idx),
            pl.BlockSpec((1, 1, d), mod_idx),
            pl.BlockSpec((1, d), fixed),
            pl.BlockSpec((d, n_in), fixed, pipeline_mode=pl.Buffered(1)),
            pl.BlockSpec((1, N_BRANCHES * d), fixed),
            pl.BlockSpec((2, LANES), fixed),
            pl.BlockSpec((2, LANES), fixed),
            pl.BlockSpec((tm, LANES), rope_idx),
            pl.BlockSpec((tm, LANES), rope_idx),
            pl.BlockSpec((LANES, LANES), fixed),
            pl.BlockSpec((LANES, LANES), fixed),
        ],
        out_specs=out_specs,
        out_shape=out_shape,
        compiler_params=_cparams(("arbitrary",)),
        name="input_projection",
    )(x_all, mod_sh, mod_sc, g, w_in, b_merge, qgain, kgain, cos_t, sin_t, seg, eye)


def _rglru_kernel(x_ref, prev_ref, next_ref, cw_ref, cb_ref, wa_ref, wx_ref, ba_ref, bx_ref, lam_ref,
                  o_ref, a_scr, b_scr, h_scr, y_scr, pe_scr, he_scr, c_scr, *, n_chunks):
    d = pl.program_id(0)
    j = pl.program_id(2)
    jj = jnp.where(d == 0, j, jnp.where(j == 0, 0, n_chunks - j))
    prev_valid = jj >= 2
    next_valid = (jj >= 1) & (jj <= n_chunks - 2)

    x = x_ref[...].astype(F32)
    tc = x.shape[0]
    hp = prev_ref[...].astype(F32)
    hn = next_ref[...].astype(F32)
    halo = prev_ref.shape[0]
    pm1 = jnp.where(prev_valid, hp[halo - 1:halo, :], 0.0)
    pm2 = jnp.where(prev_valid, hp[halo - 2:halo - 1, :], 0.0)
    nx0 = jnp.where(next_valid, hn[0:1, :], 0.0)
    row = lax.broadcasted_iota(I32, x.shape, 0)
    xm1 = jnp.where(row == 0, pm1, pltpu.roll(x, 1, 0))
    xm2 = jnp.where(row == 0, pm2, jnp.where(row == 1, pm1, pltpu.roll(x, 2, 0)))
    xp1 = jnp.where(row == tc - 1, nx0, pltpu.roll(x, tc - 1, 0))
    xc = (xm2 * cw_ref[0:1, :] + xm1 * cw_ref[1:2, :] + x * cw_ref[2:3, :] + xp1 * cw_ref[3:4, :]) + cb_ref[...]

    xb = xc.astype(BF16)
    r = _sigmoid(jnp.dot(xb, wa_ref[0], preferred_element_type=F32) + ba_ref[0])
    g = _sigmoid(jnp.dot(xb, wx_ref[0], preferred_element_type=F32) + bx_ref[0])
    neg_lam = -lam_ref[0]
    softplus = jnp.maximum(neg_lam, 0.0) + jnp.log(1.0 + jnp.exp(-jnp.abs(neg_lam)))
    a = jnp.exp(-LRU_C * r * softplus)
    bb = jnp.sqrt(1.0 - a * a) * (g * xc)
    n_lt = a.shape[1] // LANES
    for lt in range(n_lt):
        a_scr[lt * tc:(lt + 1) * tc, :] = a[:, lt * LANES:(lt + 1) * LANES]
        b_scr[lt * tc:(lt + 1) * tc, :] = bb[:, lt * LANES:(lt + 1) * LANES]

    @pl.when(j == 0)
    def _():
        h_scr[...] = jnp.zeros_like(h_scr)

    ng = tc // SUBLANES

    def scan(reverse):
        order = range(SUBLANES - 1, -1, -1) if reverse else range(SUBLANES)
        prod, loc = {}, {}
        for lt in range(n_lt):
            p = hl = None
            for k in order:
                ak = a_scr[pl.ds(lt * tc + k, ng, stride=SUBLANES), :]
                bk = b_scr[pl.ds(lt * tc + k, ng, stride=SUBLANES), :]
                p, hl = (ak, bk) if p is None else (ak * p, ak * hl + bk)
                prod[lt, k], loc[lt, k] = p, hl
            pe_scr[lt * ng:(lt + 1) * ng, :] = p
            he_scr[lt * ng:(lt + 1) * ng, :] = hl
        c = [h_scr[lt:lt + 1, :] for lt in range(n_lt)]
        for g in (range(ng - 1, -1, -1) if reverse else range(ng)):
            for lt in range(n_lt):
                r = lt * ng + g
                c_scr[r:r + 1, :] = c[lt]
                c[lt] = pe_scr[r:r + 1, :] * c[lt] + he_scr[r:r + 1, :]
        for lt in range(n_lt):
            h_scr[lt:lt + 1, :] = c[lt]
            carry_in = c_scr[lt * ng:(lt + 1) * ng, :]
            for k in order:
                y_scr[pl.ds(lt * tc + k, ng, stride=SUBLANES), :] = loc[lt, k] + prod[lt, k] * carry_in

    pl.when(d == 0)(lambda: scan(False))
    pl.when(d != 0)(lambda: scan(True))
    for lt in range(n_lt):
        o_ref[0, :, lt * LANES:(lt + 1) * LANES] = y_scr[lt * tc:(lt + 1) * tc, :].astype(BF16)


def _rglru(xr, conv_w, conv_b, wa, wx, ba, bx, lam, *, batch, n_ctx_per, seq):
    ntok, width = xr.shape
    tc = n_ctx_per
    halo = 16
    n_chunks = 1 + seq // tc
    n_lt = width // LANES
    lat0 = batch

    def seq_chunk(d, j):
        return jnp.where(d == 0, j, jnp.where(j == 0, 0, n_chunks - j))

    def chunk_blk(d, b, j):
        jj = seq_chunk(d, j)
        return jnp.where(jj == 0, b, lat0 + b * (seq // tc) + jj - 1)

    def x_idx(d, b, j):
        return (chunk_blk(d, b, j), 0)

    def prev_idx(d, b, j):
        return (jnp.maximum(chunk_blk(d, b, j) * (tc // halo) - 1, 0), 0)

    def next_idx(d, b, j):
        return (jnp.minimum((chunk_blk(d, b, j) + 1) * (tc // halo), ntok // halo - 1), 0)

    per_dir = lambda d, b, j: (d, 0, 0)
    fixed = lambda d, b, j: (0, 0)
    return pl.pallas_call(
        functools.partial(_rglru_kernel, n_chunks=n_chunks),
        grid=(2, batch, n_chunks),
        in_specs=[
            pl.BlockSpec((tc, width), x_idx),
            pl.BlockSpec((halo, width), prev_idx),
            pl.BlockSpec((halo, width), next_idx),
            pl.BlockSpec((CONV_WIDTH, width), fixed),
            pl.BlockSpec((1, width), fixed),
            pl.BlockSpec((1, width, width), per_dir),
            pl.BlockSpec((1, width, width), per_dir),
            pl.BlockSpec((1, 1, width), per_dir),
            pl.BlockSpec((1, 1, width), per_dir),
            pl.BlockSpec((1, 1, width), per_dir),
        ],
        out_specs=pl.BlockSpec((1, tc, width), lambda d, b, j: (d, chunk_blk(d, b, j), 0)),
        out_shape=jax.ShapeDtypeStruct((2, ntok, width), BF16),
        scratch_shapes=[
            pltpu.VMEM((n_lt * tc, LANES), F32), pltpu.VMEM((n_lt * tc, LANES), F32),
            pltpu.VMEM((n_lt, LANES), F32), pltpu.VMEM((n_lt * tc, LANES), F32),
            pltpu.VMEM((n_lt * tc // SUBLANES, LANES), F32), pltpu.VMEM((n_lt * tc // SUBLANES, LANES), F32),
            pltpu.VMEM((n_lt * tc // SUBLANES, LANES), F32),
        ],
        compiler_params=_cparams(("arbitrary", "arbitrary", "arbitrary")),
        name="rglru",
    )(xr, xr, xr, conv_w, conv_b, wa, wx, ba, bx, lam)


LOOKAHEAD = 2


def _col_max(s):
    parts = [s[r:r + 8] for r in range(0, s.shape[0], 8)]
    while len(parts) > 1:
        nxt = [jnp.maximum(parts[k], parts[k + 1]) for k in range(0, len(parts) - 1, 2)]
        if len(parts) % 2:
            nxt.append(parts[-1])
        parts = nxt
    return jnp.max(parts[0], axis=0, keepdims=True)


def _attn_kernel(sink_ref, q_ref, kc_ref, vc_ref, kl_ref, vl_ref, eye_ref, o_ref, *,
                 band, use_sink, tile_off, n_ctx_tiles, tiles_per_seq, tk, tq):
    i = pl.program_id(0) + tile_off
    is_lat = i >= n_ctx_tiles
    n_sub = q_ref.shape[1] // tq
    qi0 = jnp.where(is_lat, (i - n_ctx_tiles) % tiles_per_seq, 0) * n_sub
    nq = tiles_per_seq * n_sub
    nq_rows = GQA_GROUP * tq
    seq = kl_ref.shape[0]
    contract_last = (((1,), (1,)), ((), ()))
    lo64 = lax.broadcasted_iota(I32, (tq, LANES), 1) < HEAD_DIM
    chan = lax.broadcasted_iota(I32, (LANES, nq_rows), 0)

    def transposed(v):
        return lax.dot_general(eye_ref[...], v, contract_last, preferred_element_type=F32).astype(BF16)

    def scores(q, k, mask):
        s = lax.dot_general(k, q, contract_last, preferred_element_type=F32)
        return s if mask is None else jnp.where(mask, s, NEG_INF)

    def absorb(carry, s, vt):
        m, acc = carry
        m_new = jnp.maximum(m, _col_max(s))
        p = jnp.exp2(s - m_new)
        acc = jnp.exp2(m - m_new) * acc + jnp.dot(vt, p.astype(BF16), preferred_element_type=F32)
        return m_new, acc

    def run(latent):
        qs, carries, blocks = [], [], []
        for u in range(n_sub):
            if latent and band:
                qi = qi0 + u
                starts = (jnp.maximum(qi - 1, 0) * tq, qi * tq, jnp.minimum(qi + 1, nq - 1) * tq)
                starts = [pl.multiple_of(s0, tq) for s0 in starts]
                kpos = lax.broadcasted_iota(I32, (3 * tq, nq_rows), 0)
                qpos = lax.broadcasted_iota(I32, (3 * tq, nq_rows), 1) & (tq - 1)
                rel = kpos - tq - qpos
                kmin = jnp.where(qi > 0, 0, tq)
                kmax = jnp.where(qi < nq - 1, 3 * tq, 2 * tq)
                band_ok = (rel >= -WINDOW) & (rel <= WINDOW) & (kpos >= kmin) & (kpos < kmax)
            for h in range(N_KV_HEADS):
                qs.append(q_ref[GQA_GROUP * h:GQA_GROUP * (h + 1), u * tq:(u + 1) * tq, :].reshape(nq_rows, LANES))
                ones_rows = (chan >= HEAD_DIM) if h == 0 else (chan < HEAD_DIM)
                if use_sink:
                    m0 = jnp.concatenate(
                        [jnp.full((1, tq), sink_ref[GQA_GROUP * h + g] * LOG2E, F32) for g in range(GQA_GROUP)],
                        axis=1)
                    acc0 = jnp.where(ones_rows, 1.0, 0.0)
                else:
                    m0 = jnp.full((1, nq_rows), NEG_INF, F32)
                    acc0 = jnp.zeros((LANES, nq_rows), F32)
                carries.append((m0, acc0))
                blk = [(lambda: kc_ref[...],
                        (lambda h=h: transposed(vc_ref[h])) if band else (lambda h=h: vc_ref[h]), None)]
                if latent and band:
                    blk.append((
                        lambda starts=starts: jnp.concatenate([kl_ref[pl.ds(s0, tq), :] for s0 in starts], axis=0),
                        lambda starts=starts, h=h: transposed(
                            jnp.concatenate([vl_ref[h, pl.ds(s0, tq), :] for s0 in starts], axis=0)),
                        band_ok))
                elif latent:
                    for c in range(seq // tk):
                        blk.append((lambda c=c: kl_ref[c * tk:(c + 1) * tk, :],
                                    lambda c=c, h=h: vl_ref[h, :, c * tk:(c + 1) * tk], None))
                blocks.append(blk)

        n_streams = len(qs)
        items = [(st, c) for c in range(len(blocks[0])) for st in range(n_streams)]

        def issue(item):
            st, c = item
            return scores(qs[st], blocks[st][c][0](), blocks[st][c][2])

        ahead = [issue(it) for it in items[:LOOKAHEAD]]
        for idx, (st, c) in enumerate(items):
            s_cur = ahead.pop(0)
            if idx + LOOKAHEAD < len(items):
                ahead.append(issue(items[idx + LOOKAHEAD]))
            carries[st] = absorb(carries[st], s_cur, blocks[st][c][1]())

        for st in range(n_streams):
            u, h = divmod(st, N_KV_HEADS)
            _, acc = carries[st]
            denom = acc[HEAD_DIM:HEAD_DIM + 1, :] if h == 0 else acc[0:1, :]
            o = jnp.transpose(acc * (1.0 / denom))
            og = [o[g * tq:(g + 1) * tq] for g in range(GQA_GROUP)]
            for c in range(GQA_GROUP // 2):
                e, f = og[2 * c], og[2 * c + 1]
                if h == 0:
                    chunk = jnp.where(lo64, e, pltpu.roll(f, HEAD_DIM, 1))
                else:
                    chunk = jnp.where(lo64, pltpu.roll(e, HEAD_DIM, 1), f)
                col = (h * (GQA_GROUP // 2) + c) * LANES
                o_ref[u * tq:(u + 1) * tq, col:col + LANES] = chunk.astype(BF16)

    if tile_off == 0:
        pl.when(is_lat)(lambda: run(True))
        pl.when(jnp.logical_not(is_lat))(lambda: run(False))
    else:
        run(True)


def _attention(sink, q, k, v, eye, *, band, use_sink, batch, n_ctx_per, seq, with_ctx_queries, tq, tk, n_sub):
    ntok = k.shape[0]
    n_ctx = batch * n_ctx_per
    assert n_ctx % seq == 0, "context rows must cover whole latent-sequence blocks"
    assert tq == WINDOW
    rows = n_sub * tq
    assert n_ctx_per % rows == 0 and seq % rows == 0
    nct = n_ctx // rows
    tps = seq // rows
    tile_off = 0 if with_ctx_queries else nct
    n_tiles = ntok // rows - tile_off

    def batch_of(t):
        i = t + tile_off
        return jnp.where(i >= nct, (i - nct) // tps, (i * rows) // n_ctx_per)

    ctx_idx = lambda t, *_: (batch_of(t), 0)
    lat_idx = lambda t, *_: (n_ctx // seq + batch_of(t), 0)
    if band:
        vc_spec = pl.BlockSpec((N_KV_HEADS, n_ctx_per, LANES), lambda t, *_: (0, batch_of(t), 0))
        vl_spec = pl.BlockSpec((N_KV_HEADS, seq, LANES), lambda t, *_: (0, n_ctx // seq + batch_of(t), 0))
    else:
        vc_spec = pl.BlockSpec((N_KV_HEADS, LANES, n_ctx_per), lambda t, *_: (0, 0, batch_of(t)))
        vl_spec = pl.BlockSpec((N_KV_HEADS, LANES, seq), lambda t, *_: (0, 0, n_ctx // seq + batch_of(t)))
    grid_spec = pltpu.PrefetchScalarGridSpec(
        num_scalar_prefetch=1,
        grid=(n_tiles,),
        in_specs=[
            pl.BlockSpec((N_Q_HEADS, rows, LANES), lambda t, *_: (0, t + tile_off, 0)),
            pl.BlockSpec((n_ctx_per, KV_WIDTH), ctx_idx),
            vc_spec,
            pl.BlockSpec((seq, KV_WIDTH), lat_idx),
            vl_spec,
            pl.BlockSpec((LANES, LANES), lambda t, *_: (0, 0)),
        ],
        out_specs=pl.BlockSpec((rows, ATTN_WIDTH), lambda t, *_: (t, 0)),
    )
    return pl.pallas_call(
        functools.partial(_attn_kernel, band=band, use_sink=use_sink, tile_off=tile_off,
                          n_ctx_tiles=nct, tiles_per_seq=tps, tk=tk, tq=tq),
        grid_spec=grid_spec,
        out_shape=jax.ShapeDtypeStruct((n_tiles * rows, ATTN_WIDTH), BF16),
        compiler_params=_cparams(("arbitrary",)),
        name="window_attention" if band else "global_attention",
    )(sink, q, k, v, k, v, eye)


def _merge_kernel(x_ref, h_ref, gr_ref, yg_ref, yw_ref, mg_ref, wb_ref, wo_ref, g1_ref, n2_ref, sh2_ref, sc2_ref,
                  rwh_ref, rwl_ref, x1_o, h2_o, lg_o):
    tm, d = x_ref.shape
    n_parts = 2 if tm % 256 == 0 else 1
    part = tm // n_parts
    rows = [slice(k * part, (k + 1) * part) for k in range(n_parts)]

    def branches(r):
        y_rnn = ((h_ref[0, r, :].astype(F32) + h_ref[1, r, :].astype(F32)) * gr_ref[r, :].astype(F32)).astype(BF16)
        ys = (y_rnn, yg_ref[r, :], yw_ref[r, :])
        merged = None
        for b in range(N_BRANCHES):
            t = mg_ref[r, b * d:(b + 1) * d].astype(F32) * jnp.dot(ys[b], wb_ref[b], preferred_element_type=F32)
            merged = t if merged is None else merged + t
        return merged.astype(BF16)

    merged = [branches(r) for r in rows]
    outs = [jnp.dot(m, wo_ref[...], preferred_element_type=F32) for m in merged]
    h2s = []
    for r, out in zip(rows, outs):
        x1 = x_ref[r, :] + g1_ref[0] * out
        x1_o[r, :] = x1
        h2 = _rms_mod(x1, n2_ref[...], sc2_ref[0], sh2_ref[0])
        h2_o[r, :] = h2
        h2s.append(h2)
    for r, h2 in zip(rows, h2s):
        hi = h2.astype(BF16)
        lo = (h2 - hi.astype(F32)).astype(BF16)
        lg = jnp.dot(hi, rwh_ref[...], preferred_element_type=F32) + (
            jnp.dot(lo, rwh_ref[...], preferred_element_type=F32) + jnp.dot(hi, rwl_ref[...], preferred_element_type=F32))
        lg_o[:, r] = jnp.transpose(lg)[:N_EXPERTS, :]


def _merge(x_all, hfb, gr, yg, yw, mg, wb, wo, mod_g1, n2, mod_sh2, mod_sc2, rw_hi, rw_lo, *, n_ctx, seq, with_ctx, tm):
    ntok, d = x_all.shape
    nct = n_ctx // tm
    tps = seq // tm
    off = 0 if with_ctx else nct
    n_tiles = ntok // tm - off
    n_out = n_tiles * tm

    def mod_idx(t):
        i = t + off
        return (jnp.where(i >= nct, 1 + (i - nct) // tps, 0), 0, 0)

    row_in = lambda t: (t + off, 0)
    row_out = lambda t: (t, 0)
    fixed2 = lambda t: (0, 0)
    return pl.pallas_call(
        _merge_kernel,
        grid=(n_tiles,),
        in_specs=[
            pl.BlockSpec((tm, d), row_in),
            pl.BlockSpec((2, tm, RNN_WIDTH), lambda t: (0, t + off, 0)),
            pl.BlockSpec((tm, RNN_WIDTH), row_in),
            pl.BlockSpec((tm, ATTN_WIDTH), row_out),
            pl.BlockSpec((tm, ATTN_WIDTH), row_out),
            pl.BlockSpec((tm, N_BRANCHES * d), row_in),
            pl.BlockSpec((N_BRANCHES, RNN_WIDTH, d), lambda t: (0, 0, 0)),
            pl.BlockSpec((d, d), fixed2),
            pl.BlockSpec((1, 1, d), mod_idx),
            pl.BlockSpec((1, d), fixed2),
            pl.BlockSpec((1, 1, d), mod_idx),
            pl.BlockSpec((1, 1, d), mod_idx),
            pl.BlockSpec((d, LANES), fixed2),
            pl.BlockSpec((d, LANES), fixed2),
        ],
        out_specs=(
            pl.BlockSpec((tm, d), row_out),
            pl.BlockSpec((tm, d), row_out),
            pl.BlockSpec((N_EXPERTS, tm), lambda t: (0, t)),
        ),
        out_shape=(
            jax.ShapeDtypeStruct((n_out, d), F32),
            jax.ShapeDtypeStruct((n_out, d), F32),
            jax.ShapeDtypeStruct((N_EXPERTS, n_out), F32),
        ),
        compiler_params=_cparams(("arbitrary",)),
        name="merge",
    )(x_all, hfb, gr, yg, yw, mg, wb, wo, mod_g1, n2, mod_sh2, mod_sc2, rw_hi, rw_lo)


def _first_argmax(vals):
    best, idx = vals[0], jnp.zeros(vals[0].shape, I32)
    for k in range(1, len(vals)):
        take = vals[k] > best
        best = jnp.where(take, vals[k], best)
        idx = jnp.where(take, k, idx)
    return best, idx


def _routing_kernel(lg_ref, rb_ref, tri_ref, bucket_o, rank_o, count_o, carry_scr):
    t = pl.program_id(0)

    @pl.when(t == 0)
    def _():
        carry_scr[...] = jnp.zeros_like(carry_scr)

    s = _sigmoid(lg_ref[...])
    sel = s + rb_ref[...]
    rows = [sel[e:e + 1, :] for e in range(N_EXPERTS)]
    tn = s.shape[1]

    grp_scores = []
    for g in range(N_GROUPS):
        r = rows[g * EXPERTS_PER_GROUP:(g + 1) * EXPERTS_PER_GROUP]
        best = None
        for a in range(EXPERTS_PER_GROUP):
            for b in range(a + 1, EXPERTS_PER_GROUP):
                pair = r[a] + r[b]
                best = pair if best is None else jnp.maximum(best, pair)
        grp_scores.append(best)
    _, grp = _first_argmax(grp_scores)

    vals = []
    for k in range(EXPERTS_PER_GROUP):
        v = rows[k]
        for g in range(1, N_GROUPS):
            v = jnp.where(grp == g, rows[g * EXPERTS_PER_GROUP + k], v)
        vals.append(v)
    _, i1 = _first_argmax(vals)
    _, i2 = _first_argmax([jnp.where(i1 == k, -jnp.inf, vals[k]) for k in range(EXPERTS_PER_GROUP)])
    lo = jnp.minimum(i1, i2)
    hi = jnp.maximum(i1, i2)
    pair = jnp.where(lo == 0, hi - 1, jnp.where(lo == 1, 6 - hi, 5))
    bucket = grp * N_PAIRS + pair
    bucket_o[...] = bucket

    onehot = (lax.broadcasted_iota(I32, (BUCKET_ROWS, tn), 0) == bucket).astype(F32)
    incl = jnp.dot(onehot.astype(BF16), tri_ref[...], preferred_element_type=F32)
    before = carry_scr[...] + incl - 1.0
    rank_o[...] = jnp.sum(onehot * before, axis=0, keepdims=True).astype(I32)
    carry_scr[...] = carry_scr[...] + jnp.sum(onehot, axis=1, keepdims=True)
    count_o[...] = jnp.broadcast_to(carry_scr[...], count_o.shape).astype(I32)


def _routing(logits_t, router_b, tri, *, tn):
    n = logits_t.shape[1]
    return pl.pallas_call(
        _routing_kernel,
        grid=(n // tn,),
        in_specs=[
            pl.BlockSpec((N_EXPERTS, tn), lambda t: (0, t)),
            pl.BlockSpec((N_EXPERTS, 1), lambda t: (0, 0)),
            pl.BlockSpec((tn, tn), lambda t: (0, 0)),
        ],
        out_specs=(
            pl.BlockSpec((1, tn), lambda t: (0, t)),
            pl.BlockSpec((1, tn), lambda t: (0, t)),
            pl.BlockSpec((BUCKET_ROWS, LANES), lambda t: (0, 0)),
        ),
        out_shape=(
            jax.ShapeDtypeStruct((1, n), I32),
            jax.ShapeDtypeStruct((1, n), I32),
            jax.ShapeDtypeStruct((BUCKET_ROWS, LANES), I32),
        ),
        scratch_shapes=[pltpu.VMEM((BUCKET_ROWS, 1), F32)],
        compiler_params=_cparams(("arbitrary",)),
        name="routing",
    )(logits_t, router_b, tri)


def _position_kernel(bucket_ref, rank_ref, base_ref, pos_o):
    bucket = bucket_ref[...]
    ids = lax.broadcasted_iota(I32, (BUCKET_ROWS, bucket.shape[1]), 0)
    base = jnp.sum(jnp.where(ids == bucket, base_ref[...], 0), axis=0, keepdims=True)
    pos_o[...] = base + rank_ref[...]


def _positions(bucket, rank, base, *, tn):
    n = bucket.shape[1]
    blk = pl.BlockSpec((1, tn), lambda t: (0, t))
    return pl.pallas_call(
        _position_kernel,
        grid=(n // tn,),
        in_specs=[blk, blk, pl.BlockSpec((BUCKET_ROWS, 1), lambda t: (0, 0))],
        out_specs=blk,
        out_shape=jax.ShapeDtypeStruct((1, n), I32),
        compiler_params=_cparams(("arbitrary",)),
        name="positions",
    )(bucket, rank, base)


def _row_copy_wait(src, dst, sem, n):
    def wait(r, _):
        pltpu.make_async_copy(src.at[pl.ds(0, 1), :], dst.at[pl.ds(0, 1), :], sem).wait()
        return 0

    lax.fori_loop(0, n, wait, 0, unroll=8)


def _dispatch_kernel(pos_ref, h_ref, init_ref, xs_ref, stage, sem):
    del init_ref
    t = pl.program_id(0)
    tm = h_ref.shape[0]
    slot = t % 2
    stage[slot] = h_ref[...]

    def issue(r, _):
        p = pos_ref[0, 0, r]
        pltpu.make_async_copy(stage.at[slot, pl.ds(r, 1), :], xs_ref.at[pl.ds(p, 1), :], sem.at[slot]).start()
        return 0

    lax.fori_loop(0, tm, issue, 0, unroll=8)

    @pl.when(t > 0)
    def _():
        _row_copy_wait(stage.at[1 - slot], xs_ref, sem.at[1 - slot], tm)

    @pl.when(t == pl.num_programs(0) - 1)
    def _():
        _row_copy_wait(stage.at[slot], xs_ref, sem.at[slot], tm)


def _dispatch(pos3, h2, xs_init, *, tm):
    n, d = h2.shape
    return pl.pallas_call(
        _dispatch_kernel,
        grid=(n // tm,),
        in_specs=[
            pl.BlockSpec((1, 1, tm), lambda t: (t, 0, 0), memory_space=pltpu.SMEM),
            pl.BlockSpec((tm, d), lambda t: (t, 0)),
            pl.BlockSpec(memory_space=pl.ANY),
        ],
        out_specs=pl.BlockSpec(memory_space=pl.ANY),
        out_shape=jax.ShapeDtypeStruct(xs_init.shape, F32),
        scratch_shapes=[pltpu.VMEM((2, tm, d), F32), pltpu.SemaphoreType.DMA((2,))],
        input_output_aliases={2: 0},
        compiler_params=_cparams(("arbitrary",)),
        name="dispatch",
    )(pos3, h2, xs_init)


def _moe_kernel(ea_ref, eb_ref, act_ref, xs_ref, w1a, w3a, w2a, w1b, w3b, w2b, rwa, rwb, ys_ref):
    j = pl.program_id(0)

    @pl.when(act_ref[j] == 0)
    def _():
        ys_ref[...] = jnp.zeros_like(ys_ref)

    @pl.when(act_ref[j] != 0)
    def _():
        x = xs_ref[...]
        xb = x.astype(BF16)

        def gated(u, g):
            return ((u * _sigmoid(u)) * g).astype(BF16)

        ua = jnp.dot(xb, w1a[0], preferred_element_type=F32)
        ga = jnp.dot(xb, w3a[0], preferred_element_type=F32)
        ub = jnp.dot(xb, w1b[0], preferred_element_type=F32)
        gb = jnp.dot(xb, w3b[0], preferred_element_type=F32)
        ya = jnp.dot(gated(ua, ga), w2a[0], preferred_element_type=F32)
        yb = jnp.dot(gated(ub, gb), w2b[0], preferred_element_type=F32)
        sa = _sigmoid(jnp.sum(x * rwa[0], axis=-1, keepdims=True))
        sb = _sigmoid(jnp.sum(x * rwb[0], axis=-1, keepdims=True))
        inv = 1.0 / (sa + sb)
        ys_ref[...] = (sa * inv) * ya + (sb * inv) * yb


def _moe(ea, eb, act, xs, w1, w3, w2, rw3, *, tm):
    npad, d = xs.shape
    de = w1.shape[2]
    row = lambda j, *_: (j, 0)
    wa = lambda j, ea, eb, act: (ea[j], 0, 0)
    wb = lambda j, ea, eb, act: (eb[j], 0, 0)
    grid_spec = pltpu.PrefetchScalarGridSpec(
        num_scalar_prefetch=3,
        grid=(npad // tm,),
        in_specs=[
            pl.BlockSpec((tm, d), row),
            pl.BlockSpec((1, d, de), wa), pl.BlockSpec((1, d, de), wa), pl.BlockSpec((1, de, d), wa),
            pl.BlockSpec((1, d, de), wb), pl.BlockSpec((1, d, de), wb), pl.BlockSpec((1, de, d), wb),
            pl.BlockSpec((1, 1, d), wa), pl.BlockSpec((1, 1, d), wb),
        ],
        out_specs=pl.BlockSpec((tm, d), row),
    )
    return pl.pallas_call(
        _moe_kernel,
        grid_spec=grid_spec,
        out_shape=jax.ShapeDtypeStruct((npad, d), F32),
        compiler_params=_cparams(("arbitrary",)),
        name="expert_ffn",
    )(ea, eb, act, xs, w1, w3, w2, w1, w3, w2, rw3, rw3)


def _combine_kernel(pos_ref, nxt_ref, x_ref, g2_ref, ys_ref, o_ref, buf, sem):
    t = pl.program_id(0)
    tm = x_ref.shape[0]
    slot = t % 2

    def gather(idx_ref, s):
        def issue(r, _):
            p = idx_ref[0, 0, r]
            pltpu.make_async_copy(ys_ref.at[pl.ds(p, 1), :], buf.at[s, pl.ds(r, 1), :], sem.at[s]).start()
            return 0

        lax.fori_loop(0, tm, issue, 0, unroll=8)

    @pl.when(t == 0)
    def _():
        gather(pos_ref, 0)

    @pl.when(t + 1 < pl.num_programs(0))
    def _():
        gather(nxt_ref, 1 - slot)

    _row_copy_wait(ys_ref, buf.at[slot], sem.at[slot], tm)
    o_ref[...] = x_ref[...] + g2_ref[0] * buf[slot]


def _combine(pos3, x1, mod_g2, ys, *, n_ctx, seq, with_ctx, tm):
    n, d = x1.shape
    nct = n_ctx // tm if with_ctx else 0
    tps = seq // tm

    def mod_idx(t):
        return (jnp.where(t >= nct, 1 + (t - nct) // tps, 0), 0, 0)

    return pl.pallas_call(
        _combine_kernel,
        grid=(n // tm,),
        in_specs=[
            pl.BlockSpec((1, 1, tm), lambda t: (t, 0, 0), memory_space=pltpu.SMEM),
            pl.BlockSpec((1, 1, tm), lambda t: (jnp.minimum(t + 1, n // tm - 1), 0, 0), memory_space=pltpu.SMEM),
            pl.BlockSpec((tm, d), lambda t: (t, 0)),
            pl.BlockSpec((1, 1, d), mod_idx),
            pl.BlockSpec(memory_space=pl.ANY),
        ],
        out_specs=pl.BlockSpec((tm, d), lambda t: (t, 0)),
        out_shape=jax.ShapeDtypeStruct((n, d), F32),
        scratch_shapes=[pltpu.VMEM((2, tm, d), F32), pltpu.SemaphoreType.DMA((2,))],
        compiler_params=_cparams(("arbitrary",)),
        name="combine",
    )(pos3, pos3, x1, mod_g2, ys)


def _block_diag(w):
    n, d, e = w.shape
    eye = jnp.eye(n, dtype=w.dtype)
    return (eye[:, None, :, None] * w[:, :, None, :]).reshape(n * d, n * e)


def _rope_tables(seq):
    rows = seq // GRID_W
    row = jnp.repeat(jnp.arange(rows, dtype=F32), GRID_W)
    col = jnp.tile(jnp.arange(GRID_W, dtype=F32), rows)
    n_freq = HEAD_DIM // 4
    inv = ROPE_BASE ** (-jnp.arange(n_freq, dtype=F32) / n_freq)
    ang = jnp.concatenate([row[:, None] * inv, col[:, None] * inv], axis=-1)
    cos, sin = jnp.cos(ang), jnp.sin(ang)
    reps = LANES // (HEAD_DIM // 2)
    sign = jnp.tile(jnp.concatenate([-jnp.ones((HEAD_DIM // 2,), F32), jnp.ones((HEAD_DIM // 2,), F32)]),
                    LANES // HEAD_DIM)
    return jnp.tile(cos, (1, reps)), jnp.tile(sin, (1, reps)) * sign


def _tile_plan(counts, n_tokens, tm):
    n_tiles = -(-(n_tokens + N_BUCKETS * (tm - 1)) // tm)
    padded = ((counts + tm - 1) // tm) * tm
    ends = jnp.cumsum(padded)
    base = ends - padded
    tile_start = jnp.arange(n_tiles, dtype=I32) * tm
    tile_bucket = jnp.sum((tile_start[:, None] >= ends[None, :]).astype(I32), axis=1)
    active = (tile_bucket < N_BUCKETS).astype(I32)
    last_used = jnp.maximum(jnp.sum(active) - 1, 0)
    tile_bucket = jnp.where(active == 1, tile_bucket, tile_bucket[last_used])
    tile_bucket = jnp.minimum(tile_bucket, N_BUCKETS - 1)
    grp, pair = tile_bucket // N_PAIRS, tile_bucket % N_PAIRS
    ea = grp * EXPERTS_PER_GROUP + jnp.asarray(PAIR_SLOT_A, I32)[pair]
    eb = grp * EXPERTS_PER_GROUP + jnp.asarray(PAIR_SLOT_B, I32)[pair]
    return base, ea, eb, active, n_tiles


def kernel(x, c, ctx, c_ctx, w_mod, b_mod, norm1_g, norm2_g, w_in, b_merge, conv_w, conv_b, lru_wa, lru_ba,
           lru_wx, lru_bx, lru_lambda, q_norm_g, k_norm_g, sink, w_branch, w_out, router_w, router_b,
           expert_w1, expert_w3, expert_w2):
    batch, seq, d = x.shape
    n_ctx_per = ctx.shape[1]
    n_layers = w_mod.shape[0]
    n_ctx = batch * n_ctx_per
    n_lat = batch * seq
    tm = min(256, n_ctx_per)
    tm_mxu = min(512, n_ctx, seq)
    tq = 128
    tk = min(512, seq)
    tm_moe = 256
    tn_route = min(1024, n_ctx_per)

    n_mod_rows = -(-(batch + 1) // 8) * 8
    cvec = jnp.zeros((n_mod_rows, d), F32).at[0].set(c_ctx).at[1:batch + 1].set(c)
    mod = _modulation(cvec, w_mod, b_mod)[:, :batch + 1]
    mod = mod.reshape(n_layers, batch + 1, 6, 1, d)

    cos_t, sin_t = _rope_tables(seq)
    seg = jnp.kron(jnp.eye(LANES // HEAD_DIM, dtype=F32),
                   jnp.full((HEAD_DIM, HEAD_DIM), 1.0 / HEAD_DIM, F32)).astype(BF16)
    eye = jnp.eye(LANES, dtype=BF16)
    tri = (jnp.arange(tn_route)[:, None] <= jnp.arange(tn_route)[None, :]).astype(BF16)
    rw_t = router_w.T
    rw_pad = jnp.zeros((d, LANES), F32).at[:, :N_EXPERTS].set(router_w)
    rw_hi = rw_pad.astype(BF16)
    rw_lo = (rw_pad - rw_hi.astype(F32)).astype(BF16)
    rw3 = rw_t.reshape(N_EXPERTS, 1, d)
    rb = router_b.reshape(N_EXPERTS, 1)

    x_all = jnp.concatenate([ctx.reshape(n_ctx, d), x.reshape(n_lat, d)], axis=0)
    for l in range(n_layers):
        last = l == n_layers - 1
        sh1, sc1, g1, sh2, sc2, g2 = (mod[l, :, k] for k in range(6))
        qgain = jnp.tile(q_norm_g[l], (1, LANES // HEAD_DIM))
        kgain = jnp.tile(k_norm_g[l], (1, LANES // HEAD_DIM))
        xr, gr, qg, kg, vg, qw, kw, vw, mg = _input_projection(
            x_all, sh1, sc1, norm1_g[l][None], w_in[l].astype(BF16), b_merge[l][None], qgain, kgain,
            cos_t, sin_t, seg, eye, n_ctx=n_ctx, seq=seq, tm=tm_mxu)
        hfb = _rglru(xr, conv_w[l], conv_b[l][None],
                     jax.vmap(_block_diag)(lru_wa[l]).astype(BF16), jax.vmap(_block_diag)(lru_wx[l]).astype(BF16),
                     lru_ba[l][:, None], lru_bx[l][:, None], lru_lambda[l][:, None],
                     batch=batch, n_ctx_per=n_ctx_per, seq=seq)
        attn_args = dict(batch=batch, n_ctx_per=n_ctx_per, seq=seq, with_ctx_queries=not last, tq=tq, tk=tk)
        yg = _attention(sink[l], qg, kg, vg, eye, band=False, use_sink=False, n_sub=1, **attn_args)
        yw = _attention(sink[l], qw, kw, vw, eye, band=True, use_sink=True, n_sub=min(2, n_ctx_per // tq),
                        **attn_args)
        x1, h2, logits_t = _merge(x_all, hfb, gr, yg, yw, mg, w_branch[l].astype(BF16), w_out[l].astype(BF16),
                                  g1, norm2_g[l][None], sh2, sc2, rw_hi, rw_lo,
                                  n_ctx=n_ctx, seq=seq, with_ctx=not last, tm=tm_mxu)
        n_tok = x1.shape[0]
        bucket, rank, counts = _routing(logits_t, rb, tri, tn=tn_route)
        base, ea, eb, active, n_tiles = _tile_plan(counts[:N_BUCKETS, 0], n_tok, tm_moe)
        base_col = jnp.zeros((BUCKET_ROWS, 1), I32).at[:N_BUCKETS, 0].set(base.astype(I32))
        pos = _positions(bucket, rank, base_col, tn=min(4096, n_ctx))
        pos3 = pos.reshape(n_tok // tm, 1, tm)
        xs = _dispatch(pos3, h2, jnp.zeros((n_tiles * tm_moe, d), F32), tm=tm)
        ys = _moe(ea, eb, active, xs, expert_w1[l].astype(BF16), expert_w3[l].astype(BF16),
                  expert_w2[l].astype(BF16), rw3, tm=tm_moe)
        x_all = _combine(pos3, x1, g2, ys, n_ctx=n_ctx, seq=seq, with_ctx=not last, tm=tm)
    return x_all.reshape(batch, seq, d)
```

```python
import functools

import jax
import jax.numpy as jnp
from jax import lax
from jax.experimental import pallas as pl
from jax.experimental.pallas import tpu as pltpu

F32 = jnp.float32
BF16 = jnp.bfloat16
I32 = jnp.int32

HEAD_DIM = 64
N_Q_HEADS = 8
N_KV_HEADS = 2
GQA_GROUP = N_Q_HEADS // N_KV_HEADS
ATTN_WIDTH = N_Q_HEADS * HEAD_DIM
KV_WIDTH = N_KV_HEADS * HEAD_DIM
RNN_WIDTH = 512
RNN_BLOCKS = 8
CONV_WIDTH = 4
LRU_C = 8.0
WINDOW = 128
GRID_W = 64
ROPE_BASE = 10000.0
N_BRANCHES = 3
N_EXPERTS = 16
N_GROUPS = 4
EXPERTS_PER_GROUP = 4
EPS = 1e-6
NEG_INF = -1e30
LOG2E = 1.4426950408889634

PAIR_SLOT_A = (0, 0, 0, 1, 1, 3)
PAIR_SLOT_B = (1, 2, 3, 3, 2, 2)
N_PAIRS = 6
N_BUCKETS = N_GROUPS * N_PAIRS
BUCKET_ROWS = 32

LANES = 128
SUBLANES = 8
VMEM_LIMIT = 56 * 1024 * 1024

HIGHEST = lax.Precision.HIGHEST


def _cparams(sem):
    return pltpu.CompilerParams(dimension_semantics=sem, vmem_limit_bytes=VMEM_LIMIT)


def _sigmoid(x):
    return 0.5 * jnp.tanh(0.5 * x) + 0.5


def _rms_mod(x, g, scale, shift):
    ms = jnp.mean(x * x, axis=-1, keepdims=True)
    return (x * lax.rsqrt(ms + EPS) * g) * (1.0 + scale) + shift


def _mod_kernel(a_ref, w_ref, b_ref, o_ref):
    a = a_ref[...]
    a = a * _sigmoid(a)
    o_ref[0] = jnp.dot(a, w_ref[0], precision=HIGHEST, preferred_element_type=F32) + b_ref[0]


def _modulation(cvec, w_mod, b_mod):
    rows, d = cvec.shape
    n_layers, _, n_out = w_mod.shape
    tn = 1536
    return pl.pallas_call(
        _mod_kernel,
        grid=(n_layers, n_out // tn),
        in_specs=[
            pl.BlockSpec((rows, d), lambda l, j: (0, 0)),
            pl.BlockSpec((1, d, tn), lambda l, j: (l, 0, j)),
            pl.BlockSpec((1, 1, tn), lambda l, j: (l, 0, j)),
        ],
        out_specs=pl.BlockSpec((1, rows, tn), lambda l, j: (l, 0, j)),
        out_shape=jax.ShapeDtypeStruct((n_layers, rows, n_out), F32),
        compiler_params=_cparams(("arbitrary", "arbitrary")),
        name="modulation",
    )(cvec, w_mod, b_mod.reshape(n_layers, 1, n_out))


def _head_norm_rope(p, gain, cos_t, sin_t, seg, lo32):
    ss = p * p
    hi = ss.astype(BF16)
    lo = (ss - hi.astype(F32)).astype(BF16)
    mean = jnp.dot(hi, seg, preferred_element_type=F32) + jnp.dot(lo, seg, preferred_element_type=F32)
    n = p * lax.rsqrt(mean + EPS) * gain
    partner = jnp.where(lo32, pltpu.roll(n, 96, 1), pltpu.roll(n, 32, 1))
    return n * cos_t + partner * sin_t


def _inproj_kernel(x_ref, *refs, n_ctx_tiles):
    _inproj_body(x_ref[...], *refs, n_ctx_tiles=n_ctx_tiles)


def _inproj_combine_kernel(pos_ref, nxt_ref, g2_ref, ys_ref, x1_ref, *refs, n_ctx_tiles):
    *refs, x_o, buf, sem = refs
    t = pl.program_id(0)
    tm = x1_ref.shape[0]
    slot = t % 2

    @pl.when(t == 0)
    def _():
        _row_gather(pos_ref, ys_ref, buf.at[0], sem.at[0], tm)

    @pl.when(t + 1 < pl.num_programs(0))
    def _():
        _row_gather(nxt_ref, ys_ref, buf.at[1 - slot], sem.at[1 - slot], tm)

    _row_copy_wait(ys_ref, buf.at[slot], sem.at[slot], tm)
    x = x1_ref[...] + g2_ref[0] * buf[slot]
    x_o[...] = x
    _inproj_body(x, *refs, n_ctx_tiles=n_ctx_tiles)


def _inproj_body(x, sh_ref, sc_ref, g_ref, w_ref, bm_ref, qgain_ref, kgain_ref, cos_ref, sin_ref, seg_ref,
                 eye_ref, xr_o, gr_o, qg_o, kg_o, vg_o, qw_o, kw_o, vw_o, mg_o, *, n_ctx_tiles):
    i = pl.program_id(0)
    is_lat = i >= n_ctx_tiles
    h = _rms_mod(x, g_ref[...], sc_ref[0], sh_ref[0])
    hb = h.astype(BF16)
    tm = hb.shape[0]

    def proj(c0, width):
        return jnp.dot(hb, w_ref[:, c0:c0 + width], preferred_element_type=F32)

    lane = lax.broadcasted_iota(I32, (tm, LANES), 1)
    lo32 = (lane & (HEAD_DIM - 1)) < (HEAD_DIM // 2)
    lo64 = lane < HEAD_DIM
    cos_t = jnp.where(is_lat, cos_ref[...], 1.0)
    sin_t = jnp.where(is_lat, sin_ref[...], 0.0)
    seg = seg_ref[...]
    zero = jnp.zeros((tm, LANES), F32)
    d = x.shape[1]

    def rnn_epilogue(p):
        xr_o[...] = p[:, :RNN_WIDTH].astype(BF16)
        gr_o[...] = jax.nn.gelu(p[:, RNN_WIDTH:]).astype(BF16)

    def q_epilogue(a, q_o, p):
        qgain = qgain_ref[a:a + 1, :]
        for c in range(ATTN_WIDTH // LANES):
            y = _head_norm_rope(p[:, c * LANES:(c + 1) * LANES], qgain, cos_t, sin_t, seg, lo32)
            y = y * (HEAD_DIM ** -0.5 * LOG2E)
            yr = pltpu.roll(y, HEAD_DIM, 1)
            if c < 2:
                out_a, out_b = jnp.where(lo64, y, zero), jnp.where(lo64, yr, zero)
            else:
                out_a, out_b = jnp.where(lo64, zero, yr), jnp.where(lo64, zero, y)
            q_o[2 * c] = out_a.astype(BF16)
            q_o[2 * c + 1] = out_b.astype(BF16)

    def kv_epilogue(a, k_o, v_o, p):
        k_o[...] = _head_norm_rope(p[:, :KV_WIDTH], kgain_ref[a:a + 1, :], cos_t, sin_t, seg, lo32).astype(BF16)
        pv = p[:, KV_WIDTH:]
        for hd, vh in enumerate((jnp.where(lo64, pv, 1.0).astype(BF16), jnp.where(lo64, 1.0, pv).astype(BF16))):
            if a == 0:
                vh = lax.dot_general(eye_ref[...], vh, (((1,), (1,)), ((), ())),
                                     preferred_element_type=F32).astype(BF16)
            v_o[hd] = vh

    def gate_epilogue(b, p):
        mg_o[:, b * d:(b + 1) * d] = _sigmoid(p + bm_ref[:, b * d:(b + 1) * d]).astype(BF16)

    groups = [(0, 2 * RNN_WIDTH, rnn_epilogue)]
    base = 2 * RNN_WIDTH
    for a, (q_o, k_o, v_o) in enumerate(((qg_o, kg_o, vg_o), (qw_o, kw_o, vw_o))):
        groups.append((base, ATTN_WIDTH, functools.partial(q_epilogue, a, q_o)))
        groups.append((base + ATTN_WIDTH, 2 * KV_WIDTH, functools.partial(kv_epilogue, a, k_o, v_o)))
        base += ATTN_WIDTH + 2 * KV_WIDTH
    for b in range(N_BRANCHES):
        groups.append((base + b * d, d, functools.partial(gate_epilogue, b)))

    p_next = proj(groups[0][0], groups[0][1])
    for n, (_, _, epilogue) in enumerate(groups):
        p_cur = p_next
        if n + 1 < len(groups):
            p_next = proj(groups[n + 1][0], groups[n + 1][1])
        epilogue(p_cur)


def _input_projection(x_all, mod_sh, mod_sc, g, w_in, b_merge, qgain, kgain, cos_t, sin_t, seg, eye, *, n_ctx, seq, tm,
                      combine=None):
    ntok, d = x_all.shape
    n_in = w_in.shape[1]
    nct = n_ctx // tm
    tps = seq // tm

    def mod_idx(i):
        return (jnp.where(i >= nct, 1 + (i - nct) // tps, 0), 0, 0)

    def rope_idx(i):
        return (jnp.where(i >= nct, (i - nct) % tps, 0), 0)

    row = lambda i: (i, 0)
    fixed = lambda i: (0, 0)
    out_shape = (
        jax.ShapeDtypeStruct((ntok, RNN_WIDTH), BF16),
        jax.ShapeDtypeStruct((ntok, RNN_WIDTH), BF16),
        jax.ShapeDtypeStruct((N_Q_HEADS, ntok, LANES), BF16),
        jax.ShapeDtypeStruct((ntok, KV_WIDTH), BF16),
        jax.ShapeDtypeStruct((N_KV_HEADS, LANES, ntok), BF16),
        jax.ShapeDtypeStruct((N_Q_HEADS, ntok, LANES), BF16),
        jax.ShapeDtypeStruct((ntok, KV_WIDTH), BF16),
        jax.ShapeDtypeStruct((N_KV_HEADS, ntok, LANES), BF16),
        jax.ShapeDtypeStruct((ntok, N_BRANCHES * d), BF16),
    )
    q_spec = pl.BlockSpec((N_Q_HEADS, tm, LANES), lambda i: (0, i, 0))
    out_specs = (
        pl.BlockSpec((tm, RNN_WIDTH), row), pl.BlockSpec((tm, RNN_WIDTH), row),
        q_spec, pl.BlockSpec((tm, KV_WIDTH), row), pl.BlockSpec((N_KV_HEADS, LANES, tm), lambda i: (0, 0, i)),
        q_spec, pl.BlockSpec((tm, KV_WIDTH), row), pl.BlockSpec((N_KV_HEADS, tm, LANES), lambda i: (0, i, 0)),
        pl.BlockSpec((tm, N_BRANCHES * d), row),
    )
    in_specs = [
        pl.BlockSpec((tm, d), row),
        pl.BlockSpec((1, 1, d), mod_idx),
        pl.BlockSpec((1, 1, d), mod_idx),
        pl.BlockSpec((1, d), fixed),
        pl.BlockSpec((d, n_in), fixed, pipeline_mode=pl.Buffered(1)),
        pl.BlockSpec((1, N_BRANCHES * d), fixed),
        pl.BlockSpec((2, LANES), fixed),
        pl.BlockSpec((2, LANES), fixed),
        pl.BlockSpec((tm, LANES), rope_idx),
        pl.BlockSpec((tm, LANES), rope_idx),
        pl.BlockSpec((LANES, LANES), fixed),
        pl.BlockSpec((LANES, LANES), fixed),
    ]
    common = (mod_sh, mod_sc, g, w_in, b_merge, qgain, kgain, cos_t, sin_t, seg, eye)
    if combine is None:
        return pl.pallas_call(
            functools.partial(_inproj_kernel, n_ctx_tiles=nct),
            grid=(ntok // tm,),
            in_specs=in_specs,
            out_specs=out_specs,
            out_shape=out_shape,
            compiler_params=_cparams(("arbitrary",)),
            name="input_projection",
        )(x_all, *common)
    pos3, mod_g2, ys = combine
    n_steps = ntok // tm
    smem_blk = lambda f: pl.BlockSpec((1, 1, tm), f, memory_space=pltpu.SMEM)
    outs = pl.pallas_call(
        functools.partial(_inproj_combine_kernel, n_ctx_tiles=nct),
        grid=(n_steps,),
        in_specs=[
            smem_blk(lambda i: (i, 0, 0)),
            smem_blk(lambda i: (jnp.minimum(i + 1, n_steps - 1), 0, 0)),
            pl.BlockSpec((1, 1, d), mod_idx),
            pl.BlockSpec(memory_space=pl.ANY),
        ] + in_specs,
        out_specs=out_specs + (pl.BlockSpec((tm, d), row),),
        out_shape=out_shape + (jax.ShapeDtypeStruct((ntok, d), F32),),
        scratch_shapes=[pltpu.VMEM((2, tm, d), F32), pltpu.SemaphoreType.DMA((2,))],
        compiler_params=_cparams(("arbitrary",)),
        name="input_projection_combine",
    )(pos3, pos3, mod_g2, ys, x_all, *common)
    return outs[:-1], outs[-1]


def _rglru_kernel(x_ref, prev_ref, next_ref, cw_ref, cb_ref, wa_ref, wx_ref, ba_ref, bx_ref, lam_ref,
                  o_ref, a_scr, b_scr, h_scr, y_scr, pe_scr, he_scr, c_scr, *, n_chunks):
    d = pl.program_id(0)
    j = pl.program_id(2)
    jj = jnp.where(d == 0, j, jnp.where(j == 0, 0, n_chunks - j))
    prev_valid = jj >= 2
    next_valid = (jj >= 1) & (jj <= n_chunks - 2)

    x = x_ref[...].astype(F32)
    tc = x.shape[0]
    hp = prev_ref[...].astype(F32)
    hn = next_ref[...].astype(F32)
    halo = prev_ref.shape[0]
    pm1 = jnp.where(prev_valid, hp[halo - 1:halo, :], 0.0)
    pm2 = jnp.where(prev_valid, hp[halo - 2:halo - 1, :], 0.0)
    nx0 = jnp.where(next_valid, hn[0:1, :], 0.0)
    row = lax.broadcasted_iota(I32, x.shape, 0)
    xm1 = jnp.where(row == 0, pm1, pltpu.roll(x, 1, 0))
    xm2 = jnp.where(row == 0, pm2, jnp.where(row == 1, pm1, pltpu.roll(x, 2, 0)))
    xp1 = jnp.where(row == tc - 1, nx0, pltpu.roll(x, tc - 1, 0))
    xc = (xm2 * cw_ref[0:1, :] + xm1 * cw_ref[1:2, :] + x * cw_ref[2:3, :] + xp1 * cw_ref[3:4, :]) + cb_ref[...]

    xb = xc.astype(BF16)
    r = _sigmoid(jnp.dot(xb, wa_ref[0], preferred_element_type=F32) + ba_ref[0])
    g = _sigmoid(jnp.dot(xb, wx_ref[0], preferred_element_type=F32) + bx_ref[0])
    neg_lam = -lam_ref[0]
    softplus = jnp.maximum(neg_lam, 0.0) + jnp.log(1.0 + jnp.exp(-jnp.abs(neg_lam)))
    a = jnp.exp(-LRU_C * r * softplus)
    bb = jnp.sqrt(1.0 - a * a) * (g * xc)
    n_lt = a.shape[1] // LANES
    for lt in range(n_lt):
        a_scr[lt * tc:(lt + 1) * tc, :] = a[:, lt * LANES:(lt + 1) * LANES]
        b_scr[lt * tc:(lt + 1) * tc, :] = bb[:, lt * LANES:(lt + 1) * LANES]

    @pl.when(j == 0)
    def _():
        h_scr[...] = jnp.zeros_like(h_scr)

    ng = tc // SUBLANES

    def scan(reverse):
        order = range(SUBLANES - 1, -1, -1) if reverse else range(SUBLANES)
        prod, loc = {}, {}
        for lt in range(n_lt):
            p = hl = None
            for k in order:
                ak = a_scr[pl.ds(lt * tc + k, ng, stride=SUBLANES), :]
                bk = b_scr[pl.ds(lt * tc + k, ng, stride=SUBLANES), :]
                p, hl = (ak, bk) if p is None else (ak * p, ak * hl + bk)
                prod[lt, k], loc[lt, k] = p, hl
            pe_scr[lt * ng:(lt + 1) * ng, :] = p
            he_scr[lt * ng:(lt + 1) * ng, :] = hl
        c = [h_scr[lt:lt + 1, :] for lt in range(n_lt)]
        for g in (range(ng - 1, -1, -1) if reverse else range(ng)):
            for lt in range(n_lt):
                r = lt * ng + g
                c_scr[r:r + 1, :] = c[lt]
                c[lt] = pe_scr[r:r + 1, :] * c[lt] + he_scr[r:r + 1, :]
        for lt in range(n_lt):
            h_scr[lt:lt + 1, :] = c[lt]
            carry_in = c_scr[lt * ng:(lt + 1) * ng, :]
            for k in order:
                y_scr[pl.ds(lt * tc + k, ng, stride=SUBLANES), :] = loc[lt, k] + prod[lt, k] * carry_in

    pl.when(d == 0)(lambda: scan(False))
    pl.when(d != 0)(lambda: scan(True))
    for lt in range(n_lt):
        o_ref[0, :, lt * LANES:(lt + 1) * LANES] = y_scr[lt * tc:(lt + 1) * tc, :].astype(BF16)


def _rglru(xr, conv_w, conv_b, wa, wx, ba, bx, lam, *, batch, n_ctx_per, seq):
    ntok, width = xr.shape
    tc = n_ctx_per
    halo = 16
    n_chunks = 1 + seq // tc
    n_lt = width // LANES
    lat0 = batch

    def seq_chunk(d, j):
        return jnp.where(d == 0, j, jnp.where(j == 0, 0, n_chunks - j))

    def chunk_blk(d, b, j):
        jj = seq_chunk(d, j)
        return jnp.where(jj == 0, b, lat0 + b * (seq // tc) + jj - 1)

    def x_idx(d, b, j):
        return (chunk_blk(d, b, j), 0)

    def prev_idx(d, b, j):
        return (jnp.maximum(chunk_blk(d, b, j) * (tc // halo) - 1, 0), 0)

    def next_idx(d, b, j):
        return (jnp.minimum((chunk_blk(d, b, j) + 1) * (tc // halo), ntok // halo - 1), 0)

    per_dir = lambda d, b, j: (d, 0, 0)
    fixed = lambda d, b, j: (0, 0)
    return pl.pallas_call(
        functools.partial(_rglru_kernel, n_chunks=n_chunks),
        grid=(2, batch, n_chunks),
        in_specs=[
            pl.BlockSpec((tc, width), x_idx),
            pl.BlockSpec((halo, width), prev_idx),
            pl.BlockSpec((halo, width), next_idx),
            pl.BlockSpec((CONV_WIDTH, width), fixed),
            pl.BlockSpec((1, width), fixed),
            pl.BlockSpec((1, width, width), per_dir),
            pl.BlockSpec((1, width, width), per_dir),
            pl.BlockSpec((1, 1, width), per_dir),
            pl.BlockSpec((1, 1, width), per_dir),
            pl.BlockSpec((1, 1, width), per_dir),
        ],
        out_specs=pl.BlockSpec((1, tc, width), lambda d, b, j: (d, chunk_blk(d, b, j), 0)),
        out_shape=jax.ShapeDtypeStruct((2, ntok, width), BF16),
        scratch_shapes=[
            pltpu.VMEM((n_lt * tc, LANES), F32), pltpu.VMEM((n_lt * tc, LANES), F32),
            pltpu.VMEM((n_lt, LANES), F32), pltpu.VMEM((n_lt * tc, LANES), F32),
            pltpu.VMEM((n_lt * tc // SUBLANES, LANES), F32), pltpu.VMEM((n_lt * tc // SUBLANES, LANES), F32),
            pltpu.VMEM((n_lt * tc // SUBLANES, LANES), F32),
        ],
        compiler_params=_cparams(("arbitrary", "arbitrary", "arbitrary")),
        name="rglru",
    )(xr, xr, xr, conv_w, conv_b, wa, wx, ba, bx, lam)


LOOKAHEAD = 2


def _col_max(s):
    parts = [s[r:r + 8] for r in range(0, s.shape[0], 8)]
    while len(parts) > 1:
        nxt = [jnp.maximum(parts[k], parts[k + 1]) for k in range(0, len(parts) - 1, 2)]
        if len(parts) % 2:
            nxt.append(parts[-1])
        parts = nxt
    return jnp.max(parts[0], axis=0, keepdims=True)


def _attn_kernel(sink_ref, q_ref, kc_ref, vc_ref, kl_ref, vl_ref, eye_ref, o_ref, *,
                 band, use_sink, tile_off, n_ctx_tiles, tiles_per_seq, tk, tq, pv_delay):
    i = pl.program_id(0) + tile_off
    is_lat = i >= n_ctx_tiles
    n_sub = q_ref.shape[1] // tq
    qi0 = jnp.where(is_lat, (i - n_ctx_tiles) % tiles_per_seq, 0) * n_sub
    nq = tiles_per_seq * n_sub
    nq_rows = GQA_GROUP * tq
    seq = kl_ref.shape[0]
    contract_last = (((1,), (1,)), ((), ()))
    lo64 = lax.broadcasted_iota(I32, (tq, LANES), 1) < HEAD_DIM
    chan = lax.broadcasted_iota(I32, (LANES, nq_rows), 0)

    def transposed(v):
        return lax.dot_general(eye_ref[...], v, contract_last, preferred_element_type=F32).astype(BF16)

    def scores(q, k, mask):
        s = lax.dot_general(k, q, contract_last, preferred_element_type=F32)
        return s if mask is None else jnp.where(mask, s, NEG_INF)

    def absorb(carry, s, vt):
        m, acc = carry
        m_new = jnp.maximum(m, _col_max(s))
        p = jnp.exp2(s - m_new)
        acc = jnp.exp2(m - m_new) * acc + jnp.dot(vt, p.astype(BF16), preferred_element_type=F32)
        return m_new, acc

    def run(latent):
        qs, carries, blocks = [], [], []
        for u in range(n_sub):
            if latent and band:
                qi = qi0 + u
                starts = (jnp.maximum(qi - 1, 0) * tq, qi * tq, jnp.minimum(qi + 1, nq - 1) * tq)
                starts = [pl.multiple_of(s0, tq) for s0 in starts]
                kpos = lax.broadcasted_iota(I32, (3 * tq, nq_rows), 0)
                qpos = lax.broadcasted_iota(I32, (3 * tq, nq_rows), 1) & (tq - 1)
                rel = kpos - tq - qpos
                kmin = jnp.where(qi > 0, 0, tq)
                kmax = jnp.where(qi < nq - 1, 3 * tq, 2 * tq)
                band_ok = (rel >= -WINDOW) & (rel <= WINDOW) & (kpos >= kmin) & (kpos < kmax)
            for h in range(N_KV_HEADS):
                qs.append(q_ref[GQA_GROUP * h:GQA_GROUP * (h + 1), u * tq:(u + 1) * tq, :].reshape(nq_rows, LANES))
                ones_rows = (chan >= HEAD_DIM) if h == 0 else (chan < HEAD_DIM)
                if use_sink:
                    m0 = jnp.concatenate(
                        [jnp.full((1, tq), sink_ref[GQA_GROUP * h + g] * LOG2E, F32) for g in range(GQA_GROUP)],
                        axis=1)
                    acc0 = jnp.where(ones_rows, 1.0, 0.0)
                else:
                    m0 = jnp.full((1, nq_rows), NEG_INF, F32)
                    acc0 = jnp.zeros((LANES, nq_rows), F32)
                carries.append((m0, acc0))
                blk = [(lambda: kc_ref[...],
                        (lambda h=h: transposed(vc_ref[h])) if band else (lambda h=h: vc_ref[h]), None)]
                if latent and band:
                    blk.append((
                        lambda starts=starts: jnp.concatenate([kl_ref[pl.ds(s0, tq), :] for s0 in starts], axis=0),
                        lambda starts=starts, h=h: transposed(
                            jnp.concatenate([vl_ref[h, pl.ds(s0, tq), :] for s0 in starts], axis=0)),
                        band_ok))
                elif latent:
                    for c in range(seq // tk):
                        blk.append((lambda c=c: kl_ref[c * tk:(c + 1) * tk, :],
                                    lambda c=c, h=h: vl_ref[h, :, c * tk:(c + 1) * tk], None))
                blocks.append(blk)

        n_streams = len(qs)
        items = [(st, c) for c in range(len(blocks[0])) for st in range(n_streams)]

        def issue(item):
            st, c = item
            return scores(qs[st], blocks[st][c][0](), blocks[st][c][2])

        depth = LOOKAHEAD
        ahead = [issue(it) for it in items[:depth]]
        pending = []

        def value_matmul():
            st, alpha, p, vt = pending.pop(0)
            m, acc = carries[st]
            carries[st] = (m, alpha * acc + jnp.dot(vt, p, preferred_element_type=F32))

        for idx, (st, c) in enumerate(items):
            s_cur = ahead.pop(0)
            if idx + depth < len(items):
                ahead.append(issue(items[idx + depth]))
            if pv_delay and len(pending) >= pv_delay:
                value_matmul()
            m, acc = carries[st]
            m_new = jnp.maximum(m, _col_max(s_cur))
            pending.append((st, jnp.exp2(m - m_new), jnp.exp2(s_cur - m_new).astype(BF16), blocks[st][c][1]()))
            carries[st] = (m_new, acc)
            if not pv_delay:
                value_matmul()
        while pending:
            value_matmul()

        for st in range(n_streams):
            u, h = divmod(st, N_KV_HEADS)
            _, acc = carries[st]
            denom = acc[HEAD_DIM:HEAD_DIM + 1, :] if h == 0 else acc[0:1, :]
            o = jnp.transpose(acc * (1.0 / denom))
            og = [o[g * tq:(g + 1) * tq] for g in range(GQA_GROUP)]
            for c in range(GQA_GROUP // 2):
                e, f = og[2 * c], og[2 * c + 1]
                if h == 0:
                    chunk = jnp.where(lo64, e, pltpu.roll(f, HEAD_DIM, 1))
                else:
                    chunk = jnp.where(lo64, pltpu.roll(e, HEAD_DIM, 1), f)
                col = (h * (GQA_GROUP // 2) + c) * LANES
                o_ref[u * tq:(u + 1) * tq, col:col + LANES] = chunk.astype(BF16)

    if tile_off == 0:
        pl.when(is_lat)(lambda: run(True))
        pl.when(jnp.logical_not(is_lat))(lambda: run(False))
    else:
        run(True)


def _attention(sink, q, k, v, eye, *, band, use_sink, batch, n_ctx_per, seq, with_ctx_queries, tq, tk, n_sub):
    ntok = k.shape[0]
    n_ctx = batch * n_ctx_per
    assert n_ctx % seq == 0, "context rows must cover whole latent-sequence blocks"
    assert tq == WINDOW
    rows = n_sub * tq
    assert n_ctx_per % rows == 0 and seq % rows == 0
    nct = n_ctx // rows
    tps = seq // rows
    tile_off = 0 if with_ctx_queries else nct
    n_tiles = ntok // rows - tile_off

    def batch_of(t):
        i = t + tile_off
        return jnp.where(i >= nct, (i - nct) // tps, (i * rows) // n_ctx_per)

    ctx_idx = lambda t, *_: (batch_of(t), 0)
    lat_idx = lambda t, *_: (n_ctx // seq + batch_of(t), 0)
    if band:
        vc_spec = pl.BlockSpec((N_KV_HEADS, n_ctx_per, LANES), lambda t, *_: (0, batch_of(t), 0))
        vl_spec = pl.BlockSpec((N_KV_HEADS, seq, LANES), lambda t, *_: (0, n_ctx // seq + batch_of(t), 0))
    else:
        vc_spec = pl.BlockSpec((N_KV_HEADS, LANES, n_ctx_per), lambda t, *_: (0, 0, batch_of(t)))
        vl_spec = pl.BlockSpec((N_KV_HEADS, LANES, seq), lambda t, *_: (0, 0, n_ctx // seq + batch_of(t)))
    grid_spec = pltpu.PrefetchScalarGridSpec(
        num_scalar_prefetch=1,
        grid=(n_tiles,),
        in_specs=[
            pl.BlockSpec((N_Q_HEADS, rows, LANES), lambda t, *_: (0, t + tile_off, 0)),
            pl.BlockSpec((n_ctx_per, KV_WIDTH), ctx_idx),
            vc_spec,
            pl.BlockSpec((seq, KV_WIDTH), lat_idx),
            vl_spec,
            pl.BlockSpec((LANES, LANES), lambda t, *_: (0, 0)),
        ],
        out_specs=pl.BlockSpec((rows, ATTN_WIDTH), lambda t, *_: (t, 0)),
    )
    return pl.pallas_call(
        functools.partial(_attn_kernel, band=band, use_sink=use_sink, tile_off=tile_off,
                          n_ctx_tiles=nct, tiles_per_seq=tps, tk=tk, tq=tq, pv_delay=2 if band else 0),
        grid_spec=grid_spec,
        out_shape=jax.ShapeDtypeStruct((n_tiles * rows, ATTN_WIDTH), BF16),
        compiler_params=_cparams(("arbitrary",)),
        name="window_attention" if band else "global_attention",
    )(sink, q, k, v, k, v, eye)


def _merge_kernel(x_ref, h_ref, gr_ref, yg_ref, yw_ref, mg_ref, wb_ref, wo_ref, g1_ref, n2_ref, sh2_ref, sc2_ref,
                  rwh_ref, rwl_ref, x1_o, h2_o, lg_o):
    tm, d = x_ref.shape
    n_parts = 2 if tm % 256 == 0 else 1
    part = tm // n_parts
    rows = [slice(k * part, (k + 1) * part) for k in range(n_parts)]

    def branches(r):
        y_rnn = ((h_ref[0, r, :].astype(F32) + h_ref[1, r, :].astype(F32)) * gr_ref[r, :].astype(F32)).astype(BF16)
        ys = (y_rnn, yg_ref[r, :], yw_ref[r, :])
        merged = None
        for b in range(N_BRANCHES):
            t = mg_ref[r, b * d:(b + 1) * d].astype(F32) * jnp.dot(ys[b], wb_ref[b], preferred_element_type=F32)
            merged = t if merged is None else merged + t
        return merged.astype(BF16)

    merged = [branches(r) for r in rows]
    outs = [jnp.dot(m, wo_ref[...], preferred_element_type=F32) for m in merged]
    h2s = []
    for r, out in zip(rows, outs):
        x1 = x_ref[r, :] + g1_ref[0] * out
        x1_o[r, :] = x1
        h2 = _rms_mod(x1, n2_ref[...], sc2_ref[0], sh2_ref[0])
        h2_o[r, :] = h2
        h2s.append(h2)
    for r, h2 in zip(rows, h2s):
        hi = h2.astype(BF16)
        lo = (h2 - hi.astype(F32)).astype(BF16)
        lg = jnp.dot(hi, rwh_ref[...], preferred_element_type=F32) + (
            jnp.dot(lo, rwh_ref[...], preferred_element_type=F32) + jnp.dot(hi, rwl_ref[...], preferred_element_type=F32))
        lg_o[:, r] = jnp.transpose(lg)[:N_EXPERTS, :]


def _merge(x_all, hfb, gr, yg, yw, mg, wb, wo, mod_g1, n2, mod_sh2, mod_sc2, rw_hi, rw_lo, *, n_ctx, seq, with_ctx, tm):
    ntok, d = x_all.shape
    nct = n_ctx // tm
    tps = seq // tm
    off = 0 if with_ctx else nct
    n_tiles = ntok // tm - off
    n_out = n_tiles * tm

    def mod_idx(t):
        i = t + off
        return (jnp.where(i >= nct, 1 + (i - nct) // tps, 0), 0, 0)

    row_in = lambda t: (t + off, 0)
    row_out = lambda t: (t, 0)
    fixed2 = lambda t: (0, 0)
    return pl.pallas_call(
        _merge_kernel,
        grid=(n_tiles,),
        in_specs=[
            pl.BlockSpec((tm, d), row_in),
            pl.BlockSpec((2, tm, RNN_WIDTH), lambda t: (0, t + off, 0)),
            pl.BlockSpec((tm, RNN_WIDTH), row_in),
            pl.BlockSpec((tm, ATTN_WIDTH), row_out),
            pl.BlockSpec((tm, ATTN_WIDTH), row_out),
            pl.BlockSpec((tm, N_BRANCHES * d), row_in),
            pl.BlockSpec((N_BRANCHES, RNN_WIDTH, d), lambda t: (0, 0, 0)),
            pl.BlockSpec((d, d), fixed2),
            pl.BlockSpec((1, 1, d), mod_idx),
            pl.BlockSpec((1, d), fixed2),
            pl.BlockSpec((1, 1, d), mod_idx),
            pl.BlockSpec((1, 1, d), mod_idx),
            pl.BlockSpec((d, LANES), fixed2),
            pl.BlockSpec((d, LANES), fixed2),
        ],
        out_specs=(
            pl.BlockSpec((tm, d), row_out),
            pl.BlockSpec((tm, d), row_out),
            pl.BlockSpec((N_EXPERTS, tm), lambda t: (0, t)),
        ),
        out_shape=(
            jax.ShapeDtypeStruct((n_out, d), F32),
            jax.ShapeDtypeStruct((n_out, d), F32),
            jax.ShapeDtypeStruct((N_EXPERTS, n_out), F32),
        ),
        compiler_params=_cparams(("arbitrary",)),
        name="merge",
    )(x_all, hfb, gr, yg, yw, mg, wb, wo, mod_g1, n2, mod_sh2, mod_sc2, rw_hi, rw_lo)


def _first_argmax(vals):
    best, idx = vals[0], jnp.zeros(vals[0].shape, I32)
    for k in range(1, len(vals)):
        take = vals[k] > best
        best = jnp.where(take, vals[k], best)
        idx = jnp.where(take, k, idx)
    return best, idx


def _routing_kernel(lg_ref, rb_ref, tri_ref, bucket_o, rank_o, count_o, carry_scr):
    t = pl.program_id(0)

    @pl.when(t == 0)
    def _():
        carry_scr[...] = jnp.zeros_like(carry_scr)

    s = _sigmoid(lg_ref[...])
    sel = s + rb_ref[...]
    rows = [sel[e:e + 1, :] for e in range(N_EXPERTS)]
    tn = s.shape[1]

    grp_scores = []
    for g in range(N_GROUPS):
        r = rows[g * EXPERTS_PER_GROUP:(g + 1) * EXPERTS_PER_GROUP]
        best = None
        for a in range(EXPERTS_PER_GROUP):
            for b in range(a + 1, EXPERTS_PER_GROUP):
                pair = r[a] + r[b]
                best = pair if best is None else jnp.maximum(best, pair)
        grp_scores.append(best)
    _, grp = _first_argmax(grp_scores)

    vals = []
    for k in range(EXPERTS_PER_GROUP):
        v = rows[k]
        for g in range(1, N_GROUPS):
            v = jnp.where(grp == g, rows[g * EXPERTS_PER_GROUP + k], v)
        vals.append(v)
    _, i1 = _first_argmax(vals)
    _, i2 = _first_argmax([jnp.where(i1 == k, -jnp.inf, vals[k]) for k in range(EXPERTS_PER_GROUP)])
    lo = jnp.minimum(i1, i2)
    hi = jnp.maximum(i1, i2)
    pair = jnp.where(lo == 0, hi - 1, jnp.where(lo == 1, 6 - hi, 5))
    bucket = grp * N_PAIRS + pair
    bucket_o[...] = bucket

    onehot = (lax.broadcasted_iota(I32, (BUCKET_ROWS, tn), 0) == bucket).astype(F32)
    incl = jnp.dot(onehot.astype(BF16), tri_ref[...], preferred_element_type=F32)
    before = carry_scr[...] + incl - 1.0
    rank_o[...] = jnp.sum(onehot * before, axis=0, keepdims=True).astype(I32)
    carry_scr[...] = carry_scr[...] + jnp.sum(onehot, axis=1, keepdims=True)
    count_o[...] = jnp.broadcast_to(carry_scr[...], count_o.shape).astype(I32)


def _routing(logits_t, router_b, tri, *, tn):
    n = logits_t.shape[1]
    return pl.pallas_call(
        _routing_kernel,
        grid=(n // tn,),
        in_specs=[
            pl.BlockSpec((N_EXPERTS, tn), lambda t: (0, t)),
            pl.BlockSpec((N_EXPERTS, 1), lambda t: (0, 0)),
            pl.BlockSpec((tn, tn), lambda t: (0, 0)),
        ],
        out_specs=(
            pl.BlockSpec((1, tn), lambda t: (0, t)),
            pl.BlockSpec((1, tn), lambda t: (0, t)),
            pl.BlockSpec((BUCKET_ROWS, LANES), lambda t: (0, 0)),
        ),
        out_shape=(
            jax.ShapeDtypeStruct((1, n), I32),
            jax.ShapeDtypeStruct((1, n), I32),
            jax.ShapeDtypeStruct((BUCKET_ROWS, LANES), I32),
        ),
        scratch_shapes=[pltpu.VMEM((BUCKET_ROWS, 1), F32)],
        compiler_params=_cparams(("arbitrary",)),
        name="routing",
    )(logits_t, router_b, tri)


def _position_kernel(bucket_ref, rank_ref, base_ref, pos_o):
    bucket = bucket_ref[...]
    ids = lax.broadcasted_iota(I32, (BUCKET_ROWS, bucket.shape[1]), 0)
    base = jnp.sum(jnp.where(ids == bucket, base_ref[...], 0), axis=0, keepdims=True)
    pos_o[...] = base + rank_ref[...]


def _positions(bucket, rank, base, *, tn):
    n = bucket.shape[1]
    blk = pl.BlockSpec((1, tn), lambda t: (0, t))
    return pl.pallas_call(
        _position_kernel,
        grid=(n // tn,),
        in_specs=[blk, blk, pl.BlockSpec((BUCKET_ROWS, 1), lambda t: (0, 0))],
        out_specs=blk,
        out_shape=jax.ShapeDtypeStruct((1, n), I32),
        compiler_params=_cparams(("arbitrary",)),
        name="positions",
    )(bucket, rank, base)


ROWS_PER_ISSUE = 8


def _row_gather(idx_ref, src, dst, sem, n):
    def issue(g, _):
        for k in range(ROWS_PER_ISSUE):
            r = g * ROWS_PER_ISSUE + k
            p = idx_ref[0, 0, r]
            pltpu.make_async_copy(src.at[pl.ds(p, 1), :], dst.at[pl.ds(r, 1), :], sem).start(priority=k % 2)
        return 0

    lax.fori_loop(0, n // ROWS_PER_ISSUE, issue, 0)


def _row_copy_wait(src, dst, sem, n):
    def wait(r, _):
        pltpu.make_async_copy(src.at[pl.ds(0, 1), :], dst.at[pl.ds(0, 1), :], sem).wait()
        return 0

    lax.fori_loop(0, n, wait, 0, unroll=8)


def _dispatch_kernel(pos_ref, h_ref, init_ref, xs_ref, sem):
    del init_ref
    tm = h_ref.shape[0]

    def issue(g, _):
        for k in range(ROWS_PER_ISSUE):
            r = g * ROWS_PER_ISSUE + k
            p = pos_ref[0, 0, r]
            pltpu.make_async_copy(h_ref.at[pl.ds(r, 1), :], xs_ref.at[pl.ds(p, 1), :], sem).start(priority=k % 2)
        return 0

    lax.fori_loop(0, tm // ROWS_PER_ISSUE, issue, 0)
    _row_copy_wait(h_ref, xs_ref, sem, tm)


def _dispatch(pos3, h2, xs_init, *, tm):
    n, d = h2.shape
    return pl.pallas_call(
        _dispatch_kernel,
        grid=(n // tm,),
        in_specs=[
            pl.BlockSpec((1, 1, tm), lambda t: (t, 0, 0), memory_space=pltpu.SMEM),
            pl.BlockSpec((tm, d), lambda t: (t, 0)),
            pl.BlockSpec(memory_space=pl.ANY),
        ],
        out_specs=pl.BlockSpec(memory_space=pl.ANY),
        out_shape=jax.ShapeDtypeStruct(xs_init.shape, F32),
        scratch_shapes=[pltpu.SemaphoreType.DMA(())],
        input_output_aliases={2: 0},
        compiler_params=_cparams(("arbitrary",)),
        name="dispatch",
    )(pos3, h2, xs_init)


def _moe_kernel(ea_ref, eb_ref, act_ref, xs_ref, w1a, w3a, w2a, w1b, w3b, w2b, rwa, rwb, ys_ref):
    j = pl.program_id(0)

    @pl.when(act_ref[j] == 0)
    def _():
        ys_ref[...] = jnp.zeros_like(ys_ref)

    @pl.when(act_ref[j] != 0)
    def _():
        x = xs_ref[...]
        xb = x.astype(BF16)

        def gated(u, g):
            return ((u * _sigmoid(u)) * g).astype(BF16)

        ua = jnp.dot(xb, w1a[0], preferred_element_type=F32)
        ga = jnp.dot(xb, w3a[0], preferred_element_type=F32)
        ub = jnp.dot(xb, w1b[0], preferred_element_type=F32)
        gb = jnp.dot(xb, w3b[0], preferred_element_type=F32)
        ya = jnp.dot(gated(ua, ga), w2a[0], preferred_element_type=F32)
        yb = jnp.dot(gated(ub, gb), w2b[0], preferred_element_type=F32)
        sa = _sigmoid(jnp.sum(x * rwa[0], axis=-1, keepdims=True))
        sb = _sigmoid(jnp.sum(x * rwb[0], axis=-1, keepdims=True))
        inv = 1.0 / (sa + sb)
        ys_ref[...] = (sa * inv) * ya + (sb * inv) * yb


def _moe(ea, eb, act, xs, w1, w3, w2, rw3, *, tm):
    npad, d = xs.shape
    de = w1.shape[2]
    row = lambda j, *_: (j, 0)
    wa = lambda j, ea, eb, act: (ea[j], 0, 0)
    wb = lambda j, ea, eb, act: (eb[j], 0, 0)
    grid_spec = pltpu.PrefetchScalarGridSpec(
        num_scalar_prefetch=3,
        grid=(npad // tm,),
        in_specs=[
            pl.BlockSpec((tm, d), row),
            pl.BlockSpec((1, d, de), wa), pl.BlockSpec((1, d, de), wa), pl.BlockSpec((1, de, d), wa),
            pl.BlockSpec((1, d, de), wb), pl.BlockSpec((1, d, de), wb), pl.BlockSpec((1, de, d), wb),
            pl.BlockSpec((1, 1, d), wa), pl.BlockSpec((1, 1, d), wb),
        ],
        out_specs=pl.BlockSpec((tm, d), row),
    )
    return pl.pallas_call(
        _moe_kernel,
        grid_spec=grid_spec,
        out_shape=jax.ShapeDtypeStruct((npad, d), F32),
        compiler_params=_cparams(("arbitrary",)),
        name="expert_ffn",
    )(ea, eb, act, xs, w1, w3, w2, w1, w3, w2, rw3, rw3)


def _combine_kernel(pos_ref, nxt_ref, x_ref, g2_ref, ys_ref, o_ref, buf, sem):
    t = pl.program_id(0)
    tm = x_ref.shape[0]
    slot = t % 2

    @pl.when(t == 0)
    def _():
        _row_gather(pos_ref, ys_ref, buf.at[0], sem.at[0], tm)

    @pl.when(t + 1 < pl.num_programs(0))
    def _():
        _row_gather(nxt_ref, ys_ref, buf.at[1 - slot], sem.at[1 - slot], tm)

    _row_copy_wait(ys_ref, buf.at[slot], sem.at[slot], tm)
    o_ref[...] = x_ref[...] + g2_ref[0] * buf[slot]


def _combine(pos3, x1, mod_g2, ys, *, n_ctx, seq, with_ctx, tm):
    n, d = x1.shape
    nct = n_ctx // tm if with_ctx else 0
    tps = seq // tm

    def mod_idx(t):
        return (jnp.where(t >= nct, 1 + (t - nct) // tps, 0), 0, 0)

    return pl.pallas_call(
        _combine_kernel,
        grid=(n // tm,),
        in_specs=[
            pl.BlockSpec((1, 1, tm), lambda t: (t, 0, 0), memory_space=pltpu.SMEM),
            pl.BlockSpec((1, 1, tm), lambda t: (jnp.minimum(t + 1, n // tm - 1), 0, 0), memory_space=pltpu.SMEM),
            pl.BlockSpec((tm, d), lambda t: (t, 0)),
            pl.BlockSpec((1, 1, d), mod_idx),
            pl.BlockSpec(memory_space=pl.ANY),
        ],
        out_specs=pl.BlockSpec((tm, d), lambda t: (t, 0)),
        out_shape=jax.ShapeDtypeStruct((n, d), F32),
        scratch_shapes=[pltpu.VMEM((2, tm, d), F32), pltpu.SemaphoreType.DMA((2,))],
        compiler_params=_cparams(("arbitrary",)),
        name="combine",
    )(pos3, pos3, x1, mod_g2, ys)


def _block_diag(w):
    n, d, e = w.shape
    eye = jnp.eye(n, dtype=w.dtype)
    return (eye[:, None, :, None] * w[:, :, None, :]).reshape(n * d, n * e)


def _rope_tables(seq):
    rows = seq // GRID_W
    row = jnp.repeat(jnp.arange(rows, dtype=F32), GRID_W)
    col = jnp.tile(jnp.arange(GRID_W, dtype=F32), rows)
    n_freq = HEAD_DIM // 4
    inv = ROPE_BASE ** (-jnp.arange(n_freq, dtype=F32) / n_freq)
    ang = jnp.concatenate([row[:, None] * inv, col[:, None] * inv], axis=-1)
    cos, sin = jnp.cos(ang), jnp.sin(ang)
    reps = LANES // (HEAD_DIM // 2)
    sign = jnp.tile(jnp.concatenate([-jnp.ones((HEAD_DIM // 2,), F32), jnp.ones((HEAD_DIM // 2,), F32)]),
                    LANES // HEAD_DIM)
    return jnp.tile(cos, (1, reps)), jnp.tile(sin, (1, reps)) * sign


def _max_tiles(n_tokens, tm):
    return -(-(n_tokens + N_BUCKETS * (tm - 1)) // tm)


def _tile_plan(counts, n_tiles, tm):
    padded = ((counts + tm - 1) // tm) * tm
    ends = jnp.cumsum(padded)
    base = ends - padded
    tile_start = jnp.arange(n_tiles, dtype=I32) * tm
    tile_bucket = jnp.sum((tile_start[:, None] >= ends[None, :]).astype(I32), axis=1)
    active = (tile_bucket < N_BUCKETS).astype(I32)
    last_used = jnp.maximum(jnp.sum(active) - 1, 0)
    tile_bucket = jnp.where(active == 1, tile_bucket, tile_bucket[last_used])
    tile_bucket = jnp.minimum(tile_bucket, N_BUCKETS - 1)
    grp, pair = tile_bucket // N_PAIRS, tile_bucket % N_PAIRS
    ea = grp * EXPERTS_PER_GROUP + jnp.asarray(PAIR_SLOT_A, I32)[pair]
    eb = grp * EXPERTS_PER_GROUP + jnp.asarray(PAIR_SLOT_B, I32)[pair]
    return base, ea, eb, active


def kernel(x, c, ctx, c_ctx, w_mod, b_mod, norm1_g, norm2_g, w_in, b_merge, conv_w, conv_b, lru_wa, lru_ba,
           lru_wx, lru_bx, lru_lambda, q_norm_g, k_norm_g, sink, w_branch, w_out, router_w, router_b,
           expert_w1, expert_w3, expert_w2):
    batch, seq, d = x.shape
    n_ctx_per = ctx.shape[1]
    n_layers = w_mod.shape[0]
    n_ctx = batch * n_ctx_per
    n_lat = batch * seq
    tm = min(256, n_ctx_per)
    tm_mxu = min(512, n_ctx, seq)
    tq = 128
    tk = min(512, seq)
    tm_moe = 256
    tn_route = min(1024, n_ctx_per)

    n_mod_rows = -(-(batch + 1) // 8) * 8
    cvec = jnp.zeros((n_mod_rows, d), F32).at[0].set(c_ctx).at[1:batch + 1].set(c)
    mod = _modulation(cvec, w_mod, b_mod)[:, :batch + 1]
    mod = mod.reshape(n_layers, batch + 1, 6, 1, d)

    cos_t, sin_t = _rope_tables(seq)
    seg = jnp.kron(jnp.eye(LANES // HEAD_DIM, dtype=F32),
                   jnp.full((HEAD_DIM, HEAD_DIM), 1.0 / HEAD_DIM, F32)).astype(BF16)
    eye = jnp.eye(LANES, dtype=BF16)
    tri = (jnp.arange(tn_route)[:, None] <= jnp.arange(tn_route)[None, :]).astype(BF16)
    rw_t = router_w.T
    rw_pad = jnp.zeros((d, LANES), F32).at[:, :N_EXPERTS].set(router_w)
    rw_hi = rw_pad.astype(BF16)
    rw_lo = (rw_pad - rw_hi.astype(F32)).astype(BF16)
    rw3 = rw_t.reshape(N_EXPERTS, 1, d)
    rb = router_b.reshape(N_EXPERTS, 1)

    x_all = jnp.concatenate([ctx.reshape(n_ctx, d), x.reshape(n_lat, d)], axis=0)
    n_moe_tiles = _max_tiles(n_ctx + n_lat, tm_moe)
    sorted_rows = jnp.zeros((n_moe_tiles * tm_moe, d), F32)
    routed = None
    for l in range(n_layers):
        last = l == n_layers - 1
        sh1, sc1, g1, sh2, sc2, g2 = (mod[l, :, k] for k in range(6))
        qgain = jnp.tile(q_norm_g[l], (1, LANES // HEAD_DIM))
        kgain = jnp.tile(k_norm_g[l], (1, LANES // HEAD_DIM))
        proj_args = (sh1, sc1, norm1_g[l][None], w_in[l].astype(BF16), b_merge[l][None], qgain, kgain,
                     cos_t, sin_t, seg, eye)
        if routed is None:
            proj = _input_projection(x_all, *proj_args, n_ctx=n_ctx, seq=seq, tm=tm_mxu)
        else:
            proj, x_all = _input_projection(x_all, *proj_args, n_ctx=n_ctx, seq=seq, tm=tm_mxu, combine=routed)
        xr, gr, qg, kg, vg, qw, kw, vw, mg = proj
        hfb = _rglru(xr, conv_w[l], conv_b[l][None],
                     jax.vmap(_block_diag)(lru_wa[l]).astype(BF16), jax.vmap(_block_diag)(lru_wx[l]).astype(BF16),
                     lru_ba[l][:, None], lru_bx[l][:, None], lru_lambda[l][:, None],
                     batch=batch, n_ctx_per=n_ctx_per, seq=seq)
        attn_args = dict(batch=batch, n_ctx_per=n_ctx_per, seq=seq, with_ctx_queries=not last, tq=tq, tk=tk)
        yg = _attention(sink[l], qg, kg, vg, eye, band=False, use_sink=False, n_sub=1, **attn_args)
        yw = _attention(sink[l], qw, kw, vw, eye, band=True, use_sink=True, n_sub=min(2, n_ctx_per // tq),
                        **attn_args)
        x1, h2, logits_t = _merge(x_all, hfb, gr, yg, yw, mg, w_branch[l].astype(BF16), w_out[l].astype(BF16),
                                  g1, norm2_g[l][None], sh2, sc2, rw_hi, rw_lo,
                                  n_ctx=n_ctx, seq=seq, with_ctx=not last, tm=tm_mxu)
        n_tok = x1.shape[0]
        bucket, rank, counts = _routing(logits_t, rb, tri, tn=tn_route)
        base, ea, eb, active = _tile_plan(counts[:N_BUCKETS, 0], n_moe_tiles, tm_moe)
        base_col = jnp.zeros((BUCKET_ROWS, 1), I32).at[:N_BUCKETS, 0].set(base.astype(I32))
        pos = _positions(bucket, rank, base_col, tn=min(4096, n_ctx))
        pos3 = pos.reshape(n_tok // tm, 1, tm)
        xs = _dispatch(pos3, h2, sorted_rows, tm=tm)
        ys = _moe(ea, eb, active, xs, expert_w1[l].astype(BF16), expert_w3[l].astype(BF16),
                  expert_w2[l].astype(BF16), rw3, tm=tm_moe)
        sorted_rows = xs
        if last:
            x_all = _combine(pos3, x1, g2, ys, n_ctx=n_ctx, seq=seq, with_ctx=False, tm=tm)
        else:
            x_all, routed = x1, (pos.reshape(n_tok // tm_mxu, 1, tm_mxu), g2, ys)
    return x_all.reshape(batch, seq, d)
```

```python
import functools

import jax
import jax.numpy as jnp
from jax import lax
from jax.experimental import pallas as pl
from jax.experimental.pallas import tpu as pltpu

F32 = jnp.float32
BF16 = jnp.bfloat16
I32 = jnp.int32

HEAD_DIM = 64
N_Q_HEADS = 8
N_KV_HEADS = 2
GQA_GROUP = N_Q_HEADS // N_KV_HEADS
ATTN_WIDTH = N_Q_HEADS * HEAD_DIM
KV_WIDTH = N_KV_HEADS * HEAD_DIM
RNN_WIDTH = 512
RNN_BLOCKS = 8
CONV_WIDTH = 4
LRU_C = 8.0
WINDOW = 128
GRID_W = 64
ROPE_BASE = 10000.0
N_BRANCHES = 3
N_EXPERTS = 16
N_GROUPS = 4
EXPERTS_PER_GROUP = 4
EPS = 1e-6
NEG_INF = -1e30
LOG2E = 1.4426950408889634

PAIR_SLOT_A = (0, 0, 0, 1, 1, 3)
PAIR_SLOT_B = (1, 2, 3, 3, 2, 2)
N_PAIRS = 6
N_BUCKETS = N_GROUPS * N_PAIRS
BUCKET_ROWS = 32

LANES = 128
SUBLANES = 8
VMEM_LIMIT = 56 * 1024 * 1024

HIGHEST = lax.Precision.HIGHEST


def _cparams(sem):
    return pltpu.CompilerParams(dimension_semantics=sem, vmem_limit_bytes=VMEM_LIMIT)


def _sigmoid(x):
    return 0.5 * jnp.tanh(0.5 * x) + 0.5


def _rms_mod(x, g, scale, shift):
    ms = jnp.mean(x * x, axis=-1, keepdims=True)
    return (x * lax.rsqrt(ms + EPS) * g) * (1.0 + scale) + shift


def _mod_kernel(a_ref, w_ref, b_ref, o_ref):
    a = a_ref[...]
    a = a * _sigmoid(a)
    o_ref[0] = jnp.dot(a, w_ref[0], precision=HIGHEST, preferred_element_type=F32) + b_ref[0]


def _modulation(cvec, w_mod, b_mod):
    rows, d = cvec.shape
    n_layers, _, n_out = w_mod.shape
    tn = 1536
    return pl.pallas_call(
        _mod_kernel,
        grid=(n_layers, n_out // tn),
        in_specs=[
            pl.BlockSpec((rows, d), lambda l, j: (0, 0)),
            pl.BlockSpec((1, d, tn), lambda l, j: (l, 0, j)),
            pl.BlockSpec((1, 1, tn), lambda l, j: (l, 0, j)),
        ],
        out_specs=pl.BlockSpec((1, rows, tn), lambda l, j: (l, 0, j)),
        out_shape=jax.ShapeDtypeStruct((n_layers, rows, n_out), F32),
        compiler_params=_cparams(("arbitrary", "arbitrary")),
        name="modulation",
    )(cvec, w_mod, b_mod.reshape(n_layers, 1, n_out))


def _head_norm_rope(p, gain, cos_t, sin_t, seg, lo32):
    ss = p * p
    hi = ss.astype(BF16)
    lo = (ss - hi.astype(F32)).astype(BF16)
    mean = jnp.dot(hi, seg, preferred_element_type=F32) + jnp.dot(lo, seg, preferred_element_type=F32)
    n = p * lax.rsqrt(mean + EPS) * gain
    partner = jnp.where(lo32, pltpu.roll(n, 96, 1), pltpu.roll(n, 32, 1))
    return n * cos_t + partner * sin_t


def _inproj_kernel(x_ref, *refs, n_ctx_tiles):
    _inproj_body(x_ref[...], *refs, n_ctx_tiles=n_ctx_tiles)


def _inproj_combine_kernel(pos_ref, nxt_ref, g2_ref, ys_ref, x1_ref, *refs, n_ctx_tiles):
    *refs, x_o, buf, sem = refs
    t = pl.program_id(0)
    tm = x1_ref.shape[0]
    slot = t % 2

    @pl.when(t == 0)
    def _():
        _row_gather(pos_ref, ys_ref, buf.at[0], sem.at[0], tm)

    _row_copy_wait(ys_ref, buf.at[slot], sem.at[slot], tm)
    x = x1_ref[...] + g2_ref[0] * buf[slot]
    x_o[...] = x

    def gather_part(n, n_parts):
        rows = tm // n_parts
        for r in range(n * rows, (n + 1) * rows):
            pltpu.make_async_copy(ys_ref.at[pl.ds(nxt_ref[0, 0, r], 1), :], buf.at[1 - slot, pl.ds(r, 1), :],
                                  sem.at[1 - slot]).start(priority=r % 2)

    _inproj_body(x, *refs, n_ctx_tiles=n_ctx_tiles, between_groups=gather_part)

    @pl.when(t == pl.num_programs(0) - 1)
    def _():
        _row_copy_wait(ys_ref, buf.at[1 - slot], sem.at[1 - slot], tm)


def _inproj_body(x, sh_ref, sc_ref, g_ref, w_ref, bm_ref, qgain_ref, kgain_ref, cos_ref, sin_ref, seg_ref,
                 eye_ref, xr_o, gr_o, qg_o, kg_o, vg_o, qw_o, kw_o, vw_o, mg_o, *, n_ctx_tiles, between_groups=None):
    i = pl.program_id(0)
    is_lat = i >= n_ctx_tiles
    h = _rms_mod(x, g_ref[...], sc_ref[0], sh_ref[0])
    hb = h.astype(BF16)
    tm = hb.shape[0]

    def proj(c0, width):
        return jnp.dot(hb, w_ref[:, c0:c0 + width], preferred_element_type=F32)

    lane = lax.broadcasted_iota(I32, (tm, LANES), 1)
    lo32 = (lane & (HEAD_DIM - 1)) < (HEAD_DIM // 2)
    lo64 = lane < HEAD_DIM
    cos_t = jnp.where(is_lat, cos_ref[...], 1.0)
    sin_t = jnp.where(is_lat, sin_ref[...], 0.0)
    seg = seg_ref[...]
    zero = jnp.zeros((tm, LANES), F32)
    d = x.shape[1]

    def rnn_epilogue(p):
        xr_o[...] = p[:, :RNN_WIDTH].astype(BF16)
        gr_o[...] = jax.nn.gelu(p[:, RNN_WIDTH:]).astype(BF16)

    def q_epilogue(a, q_o, p):
        qgain = qgain_ref[a:a + 1, :]
        for c in range(ATTN_WIDTH // LANES):
            y = _head_norm_rope(p[:, c * LANES:(c + 1) * LANES], qgain, cos_t, sin_t, seg, lo32)
            y = y * (HEAD_DIM ** -0.5 * LOG2E)
            yr = pltpu.roll(y, HEAD_DIM, 1)
            if c < 2:
                out_a, out_b = jnp.where(lo64, y, zero), jnp.where(lo64, yr, zero)
            else:
                out_a, out_b = jnp.where(lo64, zero, yr), jnp.where(lo64, zero, y)
            q_o[2 * c] = out_a.astype(BF16)
            q_o[2 * c + 1] = out_b.astype(BF16)

    def kv_epilogue(a, k_o, v_o, p):
        k_o[...] = _head_norm_rope(p[:, :KV_WIDTH], kgain_ref[a:a + 1, :], cos_t, sin_t, seg, lo32).astype(BF16)
        pv = p[:, KV_WIDTH:]
        for hd, vh in enumerate((jnp.where(lo64, pv, 1.0).astype(BF16), jnp.where(lo64, 1.0, pv).astype(BF16))):
            if a == 0:
                vh = lax.dot_general(eye_ref[...], vh, (((1,), (1,)), ((), ())),
                                     preferred_element_type=F32).astype(BF16)
            v_o[hd] = vh

    def gate_epilogue(b, p):
        mg_o[:, b * d:(b + 1) * d] = _sigmoid(p + bm_ref[:, b * d:(b + 1) * d]).astype(BF16)

    groups = [(0, 2 * RNN_WIDTH, rnn_epilogue)]
    base = 2 * RNN_WIDTH
    for a, (q_o, k_o, v_o) in enumerate(((qg_o, kg_o, vg_o), (qw_o, kw_o, vw_o))):
        groups.append((base, ATTN_WIDTH, functools.partial(q_epilogue, a, q_o)))
        groups.append((base + ATTN_WIDTH, 2 * KV_WIDTH, functools.partial(kv_epilogue, a, k_o, v_o)))
        base += ATTN_WIDTH + 2 * KV_WIDTH
    for b in range(N_BRANCHES):
        groups.append((base + b * d, d, functools.partial(gate_epilogue, b)))

    p_next = proj(groups[0][0], groups[0][1])
    for n, (_, _, epilogue) in enumerate(groups):
        p_cur = p_next
        if n + 1 < len(groups):
            p_next = proj(groups[n + 1][0], groups[n + 1][1])
        if between_groups is not None:
            between_groups(n, len(groups))
        epilogue(p_cur)


def _input_projection(x_all, mod_sh, mod_sc, g, w_in, b_merge, qgain, kgain, cos_t, sin_t, seg, eye, *, n_ctx, seq, tm,
                      combine=None):
    ntok, d = x_all.shape
    n_in = w_in.shape[1]
    nct = n_ctx // tm
    tps = seq // tm

    def mod_idx(i):
        return (jnp.where(i >= nct, 1 + (i - nct) // tps, 0), 0, 0)

    def rope_idx(i):
        return (jnp.where(i >= nct, (i - nct) % tps, 0), 0)

    row = lambda i: (i, 0)
    fixed = lambda i: (0, 0)
    out_shape = (
        jax.ShapeDtypeStruct((ntok, RNN_WIDTH), BF16),
        jax.ShapeDtypeStruct((ntok, RNN_WIDTH), BF16),
        jax.ShapeDtypeStruct((N_Q_HEADS, ntok, LANES), BF16),
        jax.ShapeDtypeStruct((ntok, KV_WIDTH), BF16),
        jax.ShapeDtypeStruct((N_KV_HEADS, LANES, ntok), BF16),
        jax.ShapeDtypeStruct((N_Q_HEADS, ntok, LANES), BF16),
        jax.ShapeDtypeStruct((ntok, KV_WIDTH), BF16),
        jax.ShapeDtypeStruct((N_KV_HEADS, ntok, LANES), BF16),
        jax.ShapeDtypeStruct((ntok, N_BRANCHES * d), BF16),
    )
    q_spec = pl.BlockSpec((N_Q_HEADS, tm, LANES), lambda i: (0, i, 0))
    out_specs = (
        pl.BlockSpec((tm, RNN_WIDTH), row), pl.BlockSpec((tm, RNN_WIDTH), row),
        q_spec, pl.BlockSpec((tm, KV_WIDTH), row), pl.BlockSpec((N_KV_HEADS, LANES, tm), lambda i: (0, 0, i)),
        q_spec, pl.BlockSpec((tm, KV_WIDTH), row), pl.BlockSpec((N_KV_HEADS, tm, LANES), lambda i: (0, i, 0)),
        pl.BlockSpec((tm, N_BRANCHES * d), row),
    )
    in_specs = [
        pl.BlockSpec((tm, d), row),
        pl.BlockSpec((1, 1, d), mod_idx),
        pl.BlockSpec((1, 1, d), mod_idx),
        pl.BlockSpec((1, d), fixed),
        pl.BlockSpec((d, n_in), fixed, pipeline_mode=pl.Buffered(1)),
        pl.BlockSpec((1, N_BRANCHES * d), fixed),
        pl.BlockSpec((2, LANES), fixed),
        pl.BlockSpec((2, LANES), fixed),
        pl.BlockSpec((tm, LANES), rope_idx),
        pl.BlockSpec((tm, LANES), rope_idx),
        pl.BlockSpec((LANES, LANES), fixed),
        pl.BlockSpec((LANES, LANES), fixed),
    ]
    common = (mod_sh, mod_sc, g, w_in, b_merge, qgain, kgain, cos_t, sin_t, seg, eye)
    if combine is None:
        return pl.pallas_call(
            functools.partial(_inproj_kernel, n_ctx_tiles=nct),
            grid=(ntok // tm,),
            in_specs=in_specs,
            out_specs=out_specs,
            out_shape=out_shape,
            compiler_params=_cparams(("arbitrary",)),
            name="input_projection",
        )(x_all, *common)
    pos3, mod_g2, ys = combine
    n_steps = ntok // tm
    smem_blk = lambda f: pl.BlockSpec((1, 1, tm), f, memory_space=pltpu.SMEM)
    outs = pl.pallas_call(
        functools.partial(_inproj_combine_kernel, n_ctx_tiles=nct),
        grid=(n_steps,),
        in_specs=[
            smem_blk(lambda i: (i, 0, 0)),
            smem_blk(lambda i: (jnp.minimum(i + 1, n_steps - 1), 0, 0)),
            pl.BlockSpec((1, 1, d), mod_idx),
            pl.BlockSpec(memory_space=pl.ANY),
        ] + in_specs,
        out_specs=out_specs + (pl.BlockSpec((tm, d), row),),
        out_shape=out_shape + (jax.ShapeDtypeStruct((ntok, d), F32),),
        scratch_shapes=[pltpu.VMEM((2, tm, d), F32), pltpu.SemaphoreType.DMA((2,))],
        compiler_params=_cparams(("arbitrary",)),
        name="input_projection_combine",
    )(pos3, pos3, mod_g2, ys, x_all, *common)
    return outs[:-1], outs[-1]


def _rglru_kernel(x_ref, prev_ref, next_ref, cw_ref, cb_ref, wa_ref, wx_ref, ba_ref, bx_ref, lam_ref,
                  o_ref, a_scr, b_scr, h_scr, y_scr, pe_scr, he_scr, c_scr, *, n_chunks):
    d = pl.program_id(0)
    j = pl.program_id(2)
    jj = jnp.where(d == 0, j, jnp.where(j == 0, 0, n_chunks - j))
    prev_valid = jj >= 2
    next_valid = (jj >= 1) & (jj <= n_chunks - 2)

    x = x_ref[...].astype(F32)
    tc = x.shape[0]
    hp = prev_ref[...].astype(F32)
    hn = next_ref[...].astype(F32)
    halo = prev_ref.shape[0]
    pm1 = jnp.where(prev_valid, hp[halo - 1:halo, :], 0.0)
    pm2 = jnp.where(prev_valid, hp[halo - 2:halo - 1, :], 0.0)
    nx0 = jnp.where(next_valid, hn[0:1, :], 0.0)
    row = lax.broadcasted_iota(I32, x.shape, 0)
    xm1 = jnp.where(row == 0, pm1, pltpu.roll(x, 1, 0))
    xm2 = jnp.where(row == 0, pm2, jnp.where(row == 1, pm1, pltpu.roll(x, 2, 0)))
    xp1 = jnp.where(row == tc - 1, nx0, pltpu.roll(x, tc - 1, 0))
    xc = (xm2 * cw_ref[0:1, :] + xm1 * cw_ref[1:2, :] + x * cw_ref[2:3, :] + xp1 * cw_ref[3:4, :]) + cb_ref[...]

    xb = xc.astype(BF16)
    r = _sigmoid(jnp.dot(xb, wa_ref[0], preferred_element_type=F32) + ba_ref[0])
    g = _sigmoid(jnp.dot(xb, wx_ref[0], preferred_element_type=F32) + bx_ref[0])
    neg_lam = -lam_ref[0]
    softplus = jnp.maximum(neg_lam, 0.0) + jnp.log(1.0 + jnp.exp(-jnp.abs(neg_lam)))
    a = jnp.exp(-LRU_C * r * softplus)
    bb = jnp.sqrt(1.0 - a * a) * (g * xc)
    n_lt = a.shape[1] // LANES
    for lt in range(n_lt):
        a_scr[lt * tc:(lt + 1) * tc, :] = a[:, lt * LANES:(lt + 1) * LANES]
        b_scr[lt * tc:(lt + 1) * tc, :] = bb[:, lt * LANES:(lt + 1) * LANES]

    @pl.when(j == 0)
    def _():
        h_scr[...] = jnp.zeros_like(h_scr)

    ng = tc // SUBLANES

    def scan(reverse):
        order = range(SUBLANES - 1, -1, -1) if reverse else range(SUBLANES)
        prod, loc = {}, {}
        for lt in range(n_lt):
            p = hl = None
            for k in order:
                ak = a_scr[pl.ds(lt * tc + k, ng, stride=SUBLANES), :]
                bk = b_scr[pl.ds(lt * tc + k, ng, stride=SUBLANES), :]
                p, hl = (ak, bk) if p is None else (ak * p, ak * hl + bk)
                prod[lt, k], loc[lt, k] = p, hl
            pe_scr[lt * ng:(lt + 1) * ng, :] = p
            he_scr[lt * ng:(lt + 1) * ng, :] = hl
        c = [h_scr[lt:lt + 1, :] for lt in range(n_lt)]
        for g in (range(ng - 1, -1, -1) if reverse else range(ng)):
            for lt in range(n_lt):
                r = lt * ng + g
                c_scr[r:r + 1, :] = c[lt]
                c[lt] = pe_scr[r:r + 1, :] * c[lt] + he_scr[r:r + 1, :]
        for lt in range(n_lt):
            h_scr[lt:lt + 1, :] = c[lt]
            carry_in = c_scr[lt * ng:(lt + 1) * ng, :]
            for k in order:
                y_scr[pl.ds(lt * tc + k, ng, stride=SUBLANES), :] = loc[lt, k] + prod[lt, k] * carry_in

    pl.when(d == 0)(lambda: scan(False))
    pl.when(d != 0)(lambda: scan(True))
    for lt in range(n_lt):
        o_ref[0, :, lt * LANES:(lt + 1) * LANES] = y_scr[lt * tc:(lt + 1) * tc, :].astype(BF16)


def _rglru(xr, conv_w, conv_b, wa, wx, ba, bx, lam, *, batch, n_ctx_per, seq):
    ntok, width = xr.shape
    tc = n_ctx_per
    halo = 16
    n_chunks = 1 + seq // tc
    n_lt = width // LANES
    lat0 = batch

    def seq_chunk(d, j):
        return jnp.where(d == 0, j, jnp.where(j == 0, 0, n_chunks - j))

    def chunk_blk(d, b, j):
        jj = seq_chunk(d, j)
        return jnp.where(jj == 0, b, lat0 + b * (seq // tc) + jj - 1)

    def x_idx(d, b, j):
        return (chunk_blk(d, b, j), 0)

    def prev_idx(d, b, j):
        return (jnp.maximum(chunk_blk(d, b, j) * (tc // halo) - 1, 0), 0)

    def next_idx(d, b, j):
        return (jnp.minimum((chunk_blk(d, b, j) + 1) * (tc // halo), ntok // halo - 1), 0)

    per_dir = lambda d, b, j: (d, 0, 0)
    fixed = lambda d, b, j: (0, 0)
    return pl.pallas_call(
        functools.partial(_rglru_kernel, n_chunks=n_chunks),
        grid=(2, batch, n_chunks),
        in_specs=[
            pl.BlockSpec((tc, width), x_idx),
            pl.BlockSpec((halo, width), prev_idx),
            pl.BlockSpec((halo, width), next_idx),
            pl.BlockSpec((CONV_WIDTH, width), fixed),
            pl.BlockSpec((1, width), fixed),
            pl.BlockSpec((1, width, width), per_dir),
            pl.BlockSpec((1, width, width), per_dir),
            pl.BlockSpec((1, 1, width), per_dir),
            pl.BlockSpec((1, 1, width), per_dir),
            pl.BlockSpec((1, 1, width), per_dir),
        ],
        out_specs=pl.BlockSpec((1, tc, width), lambda d, b, j: (d, chunk_blk(d, b, j), 0)),
        out_shape=jax.ShapeDtypeStruct((2, ntok, width), BF16),
        scratch_shapes=[
            pltpu.VMEM((n_lt * tc, LANES), F32), pltpu.VMEM((n_lt * tc, LANES), F32),
            pltpu.VMEM((n_lt, LANES), F32), pltpu.VMEM((n_lt * tc, LANES), F32),
            pltpu.VMEM((n_lt * tc // SUBLANES, LANES), F32), pltpu.VMEM((n_lt * tc // SUBLANES, LANES), F32),
            pltpu.VMEM((n_lt * tc // SUBLANES, LANES), F32),
        ],
        compiler_params=_cparams(("arbitrary", "arbitrary", "arbitrary")),
        name="rglru",
    )(xr, xr, xr, conv_w, conv_b, wa, wx, ba, bx, lam)


LOOKAHEAD = 2


def _col_max(s):
    parts = [s[r:r + 8] for r in range(0, s.shape[0], 8)]
    while len(parts) > 1:
        nxt = [jnp.maximum(parts[k], parts[k + 1]) for k in range(0, len(parts) - 1, 2)]
        if len(parts) % 2:
            nxt.append(parts[-1])
        parts = nxt
    return jnp.max(parts[0], axis=0, keepdims=True)


def _attn_kernel(sink_ref, q_ref, kc_ref, vc_ref, kl_ref, vl_ref, eye_ref, o_ref, *,
                 band, use_sink, tile_off, n_ctx_tiles, tiles_per_seq, tk, tq, pv_delay):
    i = pl.program_id(0) + tile_off
    is_lat = i >= n_ctx_tiles
    n_sub = q_ref.shape[1] // tq
    qi0 = jnp.where(is_lat, (i - n_ctx_tiles) % tiles_per_seq, 0) * n_sub
    nq = tiles_per_seq * n_sub
    nq_rows = GQA_GROUP * tq
    seq = kl_ref.shape[0]
    contract_last = (((1,), (1,)), ((), ()))
    lo64 = lax.broadcasted_iota(I32, (tq, LANES), 1) < HEAD_DIM
    chan = lax.broadcasted_iota(I32, (LANES, nq_rows), 0)

    def transposed(v):
        return lax.dot_general(eye_ref[...], v, contract_last, preferred_element_type=F32).astype(BF16)

    def scores(q, k, mask):
        s = lax.dot_general(k, q, contract_last, preferred_element_type=F32)
        return s if mask is None else jnp.where(mask, s, NEG_INF)

    def absorb(carry, s, vt):
        m, acc = carry
        m_new = jnp.maximum(m, _col_max(s))
        p = jnp.exp2(s - m_new)
        acc = jnp.exp2(m - m_new) * acc + jnp.dot(vt, p.astype(BF16), preferred_element_type=F32)
        return m_new, acc

    def run(latent):
        qs, carries, blocks = [], [], []
        for u in range(n_sub):
            if latent and band:
                qi = qi0 + u
                starts = (jnp.maximum(qi - 1, 0) * tq, qi * tq, jnp.minimum(qi + 1, nq - 1) * tq)
                starts = [pl.multiple_of(s0, tq) for s0 in starts]
                kpos = lax.broadcasted_iota(I32, (3 * tq, nq_rows), 0)
                qpos = lax.broadcasted_iota(I32, (3 * tq, nq_rows), 1) & (tq - 1)
                rel = kpos - tq - qpos
                kmin = jnp.where(qi > 0, 0, tq)
                kmax = jnp.where(qi < nq - 1, 3 * tq, 2 * tq)
                band_ok = (rel >= -WINDOW) & (rel <= WINDOW) & (kpos >= kmin) & (kpos < kmax)
            for h in range(N_KV_HEADS):
                qs.append(q_ref[GQA_GROUP * h:GQA_GROUP * (h + 1), u * tq:(u + 1) * tq, :].reshape(nq_rows, LANES))
                ones_rows = (chan >= HEAD_DIM) if h == 0 else (chan < HEAD_DIM)
                if use_sink:
                    m0 = jnp.concatenate(
                        [jnp.full((1, tq), sink_ref[GQA_GROUP * h + g] * LOG2E, F32) for g in range(GQA_GROUP)],
                        axis=1)
                    acc0 = jnp.where(ones_rows, 1.0, 0.0)
                else:
                    m0 = jnp.full((1, nq_rows), NEG_INF, F32)
                    acc0 = jnp.zeros((LANES, nq_rows), F32)
                carries.append((m0, acc0))
                blk = [(lambda: kc_ref[...],
                        (lambda h=h: transposed(vc_ref[h])) if band else (lambda h=h: vc_ref[h]), None)]
                if latent and band:
                    blk.append((
                        lambda starts=starts: jnp.concatenate([kl_ref[pl.ds(s0, tq), :] for s0 in starts], axis=0),
                        lambda starts=starts, h=h: transposed(
                            jnp.concatenate([vl_ref[h, pl.ds(s0, tq), :] for s0 in starts], axis=0)),
                        band_ok))
                elif latent:
                    for c in range(seq // tk):
                        blk.append((lambda c=c: kl_ref[c * tk:(c + 1) * tk, :],
                                    lambda c=c, h=h: vl_ref[h, :, c * tk:(c + 1) * tk], None))
                blocks.append(blk)

        n_streams = len(qs)
        items = [(st, c) for c in range(len(blocks[0])) for st in range(n_streams)]

        def issue(item):
            st, c = item
            return scores(qs[st], blocks[st][c][0](), blocks[st][c][2])

        depth = LOOKAHEAD
        ahead = [issue(it) for it in items[:depth]]
        pending = []

        def value_matmul():
            st, alpha, p, vt = pending.pop(0)
            m, acc = carries[st]
            carries[st] = (m, alpha * acc + jnp.dot(vt, p, preferred_element_type=F32))

        for idx, (st, c) in enumerate(items):
            s_cur = ahead.pop(0)
            if idx + depth < len(items):
                ahead.append(issue(items[idx + depth]))
            if pv_delay and len(pending) >= pv_delay:
                value_matmul()
            m, acc = carries[st]
            m_new = jnp.maximum(m, _col_max(s_cur))
            pending.append((st, jnp.exp2(m - m_new), jnp.exp2(s_cur - m_new).astype(BF16), blocks[st][c][1]()))
            carries[st] = (m_new, acc)
            if not pv_delay:
                value_matmul()
        while pending:
            value_matmul()

        for st in range(n_streams):
            u, h = divmod(st, N_KV_HEADS)
            _, acc = carries[st]
            denom = acc[HEAD_DIM:HEAD_DIM + 1, :] if h == 0 else acc[0:1, :]
            o = jnp.transpose(acc * (1.0 / denom))
            og = [o[g * tq:(g + 1) * tq] for g in range(GQA_GROUP)]
            for c in range(GQA_GROUP // 2):
                e, f = og[2 * c], og[2 * c + 1]
                if h == 0:
                    chunk = jnp.where(lo64, e, pltpu.roll(f, HEAD_DIM, 1))
                else:
                    chunk = jnp.where(lo64, pltpu.roll(e, HEAD_DIM, 1), f)
                col = (h * (GQA_GROUP // 2) + c) * LANES
                o_ref[u * tq:(u + 1) * tq, col:col + LANES] = chunk.astype(BF16)

    if tile_off == 0:
        pl.when(is_lat)(lambda: run(True))
        pl.when(jnp.logical_not(is_lat))(lambda: run(False))
    else:
        run(True)


def _attention(sink, q, k, v, eye, *, band, use_sink, batch, n_ctx_per, seq, with_ctx_queries, tq, tk, n_sub):
    ntok = k.shape[0]
    n_ctx = batch * n_ctx_per
    assert n_ctx % seq == 0, "context rows must cover whole latent-sequence blocks"
    assert tq == WINDOW
    rows = n_sub * tq
    assert n_ctx_per % rows == 0 and seq % rows == 0
    nct = n_ctx // rows
    tps = seq // rows
    tile_off = 0 if with_ctx_queries else nct
    n_tiles = ntok // rows - tile_off

    def batch_of(t):
        i = t + tile_off
        return jnp.where(i >= nct, (i - nct) // tps, (i * rows) // n_ctx_per)

    ctx_idx = lambda t, *_: (batch_of(t), 0)
    lat_idx = lambda t, *_: (n_ctx // seq + batch_of(t), 0)
    if band:
        vc_spec = pl.BlockSpec((N_KV_HEADS, n_ctx_per, LANES), lambda t, *_: (0, batch_of(t), 0))
        vl_spec = pl.BlockSpec((N_KV_HEADS, seq, LANES), lambda t, *_: (0, n_ctx // seq + batch_of(t), 0))
    else:
        vc_spec = pl.BlockSpec((N_KV_HEADS, LANES, n_ctx_per), lambda t, *_: (0, 0, batch_of(t)))
        vl_spec = pl.BlockSpec((N_KV_HEADS, LANES, seq), lambda t, *_: (0, 0, n_ctx // seq + batch_of(t)))
    grid_spec = pltpu.PrefetchScalarGridSpec(
        num_scalar_prefetch=1,
        grid=(n_tiles,),
        in_specs=[
            pl.BlockSpec((N_Q_HEADS, rows, LANES), lambda t, *_: (0, t + tile_off, 0)),
            pl.BlockSpec((n_ctx_per, KV_WIDTH), ctx_idx),
            vc_spec,
            pl.BlockSpec((seq, KV_WIDTH), lat_idx),
            vl_spec,
            pl.BlockSpec((LANES, LANES), lambda t, *_: (0, 0)),
        ],
        out_specs=pl.BlockSpec((rows, ATTN_WIDTH), lambda t, *_: (t, 0)),
    )
    return pl.pallas_call(
        functools.partial(_attn_kernel, band=band, use_sink=use_sink, tile_off=tile_off,
                          n_ctx_tiles=nct, tiles_per_seq=tps, tk=tk, tq=tq, pv_delay=2 if band else 0),
        grid_spec=grid_spec,
        out_shape=jax.ShapeDtypeStruct((n_tiles * rows, ATTN_WIDTH), BF16),
        compiler_params=_cparams(("arbitrary",)),
        name="window_attention" if band else "global_attention",
    )(sink, q, k, v, k, v, eye)


def _merge_kernel(x_ref, h_ref, gr_ref, yg_ref, yw_ref, mg_ref, wb_ref, wo_ref, g1_ref, n2_ref, sh2_ref, sc2_ref,
                  rwh_ref, rwl_ref, x1_o, h2_o, lg_o):
    tm, d = x_ref.shape
    n_parts = 2 if tm % 256 == 0 else 1
    part = tm // n_parts
    rows = [slice(k * part, (k + 1) * part) for k in range(n_parts)]

    def branches(r):
        y_rnn = ((h_ref[0, r, :].astype(F32) + h_ref[1, r, :].astype(F32)) * gr_ref[r, :].astype(F32)).astype(BF16)
        ys = (y_rnn, yg_ref[r, :], yw_ref[r, :])
        merged = None
        for b in range(N_BRANCHES):
            t = mg_ref[r, b * d:(b + 1) * d].astype(F32) * jnp.dot(ys[b], wb_ref[b], preferred_element_type=F32)
            merged = t if merged is None else merged + t
        return merged.astype(BF16)

    merged = [branches(r) for r in rows]
    outs = [jnp.dot(m, wo_ref[...], preferred_element_type=F32) for m in merged]
    h2s = []
    for r, out in zip(rows, outs):
        x1 = x_ref[r, :] + g1_ref[0] * out
        x1_o[r, :] = x1
        h2 = _rms_mod(x1, n2_ref[...], sc2_ref[0], sh2_ref[0])
        h2_o[r, :] = h2
        h2s.append(h2)
    for r, h2 in zip(rows, h2s):
        hi = h2.astype(BF16)
        lo = (h2 - hi.astype(F32)).astype(BF16)
        lg = jnp.dot(hi, rwh_ref[...], preferred_element_type=F32) + (
            jnp.dot(lo, rwh_ref[...], preferred_element_type=F32) + jnp.dot(hi, rwl_ref[...], preferred_element_type=F32))
        lg_o[:, r] = jnp.transpose(lg)[:N_EXPERTS, :]


def _merge(x_all, hfb, gr, yg, yw, mg, wb, wo, mod_g1, n2, mod_sh2, mod_sc2, rw_hi, rw_lo, *, n_ctx, seq, with_ctx, tm):
    ntok, d = x_all.shape
    nct = n_ctx // tm
    tps = seq // tm
    off = 0 if with_ctx else nct
    n_tiles = ntok // tm - off
    n_out = n_tiles * tm

    def mod_idx(t):
        i = t + off
        return (jnp.where(i >= nct, 1 + (i - nct) // tps, 0), 0, 0)

    row_in = lambda t: (t + off, 0)
    row_out = lambda t: (t, 0)
    fixed2 = lambda t: (0, 0)
    return pl.pallas_call(
        _merge_kernel,
        grid=(n_tiles,),
        in_specs=[
            pl.BlockSpec((tm, d), row_in),
            pl.BlockSpec((2, tm, RNN_WIDTH), lambda t: (0, t + off, 0)),
            pl.BlockSpec((tm, RNN_WIDTH), row_in),
            pl.BlockSpec((tm, ATTN_WIDTH), row_out),
            pl.BlockSpec((tm, ATTN_WIDTH), row_out),
            pl.BlockSpec((tm, N_BRANCHES * d), row_in),
            pl.BlockSpec((N_BRANCHES, RNN_WIDTH, d), lambda t: (0, 0, 0)),
            pl.BlockSpec((d, d), fixed2),
            pl.BlockSpec((1, 1, d), mod_idx),
            pl.BlockSpec((1, d), fixed2),
            pl.BlockSpec((1, 1, d), mod_idx),
            pl.BlockSpec((1, 1, d), mod_idx),
            pl.BlockSpec((d, LANES), fixed2),
            pl.BlockSpec((d, LANES), fixed2),
        ],
        out_specs=(
            pl.BlockSpec((tm, d), row_out),
            pl.BlockSpec((tm, d), row_out),
            pl.BlockSpec((N_EXPERTS, tm), lambda t: (0, t)),
        ),
        out_shape=(
            jax.ShapeDtypeStruct((n_out, d), F32),
            jax.ShapeDtypeStruct((n_out, d), F32),
            jax.ShapeDtypeStruct((N_EXPERTS, n_out), F32),
        ),
        compiler_params=_cparams(("arbitrary",)),
        name="merge",
    )(x_all, hfb, gr, yg, yw, mg, wb, wo, mod_g1, n2, mod_sh2, mod_sc2, rw_hi, rw_lo)


def _first_argmax(vals):
    best, idx = vals[0], jnp.zeros(vals[0].shape, I32)
    for k in range(1, len(vals)):
        take = vals[k] > best
        best = jnp.where(take, vals[k], best)
        idx = jnp.where(take, k, idx)
    return best, idx


def _routing_kernel(lg_ref, rb_ref, tri_ref, bucket_o, rank_o, count_o, carry_scr):
    t = pl.program_id(0)

    @pl.when(t == 0)
    def _():
        carry_scr[...] = jnp.zeros_like(carry_scr)

    s = _sigmoid(lg_ref[...])
    sel = s + rb_ref[...]
    rows = [sel[e:e + 1, :] for e in range(N_EXPERTS)]
    tn = s.shape[1]

    grp_scores = []
    for g in range(N_GROUPS):
        r = rows[g * EXPERTS_PER_GROUP:(g + 1) * EXPERTS_PER_GROUP]
        best = None
        for a in range(EXPERTS_PER_GROUP):
            for b in range(a + 1, EXPERTS_PER_GROUP):
                pair = r[a] + r[b]
                best = pair if best is None else jnp.maximum(best, pair)
        grp_scores.append(best)
    _, grp = _first_argmax(grp_scores)

    vals = []
    for k in range(EXPERTS_PER_GROUP):
        v = rows[k]
        for g in range(1, N_GROUPS):
            v = jnp.where(grp == g, rows[g * EXPERTS_PER_GROUP + k], v)
        vals.append(v)
    _, i1 = _first_argmax(vals)
    _, i2 = _first_argmax([jnp.where(i1 == k, -jnp.inf, vals[k]) for k in range(EXPERTS_PER_GROUP)])
    lo = jnp.minimum(i1, i2)
    hi = jnp.maximum(i1, i2)
    pair = jnp.where(lo == 0, hi - 1, jnp.where(lo == 1, 6 - hi, 5))
    bucket = grp * N_PAIRS + pair
    bucket_o[...] = bucket

    onehot = (lax.broadcasted_iota(I32, (BUCKET_ROWS, tn), 0) == bucket).astype(F32)
    incl = jnp.dot(onehot.astype(BF16), tri_ref[...], preferred_element_type=F32)
    before = carry_scr[...] + incl - 1.0
    rank_o[...] = jnp.sum(onehot * before, axis=0, keepdims=True).astype(I32)
    carry_scr[...] = carry_scr[...] + jnp.sum(onehot, axis=1, keepdims=True)
    count_o[...] = jnp.broadcast_to(carry_scr[...], count_o.shape).astype(I32)


def _routing(logits_t, router_b, tri, *, tn):
    n = logits_t.shape[1]
    return pl.pallas_call(
        _routing_kernel,
        grid=(n // tn,),
        in_specs=[
            pl.BlockSpec((N_EXPERTS, tn), lambda t: (0, t)),
            pl.BlockSpec((N_EXPERTS, 1), lambda t: (0, 0)),
            pl.BlockSpec((tn, tn), lambda t: (0, 0)),
        ],
        out_specs=(
            pl.BlockSpec((1, tn), lambda t: (0, t)),
            pl.BlockSpec((1, tn), lambda t: (0, t)),
            pl.BlockSpec((BUCKET_ROWS, LANES), lambda t: (0, 0)),
        ),
        out_shape=(
            jax.ShapeDtypeStruct((1, n), I32),
            jax.ShapeDtypeStruct((1, n), I32),
            jax.ShapeDtypeStruct((BUCKET_ROWS, LANES), I32),
        ),
        scratch_shapes=[pltpu.VMEM((BUCKET_ROWS, 1), F32)],
        compiler_params=_cparams(("arbitrary",)),
        name="routing",
    )(logits_t, router_b, tri)


def _position_kernel(bucket_ref, rank_ref, base_ref, pos_o):
    bucket = bucket_ref[...]
    ids = lax.broadcasted_iota(I32, (BUCKET_ROWS, bucket.shape[1]), 0)
    base = jnp.sum(jnp.where(ids == bucket, base_ref[...], 0), axis=0, keepdims=True)
    pos_o[...] = base + rank_ref[...]


def _positions(bucket, rank, base, *, tn):
    n = bucket.shape[1]
    blk = pl.BlockSpec((1, tn), lambda t: (0, t))
    return pl.pallas_call(
        _position_kernel,
        grid=(n // tn,),
        in_specs=[blk, blk, pl.BlockSpec((BUCKET_ROWS, 1), lambda t: (0, 0))],
        out_specs=blk,
        out_shape=jax.ShapeDtypeStruct((1, n), I32),
        compiler_params=_cparams(("arbitrary",)),
        name="positions",
    )(bucket, rank, base)


def _row_gather(idx_ref, src, dst, sem, n):
    for r in range(n):
        pltpu.make_async_copy(src.at[pl.ds(idx_ref[0, 0, r], 1), :], dst.at[pl.ds(r, 1), :], sem).start(priority=r % 2)


def _row_copy_wait(src, dst, sem, n):
    def wait(r, _):
        pltpu.make_async_copy(src.at[pl.ds(0, 1), :], dst.at[pl.ds(0, 1), :], sem).wait()
        return 0

    lax.fori_loop(0, n, wait, 0, unroll=8)


def _dispatch_kernel(pos_ref, h_ref, init_ref, xs_ref, sem):
    del init_ref
    tm = h_ref.shape[0]

    for r in range(tm):
        pltpu.make_async_copy(h_ref.at[pl.ds(r, 1), :], xs_ref.at[pl.ds(pos_ref[0, 0, r], 1), :],
                              sem).start(priority=r % 2)
    _row_copy_wait(h_ref, xs_ref, sem, tm)


def _dispatch(pos3, h2, xs_init, *, tm):
    n, d = h2.shape
    return pl.pallas_call(
        _dispatch_kernel,
        grid=(n // tm,),
        in_specs=[
            pl.BlockSpec((1, 1, tm), lambda t: (t, 0, 0), memory_space=pltpu.SMEM),
            pl.BlockSpec((tm, d), lambda t: (t, 0)),
            pl.BlockSpec(memory_space=pl.ANY),
        ],
        out_specs=pl.BlockSpec(memory_space=pl.ANY),
        out_shape=jax.ShapeDtypeStruct(xs_init.shape, F32),
        scratch_shapes=[pltpu.SemaphoreType.DMA(())],
        input_output_aliases={2: 0},
        compiler_params=_cparams(("arbitrary",)),
        name="dispatch",
    )(pos3, h2, xs_init)


def _moe_kernel(ea_ref, eb_ref, act_ref, xs_ref, w1a, w3a, w2a, w1b, w3b, w2b, rwa, rwb, ys_ref):
    j = pl.program_id(0)

    @pl.when(act_ref[j] == 0)
    def _():
        ys_ref[...] = jnp.zeros_like(ys_ref)

    @pl.when(act_ref[j] != 0)
    def _():
        x = xs_ref[...]
        xb = x.astype(BF16)

        def gated(u, g):
            return ((u * _sigmoid(u)) * g).astype(BF16)

        ua = jnp.dot(xb, w1a[0], preferred_element_type=F32)
        ga = jnp.dot(xb, w3a[0], preferred_element_type=F32)
        ub = jnp.dot(xb, w1b[0], preferred_element_type=F32)
        gb = jnp.dot(xb, w3b[0], preferred_element_type=F32)
        ya = jnp.dot(gated(ua, ga), w2a[0], preferred_element_type=F32)
        yb = jnp.dot(gated(ub, gb), w2b[0], preferred_element_type=F32)
        sa = _sigmoid(jnp.sum(x * rwa[0], axis=-1, keepdims=True))
        sb = _sigmoid(jnp.sum(x * rwb[0], axis=-1, keepdims=True))
        inv = 1.0 / (sa + sb)
        ys_ref[...] = (sa * inv) * ya + (sb * inv) * yb


def _moe(ea, eb, act, xs, w1, w3, w2, rw3, *, tm):
    npad, d = xs.shape
    de = w1.shape[2]
    row = lambda j, *_: (j, 0)
    wa = lambda j, ea, eb, act: (ea[j], 0, 0)
    wb = lambda j, ea, eb, act: (eb[j], 0, 0)
    grid_spec = pltpu.PrefetchScalarGridSpec(
        num_scalar_prefetch=3,
        grid=(npad // tm,),
        in_specs=[
            pl.BlockSpec((tm, d), row),
            pl.BlockSpec((1, d, de), wa), pl.BlockSpec((1, d, de), wa), pl.BlockSpec((1, de, d), wa),
            pl.BlockSpec((1, d, de), wb), pl.BlockSpec((1, d, de), wb), pl.BlockSpec((1, de, d), wb),
            pl.BlockSpec((1, 1, d), wa), pl.BlockSpec((1, 1, d), wb),
        ],
        out_specs=pl.BlockSpec((tm, d), row),
    )
    return pl.pallas_call(
        _moe_kernel,
        grid_spec=grid_spec,
        out_shape=jax.ShapeDtypeStruct((npad, d), F32),
        compiler_params=_cparams(("arbitrary",)),
        name="expert_ffn",
    )(ea, eb, act, xs, w1, w3, w2, w1, w3, w2, rw3, rw3)


def _combine_kernel(pos_ref, nxt_ref, x_ref, g2_ref, ys_ref, o_ref, buf, sem):
    t = pl.program_id(0)
    tm = x_ref.shape[0]
    slot = t % 2

    @pl.when(t == 0)
    def _():
        _row_gather(pos_ref, ys_ref, buf.at[0], sem.at[0], tm)

    @pl.when(t + 1 < pl.num_programs(0))
    def _():
        _row_gather(nxt_ref, ys_ref, buf.at[1 - slot], sem.at[1 - slot], tm)

    _row_copy_wait(ys_ref, buf.at[slot], sem.at[slot], tm)
    o_ref[...] = x_ref[...] + g2_ref[0] * buf[slot]


def _combine(pos3, x1, mod_g2, ys, *, n_ctx, seq, with_ctx, tm):
    n, d = x1.shape
    nct = n_ctx // tm if with_ctx else 0
    tps = seq // tm

    def mod_idx(t):
        return (jnp.where(t >= nct, 1 + (t - nct) // tps, 0), 0, 0)

    return pl.pallas_call(
        _combine_kernel,
        grid=(n // tm,),
        in_specs=[
            pl.BlockSpec((1, 1, tm), lambda t: (t, 0, 0), memory_space=pltpu.SMEM),
            pl.BlockSpec((1, 1, tm), lambda t: (jnp.minimum(t + 1, n // tm - 1), 0, 0), memory_space=pltpu.SMEM),
            pl.BlockSpec((tm, d), lambda t: (t, 0)),
            pl.BlockSpec((1, 1, d), mod_idx),
            pl.BlockSpec(memory_space=pl.ANY),
        ],
        out_specs=pl.BlockSpec((tm, d), lambda t: (t, 0)),
        out_shape=jax.ShapeDtypeStruct((n, d), F32),
        scratch_shapes=[pltpu.VMEM((2, tm, d), F32), pltpu.SemaphoreType.DMA((2,))],
        compiler_params=_cparams(("arbitrary",)),
        name="combine",
    )(pos3, pos3, x1, mod_g2, ys)


def _block_diag(w):
    n, d, e = w.shape
    eye = jnp.eye(n, dtype=w.dtype)
    return (eye[:, None, :, None] * w[:, :, None, :]).reshape(n * d, n * e)


def _rope_tables(seq):
    rows = seq // GRID_W
    row = jnp.repeat(jnp.arange(rows, dtype=F32), GRID_W)
    col = jnp.tile(jnp.arange(GRID_W, dtype=F32), rows)
    n_freq = HEAD_DIM // 4
    inv = ROPE_BASE ** (-jnp.arange(n_freq, dtype=F32) / n_freq)
    ang = jnp.concatenate([row[:, None] * inv, col[:, None] * inv], axis=-1)
    cos, sin = jnp.cos(ang), jnp.sin(ang)
    reps = LANES // (HEAD_DIM // 2)
    sign = jnp.tile(jnp.concatenate([-jnp.ones((HEAD_DIM // 2,), F32), jnp.ones((HEAD_DIM // 2,), F32)]),
                    LANES // HEAD_DIM)
    return jnp.tile(cos, (1, reps)), jnp.tile(sin, (1, reps)) * sign


def _max_tiles(n_tokens, tm):
    return -(-(n_tokens + N_BUCKETS * (tm - 1)) // tm)


def _tile_plan(counts, n_tiles, tm):
    padded = ((counts + tm - 1) // tm) * tm
    ends = jnp.cumsum(padded)
    base = ends - padded
    tile_start = jnp.arange(n_tiles, dtype=I32) * tm
    tile_bucket = jnp.sum((tile_start[:, None] >= ends[None, :]).astype(I32), axis=1)
    active = (tile_bucket < N_BUCKETS).astype(I32)
    last_used = jnp.maximum(jnp.sum(active) - 1, 0)
    tile_bucket = jnp.where(active == 1, tile_bucket, tile_bucket[last_used])
    tile_bucket = jnp.minimum(tile_bucket, N_BUCKETS - 1)
    grp, pair = tile_bucket // N_PAIRS, tile_bucket % N_PAIRS
    ea = grp * EXPERTS_PER_GROUP + jnp.asarray(PAIR_SLOT_A, I32)[pair]
    eb = grp * EXPERTS_PER_GROUP + jnp.asarray(PAIR_SLOT_B, I32)[pair]
    return base, ea, eb, active


def kernel(x, c, ctx, c_ctx, w_mod, b_mod, norm1_g, norm2_g, w_in, b_merge, conv_w, conv_b, lru_wa, lru_ba,
           lru_wx, lru_bx, lru_lambda, q_norm_g, k_norm_g, sink, w_branch, w_out, router_w, router_b,
           expert_w1, expert_w3, expert_w2):
    batch, seq, d = x.shape
    n_ctx_per = ctx.shape[1]
    n_layers = w_mod.shape[0]
    n_ctx = batch * n_ctx_per
    n_lat = batch * seq
    tm = min(256, n_ctx_per)
    tm_mxu = min(512, n_ctx, seq)
    tq = 128
    tk = min(512, seq)
    tm_moe = 256
    tn_route = min(1024, n_ctx_per)

    n_mod_rows = -(-(batch + 1) // 8) * 8
    cvec = jnp.zeros((n_mod_rows, d), F32).at[0].set(c_ctx).at[1:batch + 1].set(c)
    mod = _modulation(cvec, w_mod, b_mod)[:, :batch + 1]
    mod = mod.reshape(n_layers, batch + 1, 6, 1, d)

    cos_t, sin_t = _rope_tables(seq)
    seg = jnp.kron(jnp.eye(LANES // HEAD_DIM, dtype=F32),
                   jnp.full((HEAD_DIM, HEAD_DIM), 1.0 / HEAD_DIM, F32)).astype(BF16)
    eye = jnp.eye(LANES, dtype=BF16)
    tri = (jnp.arange(tn_route)[:, None] <= jnp.arange(tn_route)[None, :]).astype(BF16)
    rw_t = router_w.T
    rw_pad = jnp.zeros((d, LANES), F32).at[:, :N_EXPERTS].set(router_w)
    rw_hi = rw_pad.astype(BF16)
    rw_lo = (rw_pad - rw_hi.astype(F32)).astype(BF16)
    rw3 = rw_t.reshape(N_EXPERTS, 1, d)
    rb = router_b.reshape(N_EXPERTS, 1)

    x_all = jnp.concatenate([ctx.reshape(n_ctx, d), x.reshape(n_lat, d)], axis=0)
    n_moe_tiles = _max_tiles(n_ctx + n_lat, tm_moe)
    sorted_rows = jnp.zeros((n_moe_tiles * tm_moe, d), F32)
    routed = None
    for l in range(n_layers):
        last = l == n_layers - 1
        sh1, sc1, g1, sh2, sc2, g2 = (mod[l, :, k] for k in range(6))
        qgain = jnp.tile(q_norm_g[l], (1, LANES // HEAD_DIM))
        kgain = jnp.tile(k_norm_g[l], (1, LANES // HEAD_DIM))
        proj_args = (sh1, sc1, norm1_g[l][None], w_in[l].astype(BF16), b_merge[l][None], qgain, kgain,
                     cos_t, sin_t, seg, eye)
        if routed is None:
            proj = _input_projection(x_all, *proj_args, n_ctx=n_ctx, seq=seq, tm=tm_mxu)
        else:
            proj, x_all = _input_projection(x_all, *proj_args, n_ctx=n_ctx, seq=seq, tm=tm_mxu, combine=routed)
        xr, gr, qg, kg, vg, qw, kw, vw, mg = proj
        hfb = _rglru(xr, conv_w[l], conv_b[l][None],
                     jax.vmap(_block_diag)(lru_wa[l]).astype(BF16), jax.vmap(_block_diag)(lru_wx[l]).astype(BF16),
                     lru_ba[l][:, None], lru_bx[l][:, None], lru_lambda[l][:, None],
                     batch=batch, n_ctx_per=n_ctx_per, seq=seq)
        attn_args = dict(batch=batch, n_ctx_per=n_ctx_per, seq=seq, with_ctx_queries=not last, tq=tq, tk=tk)
        yg = _attention(sink[l], qg, kg, vg, eye, band=False, use_sink=False, n_sub=1, **attn_args)
        yw = _attention(sink[l], qw, kw, vw, eye, band=True, use_sink=True, n_sub=min(2, n_ctx_per // tq),
                        **attn_args)
        x1, h2, logits_t = _merge(x_all, hfb, gr, yg, yw, mg, w_branch[l].astype(BF16), w_out[l].astype(BF16),
                                  g1, norm2_g[l][None], sh2, sc2, rw_hi, rw_lo,
                                  n_ctx=n_ctx, seq=seq, with_ctx=not last, tm=tm_mxu)
        n_tok = x1.shape[0]
        bucket, rank, counts = _routing(logits_t, rb, tri, tn=tn_route)
        base, ea, eb, active = _tile_plan(counts[:N_BUCKETS, 0], n_moe_tiles, tm_moe)
        base_col = jnp.zeros((BUCKET_ROWS, 1), I32).at[:N_BUCKETS, 0].set(base.astype(I32))
        pos = _positions(bucket, rank, base_col, tn=min(4096, n_ctx))
        pos3 = pos.reshape(n_tok // tm, 1, tm)
        xs = _dispatch(pos3, h2, sorted_rows, tm=tm)
        ys = _moe(ea, eb, active, xs, expert_w1[l].astype(BF16), expert_w3[l].astype(BF16),
                  expert_w2[l].astype(BF16), rw3, tm=tm_moe)
        sorted_rows = xs
        if last:
            x_all = _combine(pos3, x1, g2, ys, n_ctx=n_ctx, seq=seq, with_ctx=False, tm=tm)
        else:
            x_all, routed = x1, (pos.reshape(n_tok // tm_mxu, 1, tm_mxu), g2, ys)
    return x_all.reshape(batch, seq, d)
```

```python
import functools

import jax
import jax.numpy as jnp
from jax import lax
from jax.experimental import pallas as pl
from jax.experimental.pallas import tpu as pltpu

F32 = jnp.float32
BF16 = jnp.bfloat16
I32 = jnp.int32

HEAD_DIM = 64
N_Q_HEADS = 8
N_KV_HEADS = 2
GQA_GROUP = N_Q_HEADS // N_KV_HEADS
ATTN_WIDTH = N_Q_HEADS * HEAD_DIM
KV_WIDTH = N_KV_HEADS * HEAD_DIM
RNN_WIDTH = 512
RNN_BLOCKS = 8
CONV_WIDTH = 4
LRU_C = 8.0
WINDOW = 128
GRID_W = 64
ROPE_BASE = 10000.0
N_BRANCHES = 3
N_EXPERTS = 16
N_GROUPS = 4
EXPERTS_PER_GROUP = 4
EPS = 1e-6
NEG_INF = -1e30
V_ROWS = HEAD_DIM + 16
LOG2E = 1.4426950408889634

PAIR_SLOT_A = (0, 0, 0, 1, 1, 3)
PAIR_SLOT_B = (1, 2, 3, 3, 2, 2)
N_PAIRS = 6
N_BUCKETS = N_GROUPS * N_PAIRS
BUCKET_ROWS = 32

LANES = 128
SUBLANES = 8
VMEM_LIMIT = 56 * 1024 * 1024

HIGHEST = lax.Precision.HIGHEST


def _cparams(sem):
    return pltpu.CompilerParams(dimension_semantics=sem, vmem_limit_bytes=VMEM_LIMIT)


def _sigmoid(x):
    return 0.5 * jnp.tanh(0.5 * x) + 0.5


def _rms_mod(x, g, scale, shift):
    ms = jnp.mean(x * x, axis=-1, keepdims=True)
    return (x * lax.rsqrt(ms + EPS) * g) * (1.0 + scale) + shift


def _mod_kernel(a_ref, w_ref, b_ref, o_ref):
    a = a_ref[...]
    a = a * _sigmoid(a)
    o_ref[0] = jnp.dot(a, w_ref[0], precision=HIGHEST, preferred_element_type=F32) + b_ref[0]


def _modulation(cvec, w_mod, b_mod):
    rows, d = cvec.shape
    n_layers, _, n_out = w_mod.shape
    tn = 1536
    return pl.pallas_call(
        _mod_kernel,
        grid=(n_layers, n_out // tn),
        in_specs=[
            pl.BlockSpec((rows, d), lambda l, j: (0, 0)),
            pl.BlockSpec((1, d, tn), lambda l, j: (l, 0, j)),
            pl.BlockSpec((1, 1, tn), lambda l, j: (l, 0, j)),
        ],
        out_specs=pl.BlockSpec((1, rows, tn), lambda l, j: (l, 0, j)),
        out_shape=jax.ShapeDtypeStruct((n_layers, rows, n_out), F32),
        compiler_params=_cparams(("arbitrary", "arbitrary")),
        name="modulation",
    )(cvec, w_mod, b_mod.reshape(n_layers, 1, n_out))


def _head_norm_rope(p, gain, cos_t, sin_t, seg, lo32):
    ss = p * p
    hi = ss.astype(BF16)
    lo = (ss - hi.astype(F32)).astype(BF16)
    mean = jnp.dot(hi, seg, preferred_element_type=F32) + jnp.dot(lo, seg, preferred_element_type=F32)
    n = p * lax.rsqrt(mean + EPS) * gain
    partner = jnp.where(lo32, pltpu.roll(n, 96, 1), pltpu.roll(n, 32, 1))
    return n * cos_t + partner * sin_t


def _inproj_kernel(x_ref, *refs, n_ctx_tiles):
    _inproj_body(x_ref[...], *refs, n_ctx_tiles=n_ctx_tiles)


def _inproj_combine_kernel(pos_ref, nxt_ref, g2_ref, ys_ref, x1_ref, *refs, n_ctx_tiles):
    *refs, x_o, buf, sem = refs
    t = pl.program_id(0)
    tm = x1_ref.shape[0]
    slot = t % 2

    @pl.when(t == 0)
    def _():
        _row_gather(pos_ref, ys_ref, buf.at[0], sem.at[0], tm)

    _row_copy_wait(ys_ref, buf.at[slot], sem.at[slot], tm)
    x = x1_ref[...] + g2_ref[0] * buf[slot]
    x_o[...] = x

    def gather_part(n, n_parts):
        rows = tm // n_parts
        for r in range(n * rows, (n + 1) * rows):
            pltpu.make_async_copy(ys_ref.at[pl.ds(nxt_ref[0, 0, r], 1), :], buf.at[1 - slot, pl.ds(r, 1), :],
                                  sem.at[1 - slot]).start(priority=r % 2)

    _inproj_body(x, *refs, n_ctx_tiles=n_ctx_tiles, between_groups=gather_part)

    @pl.when(t == pl.num_programs(0) - 1)
    def _():
        _row_copy_wait(ys_ref, buf.at[1 - slot], sem.at[1 - slot], tm)


def _inproj_body(x, sh_ref, sc_ref, g_ref, w_ref, bm_ref, qgain_ref, kgain_ref, cos_ref, sin_ref, seg_ref,
                 eye_ref, xr_o, gr_o, qg_o, kg_o, vg_o, qw_o, kw_o, vw_o, mg_o, *, n_ctx_tiles, between_groups=None):
    i = pl.program_id(0)
    is_lat = i >= n_ctx_tiles
    h = _rms_mod(x, g_ref[...], sc_ref[0], sh_ref[0])
    hb = h.astype(BF16)
    tm = hb.shape[0]

    def proj(c0, width):
        return jnp.dot(hb, w_ref[:, c0:c0 + width], preferred_element_type=F32)

    lane = lax.broadcasted_iota(I32, (tm, LANES), 1)
    lo32 = (lane & (HEAD_DIM - 1)) < (HEAD_DIM // 2)
    lo64 = lane < HEAD_DIM
    cos_t = jnp.where(is_lat, cos_ref[...], 1.0)
    sin_t = jnp.where(is_lat, sin_ref[...], 0.0)
    seg = seg_ref[...]
    zero = jnp.zeros((tm, LANES), F32)
    d = x.shape[1]

    def rnn_epilogue(p):
        xr_o[...] = p[:, :RNN_WIDTH].astype(BF16)
        gr_o[...] = jax.nn.gelu(p[:, RNN_WIDTH:]).astype(BF16)

    def q_epilogue(a, q_o, p):
        qgain = qgain_ref[a:a + 1, :]
        for c in range(ATTN_WIDTH // LANES):
            y = _head_norm_rope(p[:, c * LANES:(c + 1) * LANES], qgain, cos_t, sin_t, seg, lo32)
            y = y * (HEAD_DIM ** -0.5 * LOG2E)
            yr = pltpu.roll(y, HEAD_DIM, 1)
            if c < 2:
                out_a, out_b = jnp.where(lo64, y, zero), jnp.where(lo64, yr, zero)
            else:
                out_a, out_b = jnp.where(lo64, zero, yr), jnp.where(lo64, zero, y)
            q_o[2 * c] = out_a.astype(BF16)
            q_o[2 * c + 1] = out_b.astype(BF16)

    def kv_epilogue(a, k_o, v_o, p):
        k_o[...] = _head_norm_rope(p[:, :KV_WIDTH], kgain_ref[a:a + 1, :], cos_t, sin_t, seg, lo32).astype(BF16)
        pv = p[:, KV_WIDTH:]
        heads = (jnp.where(lo64, pv, 1.0), jnp.where(lo64, pltpu.roll(pv, HEAD_DIM, 1), 1.0))
        for hd, vh in enumerate(heads):
            vh = vh.astype(BF16)
            if a == 0:
                vh = lax.dot_general(eye_ref[...], vh, (((1,), (1,)), ((), ())),
                                     preferred_element_type=F32).astype(BF16)[:V_ROWS]
            v_o[hd] = vh

    def gate_epilogue(b, p):
        mg_o[:, b * d:(b + 1) * d] = _sigmoid(p + bm_ref[:, b * d:(b + 1) * d]).astype(BF16)

    groups = [(0, 2 * RNN_WIDTH, rnn_epilogue)]
    base = 2 * RNN_WIDTH
    for a, (q_o, k_o, v_o) in enumerate(((qg_o, kg_o, vg_o), (qw_o, kw_o, vw_o))):
        groups.append((base, ATTN_WIDTH, functools.partial(q_epilogue, a, q_o)))
        groups.append((base + ATTN_WIDTH, 2 * KV_WIDTH, functools.partial(kv_epilogue, a, k_o, v_o)))
        base += ATTN_WIDTH + 2 * KV_WIDTH
    for b in range(N_BRANCHES):
        groups.append((base + b * d, d, functools.partial(gate_epilogue, b)))

    p_next = proj(groups[0][0], groups[0][1])
    for n, (_, _, epilogue) in enumerate(groups):
        p_cur = p_next
        if n + 1 < len(groups):
            p_next = proj(groups[n + 1][0], groups[n + 1][1])
        if between_groups is not None:
            between_groups(n, len(groups))
        epilogue(p_cur)


def _input_projection(x_all, mod_sh, mod_sc, g, w_in, b_merge, qgain, kgain, cos_t, sin_t, seg, eye, *, n_ctx, seq, tm,
                      combine=None):
    ntok, d = x_all.shape
    n_in = w_in.shape[1]
    nct = n_ctx // tm
    tps = seq // tm

    def mod_idx(i):
        return (jnp.where(i >= nct, 1 + (i - nct) // tps, 0), 0, 0)

    def rope_idx(i):
        return (jnp.where(i >= nct, (i - nct) % tps, 0), 0)

    row = lambda i: (i, 0)
    fixed = lambda i: (0, 0)
    out_shape = (
        jax.ShapeDtypeStruct((ntok, RNN_WIDTH), BF16),
        jax.ShapeDtypeStruct((ntok, RNN_WIDTH), BF16),
        jax.ShapeDtypeStruct((N_Q_HEADS, ntok, LANES), BF16),
        jax.ShapeDtypeStruct((ntok, KV_WIDTH), BF16),
        jax.ShapeDtypeStruct((N_KV_HEADS, V_ROWS, ntok), BF16),
        jax.ShapeDtypeStruct((N_Q_HEADS, ntok, LANES), BF16),
        jax.ShapeDtypeStruct((ntok, KV_WIDTH), BF16),
        jax.ShapeDtypeStruct((N_KV_HEADS, ntok, LANES), BF16),
        jax.ShapeDtypeStruct((ntok, N_BRANCHES * d), BF16),
    )
    q_spec = pl.BlockSpec((N_Q_HEADS, tm, LANES), lambda i: (0, i, 0))
    out_specs = (
        pl.BlockSpec((tm, RNN_WIDTH), row), pl.BlockSpec((tm, RNN_WIDTH), row),
        q_spec, pl.BlockSpec((tm, KV_WIDTH), row), pl.BlockSpec((N_KV_HEADS, V_ROWS, tm), lambda i: (0, 0, i)),
        q_spec, pl.BlockSpec((tm, KV_WIDTH), row), pl.BlockSpec((N_KV_HEADS, tm, LANES), lambda i: (0, i, 0)),
        pl.BlockSpec((tm, N_BRANCHES * d), row),
    )
    in_specs = [
        pl.BlockSpec((tm, d), row),
        pl.BlockSpec((1, 1, d), mod_idx),
        pl.BlockSpec((1, 1, d), mod_idx),
        pl.BlockSpec((1, d), fixed),
        pl.BlockSpec((d, n_in), fixed, pipeline_mode=pl.Buffered(1)),
        pl.BlockSpec((1, N_BRANCHES * d), fixed),
        pl.BlockSpec((2, LANES), fixed),
        pl.BlockSpec((2, LANES), fixed),
        pl.BlockSpec((tm, LANES), rope_idx),
        pl.BlockSpec((tm, LANES), rope_idx),
        pl.BlockSpec((LANES, LANES), fixed),
        pl.BlockSpec((LANES, LANES), fixed),
    ]
    common = (mod_sh, mod_sc, g, w_in, b_merge, qgain, kgain, cos_t, sin_t, seg, eye)
    if combine is None:
        return pl.pallas_call(
            functools.partial(_inproj_kernel, n_ctx_tiles=nct),
            grid=(ntok // tm,),
            in_specs=in_specs,
            out_specs=out_specs,
            out_shape=out_shape,
            compiler_params=_cparams(("arbitrary",)),
            name="input_projection",
        )(x_all, *common)
    pos3, mod_g2, ys = combine
    n_steps = ntok // tm
    smem_blk = lambda f: pl.BlockSpec((1, 1, tm), f, memory_space=pltpu.SMEM)
    outs = pl.pallas_call(
        functools.partial(_inproj_combine_kernel, n_ctx_tiles=nct),
        grid=(n_steps,),
        in_specs=[
            smem_blk(lambda i: (i, 0, 0)),
            smem_blk(lambda i: (jnp.minimum(i + 1, n_steps - 1), 0, 0)),
            pl.BlockSpec((1, 1, d), mod_idx),
            pl.BlockSpec(memory_space=pl.ANY),
        ] + in_specs,
        out_specs=out_specs + (pl.BlockSpec((tm, d), row),),
        out_shape=out_shape + (jax.ShapeDtypeStruct((ntok, d), F32),),
        scratch_shapes=[pltpu.VMEM((2, tm, d), F32), pltpu.SemaphoreType.DMA((2,))],
        compiler_params=_cparams(("arbitrary",)),
        name="input_projection_combine",
    )(pos3, pos3, mod_g2, ys, x_all, *common)
    return outs[:-1], outs[-1]


def _rglru_kernel(x_ref, prev_ref, next_ref, cw_ref, cb_ref, wa_ref, wx_ref, ba_ref, bx_ref, lam_ref,
                  o_ref, a_scr, b_scr, h_scr, y_scr, pe_scr, he_scr, c_scr, *, n_chunks):
    d = pl.program_id(0)
    j = pl.program_id(2)
    jj = jnp.where(d == 0, j, jnp.where(j == 0, 0, n_chunks - j))
    prev_valid = jj >= 2
    next_valid = (jj >= 1) & (jj <= n_chunks - 2)

    x = x_ref[...].astype(F32)
    tc = x.shape[0]
    hp = prev_ref[...].astype(F32)
    hn = next_ref[...].astype(F32)
    halo = prev_ref.shape[0]
    pm1 = jnp.where(prev_valid, hp[halo - 1:halo, :], 0.0)
    pm2 = jnp.where(prev_valid, hp[halo - 2:halo - 1, :], 0.0)
    nx0 = jnp.where(next_valid, hn[0:1, :], 0.0)
    row = lax.broadcasted_iota(I32, x.shape, 0)
    xm1 = jnp.where(row == 0, pm1, pltpu.roll(x, 1, 0))
    xm2 = jnp.where(row == 0, pm2, jnp.where(row == 1, pm1, pltpu.roll(x, 2, 0)))
    xp1 = jnp.where(row == tc - 1, nx0, pltpu.roll(x, tc - 1, 0))
    xc = (xm2 * cw_ref[0:1, :] + xm1 * cw_ref[1:2, :] + x * cw_ref[2:3, :] + xp1 * cw_ref[3:4, :]) + cb_ref[...]

    xb = xc.astype(BF16)
    r = _sigmoid(jnp.dot(xb, wa_ref[0], preferred_element_type=F32) + ba_ref[0])
    g = _sigmoid(jnp.dot(xb, wx_ref[0], preferred_element_type=F32) + bx_ref[0])
    neg_lam = -lam_ref[0]
    softplus = jnp.maximum(neg_lam, 0.0) + jnp.log(1.0 + jnp.exp(-jnp.abs(neg_lam)))
    a = jnp.exp(-LRU_C * r * softplus)
    bb = jnp.sqrt(1.0 - a * a) * (g * xc)
    n_lt = a.shape[1] // LANES
    for lt in range(n_lt):
        a_scr[lt * tc:(lt + 1) * tc, :] = a[:, lt * LANES:(lt + 1) * LANES]
        b_scr[lt * tc:(lt + 1) * tc, :] = bb[:, lt * LANES:(lt + 1) * LANES]

    @pl.when(j == 0)
    def _():
        h_scr[...] = jnp.zeros_like(h_scr)

    ng = tc // SUBLANES

    def scan(reverse):
        order = range(SUBLANES - 1, -1, -1) if reverse else range(SUBLANES)
        prod, loc = {}, {}
        for lt in range(n_lt):
            p = hl = None
            for k in order:
                ak = a_scr[pl.ds(lt * tc + k, ng, stride=SUBLANES), :]
                bk = b_scr[pl.ds(lt * tc + k, ng, stride=SUBLANES), :]
                p, hl = (ak, bk) if p is None else (ak * p, ak * hl + bk)
                prod[lt, k], loc[lt, k] = p, hl
            pe_scr[lt * ng:(lt + 1) * ng, :] = p
            he_scr[lt * ng:(lt + 1) * ng, :] = hl
        c = [h_scr[lt:lt + 1, :] for lt in range(n_lt)]
        for g in (range(ng - 1, -1, -1) if reverse else range(ng)):
            for lt in range(n_lt):
                r = lt * ng + g
                c_scr[r:r + 1, :] = c[lt]
                c[lt] = pe_scr[r:r + 1, :] * c[lt] + he_scr[r:r + 1, :]
        for lt in range(n_lt):
            h_scr[lt:lt + 1, :] = c[lt]
            carry_in = c_scr[lt * ng:(lt + 1) * ng, :]
            for k in order:
                y_scr[pl.ds(lt * tc + k, ng, stride=SUBLANES), :] = loc[lt, k] + prod[lt, k] * carry_in

    pl.when(d == 0)(lambda: scan(False))
    pl.when(d != 0)(lambda: scan(True))
    for lt in range(n_lt):
        o_ref[0, :, lt * LANES:(lt + 1) * LANES] = y_scr[lt * tc:(lt + 1) * tc, :].astype(BF16)


def _rglru(xr, conv_w, conv_b, wa, wx, ba, bx, lam, *, batch, n_ctx_per, seq):
    ntok, width = xr.shape
    tc = n_ctx_per
    halo = 16
    n_chunks = 1 + seq // tc
    n_lt = width // LANES
    lat0 = batch

    def seq_chunk(d, j):
        return jnp.where(d == 0, j, jnp.where(j == 0, 0, n_chunks - j))

    def chunk_blk(d, b, j):
        jj = seq_chunk(d, j)
        return jnp.where(jj == 0, b, lat0 + b * (seq // tc) + jj - 1)

    def x_idx(d, b, j):
        return (chunk_blk(d, b, j), 0)

    def prev_idx(d, b, j):
        return (jnp.maximum(chunk_blk(d, b, j) * (tc // halo) - 1, 0), 0)

    def next_idx(d, b, j):
        return (jnp.minimum((chunk_blk(d, b, j) + 1) * (tc // halo), ntok // halo - 1), 0)

    per_dir = lambda d, b, j: (d, 0, 0)
    fixed = lambda d, b, j: (0, 0)
    return pl.pallas_call(
        functools.partial(_rglru_kernel, n_chunks=n_chunks),
        grid=(2, batch, n_chunks),
        in_specs=[
            pl.BlockSpec((tc, width), x_idx),
            pl.BlockSpec((halo, width), prev_idx),
            pl.BlockSpec((halo, width), next_idx),
            pl.BlockSpec((CONV_WIDTH, width), fixed),
            pl.BlockSpec((1, width), fixed),
            pl.BlockSpec((1, width, width), per_dir),
            pl.BlockSpec((1, width, width), per_dir),
            pl.BlockSpec((1, 1, width), per_dir),
            pl.BlockSpec((1, 1, width), per_dir),
            pl.BlockSpec((1, 1, width), per_dir),
        ],
        out_specs=pl.BlockSpec((1, tc, width), lambda d, b, j: (d, chunk_blk(d, b, j), 0)),
        out_shape=jax.ShapeDtypeStruct((2, ntok, width), BF16),
        scratch_shapes=[
            pltpu.VMEM((n_lt * tc, LANES), F32), pltpu.VMEM((n_lt * tc, LANES), F32),
            pltpu.VMEM((n_lt, LANES), F32), pltpu.VMEM((n_lt * tc, LANES), F32),
            pltpu.VMEM((n_lt * tc // SUBLANES, LANES), F32), pltpu.VMEM((n_lt * tc // SUBLANES, LANES), F32),
            pltpu.VMEM((n_lt * tc // SUBLANES, LANES), F32),
        ],
        compiler_params=_cparams(("arbitrary", "arbitrary", "arbitrary")),
        name="rglru",
    )(xr, xr, xr, conv_w, conv_b, wa, wx, ba, bx, lam)


LOOKAHEAD = 2


def _col_max(s):
    parts = [s[r:r + 8] for r in range(0, s.shape[0], 8)]
    while len(parts) > 1:
        nxt = [jnp.maximum(parts[k], parts[k + 1]) for k in range(0, len(parts) - 1, 2)]
        if len(parts) % 2:
            nxt.append(parts[-1])
        parts = nxt
    return jnp.max(parts[0], axis=0, keepdims=True)


def _attn_kernel(sink_ref, q_ref, kc_ref, vc_ref, kl_ref, vl_ref, eye_ref, o_ref, *,
                 band, use_sink, tile_off, n_ctx_tiles, tiles_per_seq, tk, tq, pv_delay):
    i = pl.program_id(0) + tile_off
    is_lat = i >= n_ctx_tiles
    n_sub = q_ref.shape[1] // tq
    qi0 = jnp.where(is_lat, (i - n_ctx_tiles) % tiles_per_seq, 0) * n_sub
    nq = tiles_per_seq * n_sub
    nq_rows = GQA_GROUP * tq
    seq = kl_ref.shape[0]
    contract_last = (((1,), (1,)), ((), ()))
    lo64 = lax.broadcasted_iota(I32, (tq, LANES), 1) < HEAD_DIM
    chan = lax.broadcasted_iota(I32, (V_ROWS, nq_rows), 0)

    def transposed(v):
        return lax.dot_general(eye_ref[:V_ROWS, :], v, contract_last, preferred_element_type=F32).astype(BF16)

    def scores(q, k, mask):
        s = lax.dot_general(k, q, contract_last, preferred_element_type=F32)
        return s if mask is None else jnp.where(mask, s, NEG_INF)

    def absorb(carry, s, vt):
        m, acc = carry
        m_new = jnp.maximum(m, _col_max(s))
        p = jnp.exp2(s - m_new)
        acc = jnp.exp2(m - m_new) * acc + jnp.dot(vt, p.astype(BF16), preferred_element_type=F32)
        return m_new, acc

    def run(latent):
        qs, carries, blocks = [], [], []
        for u in range(n_sub):
            if latent and band:
                qi = qi0 + u
                starts = (jnp.maximum(qi - 1, 0) * tq, qi * tq, jnp.minimum(qi + 1, nq - 1) * tq)
                starts = [pl.multiple_of(s0, tq) for s0 in starts]
                kpos = lax.broadcasted_iota(I32, (3 * tq, nq_rows), 0)
                qpos = lax.broadcasted_iota(I32, (3 * tq, nq_rows), 1) & (tq - 1)
                rel = kpos - tq - qpos
                kmin = jnp.where(qi > 0, 0, tq)
                kmax = jnp.where(qi < nq - 1, 3 * tq, 2 * tq)
                band_ok = (rel >= -WINDOW) & (rel <= WINDOW) & (kpos >= kmin) & (kpos < kmax)
            for h in range(N_KV_HEADS):
                qs.append(q_ref[GQA_GROUP * h:GQA_GROUP * (h + 1), u * tq:(u + 1) * tq, :].reshape(nq_rows, LANES))
                if use_sink:
                    m0 = jnp.concatenate(
                        [jnp.full((1, tq), sink_ref[GQA_GROUP * h + g] * LOG2E, F32) for g in range(GQA_GROUP)],
                        axis=1)
                    acc0 = jnp.where(chan >= HEAD_DIM, 1.0, 0.0)
                else:
                    m0 = jnp.full((1, nq_rows), NEG_INF, F32)
                    acc0 = jnp.zeros((V_ROWS, nq_rows), F32)
                carries.append((m0, acc0))
                blk = [(lambda: kc_ref[...],
                        (lambda h=h: transposed(vc_ref[h])) if band else (lambda h=h: vc_ref[h]), None)]
                if latent and band:
                    blk.append((
                        lambda starts=starts: jnp.concatenate([kl_ref[pl.ds(s0, tq), :] for s0 in starts], axis=0),
                        lambda starts=starts, h=h: transposed(
                            jnp.concatenate([vl_ref[h, pl.ds(s0, tq), :] for s0 in starts], axis=0)),
                        band_ok))
                elif latent:
                    for c in range(seq // tk):
                        blk.append((lambda c=c: kl_ref[c * tk:(c + 1) * tk, :],
                                    lambda c=c, h=h: vl_ref[h, :, c * tk:(c + 1) * tk], None))
                blocks.append(blk)

        n_streams = len(qs)
        items = [(st, c) for c in range(len(blocks[0])) for st in range(n_streams)]

        def issue(item):
            st, c = item
            return scores(qs[st], blocks[st][c][0](), blocks[st][c][2])

        depth = LOOKAHEAD if band else LOOKAHEAD * n_sub
        ahead = [issue(it) for it in items[:depth]]
        pending = []

        def value_matmul():
            st, alpha, p, vt = pending.pop(0)
            m, acc = carries[st]
            carries[st] = (m, alpha * acc + jnp.dot(vt, p, preferred_element_type=F32))

        for idx, (st, c) in enumerate(items):
            s_cur = ahead.pop(0)
            if idx + depth < len(items):
                ahead.append(issue(items[idx + depth]))
            if pv_delay and len(pending) >= pv_delay:
                value_matmul()
            m, acc = carries[st]
            m_new = jnp.maximum(m, _col_max(s_cur))
            pending.append((st, jnp.exp2(m - m_new), jnp.exp2(s_cur - m_new).astype(BF16), blocks[st][c][1]()))
            carries[st] = (m_new, acc)
            if not pv_delay:
                value_matmul()
        while pending:
            value_matmul()

        for st in range(n_streams):
            u, h = divmod(st, N_KV_HEADS)
            _, acc = carries[st]
            o = acc * (1.0 / acc[HEAD_DIM:HEAD_DIM + 1, :])
            o = jnp.concatenate([o, jnp.zeros((LANES - V_ROWS, nq_rows), F32)], axis=0)
            o = jnp.transpose(o)
            og = [o[g * tq:(g + 1) * tq] for g in range(GQA_GROUP)]
            for c in range(GQA_GROUP // 2):
                chunk = jnp.where(lo64, og[2 * c], pltpu.roll(og[2 * c + 1], HEAD_DIM, 1))
                col = (h * (GQA_GROUP // 2) + c) * LANES
                o_ref[u * tq:(u + 1) * tq, col:col + LANES] = chunk.astype(BF16)

    if tile_off == 0:
        pl.when(is_lat)(lambda: run(True))
        pl.when(jnp.logical_not(is_lat))(lambda: run(False))
    else:
        run(True)


def _attention(sink, q, k, v, eye, *, band, use_sink, batch, n_ctx_per, seq, with_ctx_queries, tq, tk, n_sub):
    ntok = k.shape[0]
    n_ctx = batch * n_ctx_per
    assert n_ctx % seq == 0, "context rows must cover whole latent-sequence blocks"
    assert tq == WINDOW
    rows = n_sub * tq
    assert n_ctx_per % rows == 0 and seq % rows == 0
    nct = n_ctx // rows
    tps = seq // rows
    tile_off = 0 if with_ctx_queries else nct
    n_tiles = ntok // rows - tile_off

    def batch_of(t):
        i = t + tile_off
        return jnp.where(i >= nct, (i - nct) // tps, (i * rows) // n_ctx_per)

    ctx_idx = lambda t, *_: (batch_of(t), 0)
    lat_idx = lambda t, *_: (n_ctx // seq + batch_of(t), 0)
    if band:
        vc_spec = pl.BlockSpec((N_KV_HEADS, n_ctx_per, LANES), lambda t, *_: (0, batch_of(t), 0))
        vl_spec = pl.BlockSpec((N_KV_HEADS, seq, LANES), lambda t, *_: (0, n_ctx // seq + batch_of(t), 0))
    else:
        vc_spec = pl.BlockSpec((N_KV_HEADS, V_ROWS, n_ctx_per), lambda t, *_: (0, 0, batch_of(t)))
        vl_spec = pl.BlockSpec((N_KV_HEADS, V_ROWS, seq), lambda t, *_: (0, 0, n_ctx // seq + batch_of(t)))
    grid_spec = pltpu.PrefetchScalarGridSpec(
        num_scalar_prefetch=1,
        grid=(n_tiles,),
        in_specs=[
            pl.BlockSpec((N_Q_HEADS, rows, LANES), lambda t, *_: (0, t + tile_off, 0)),
            pl.BlockSpec((n_ctx_per, KV_WIDTH), ctx_idx),
            vc_spec,
            pl.BlockSpec((seq, KV_WIDTH), lat_idx),
            vl_spec,
            pl.BlockSpec((LANES, LANES), lambda t, *_: (0, 0)),
        ],
        out_specs=pl.BlockSpec((rows, ATTN_WIDTH), lambda t, *_: (t, 0)),
    )
    return pl.pallas_call(
        functools.partial(_attn_kernel, band=band, use_sink=use_sink, tile_off=tile_off,
                          n_ctx_tiles=nct, tiles_per_seq=tps, tk=tk, tq=tq, pv_delay=2),
        grid_spec=grid_spec,
        out_shape=jax.ShapeDtypeStruct((n_tiles * rows, ATTN_WIDTH), BF16),
        compiler_params=_cparams(("arbitrary",)),
        name="window_attention" if band else "global_attention",
    )(sink, q, k, v, k, v, eye)


def _merge_kernel(x_ref, h_ref, gr_ref, yg_ref, yw_ref, mg_ref, wb_ref, wo_ref, g1_ref, n2_ref, sh2_ref, sc2_ref,
                  rwh_ref, rwl_ref, x1_o, h2_o, lg_o):
    tm, d = x_ref.shape
    n_parts = 2 if tm % 256 == 0 else 1
    part = tm // n_parts
    rows = [slice(k * part, (k + 1) * part) for k in range(n_parts)]

    def branches(r):
        y_rnn = ((h_ref[0, r, :].astype(F32) + h_ref[1, r, :].astype(F32)) * gr_ref[r, :].astype(F32)).astype(BF16)
        ys = (y_rnn, yg_ref[r, :], yw_ref[r, :])
        merged = None
        for b in range(N_BRANCHES):
            t = mg_ref[r, b * d:(b + 1) * d].astype(F32) * jnp.dot(ys[b], wb_ref[b], preferred_element_type=F32)
            merged = t if merged is None else merged + t
        return merged.astype(BF16)

    merged = [branches(r) for r in rows]
    outs = [jnp.dot(m, wo_ref[...], preferred_element_type=F32) for m in merged]
    h2s = []
    for r, out in zip(rows, outs):
        x1 = x_ref[r, :] + g1_ref[0] * out
        x1_o[r, :] = x1
        h2 = _rms_mod(x1, n2_ref[...], sc2_ref[0], sh2_ref[0])
        h2_o[r, :] = h2
        h2s.append(h2)
    for r, h2 in zip(rows, h2s):
        hi = h2.astype(BF16)
        lo = (h2 - hi.astype(F32)).astype(BF16)
        lg = jnp.dot(hi, rwh_ref[...], preferred_element_type=F32) + (
            jnp.dot(lo, rwh_ref[...], preferred_element_type=F32) + jnp.dot(hi, rwl_ref[...], preferred_element_type=F32))
        lg_o[:, r] = jnp.transpose(lg)[:N_EXPERTS, :]


def _merge(x_all, hfb, gr, yg, yw, mg, wb, wo, mod_g1, n2, mod_sh2, mod_sc2, rw_hi, rw_lo, *, n_ctx, seq, with_ctx, tm):
    ntok, d = x_all.shape
    nct = n_ctx // tm
    tps = seq // tm
    off = 0 if with_ctx else nct
    n_tiles = ntok // tm - off
    n_out = n_tiles * tm

    def mod_idx(t):
        i = t + off
        return (jnp.where(i >= nct, 1 + (i - nct) // tps, 0), 0, 0)

    row_in = lambda t: (t + off, 0)
    row_out = lambda t: (t, 0)
    fixed2 = lambda t: (0, 0)
    return pl.pallas_call(
        _merge_kernel,
        grid=(n_tiles,),
        in_specs=[
            pl.BlockSpec((tm, d), row_in),
            pl.BlockSpec((2, tm, RNN_WIDTH), lambda t: (0, t + off, 0)),
            pl.BlockSpec((tm, RNN_WIDTH), row_in),
            pl.BlockSpec((tm, ATTN_WIDTH), row_out),
            pl.BlockSpec((tm, ATTN_WIDTH), row_out),
            pl.BlockSpec((tm, N_BRANCHES * d), row_in),
            pl.BlockSpec((N_BRANCHES, RNN_WIDTH, d), lambda t: (0, 0, 0)),
            pl.BlockSpec((d, d), fixed2),
            pl.BlockSpec((1, 1, d), mod_idx),
            pl.BlockSpec((1, d), fixed2),
            pl.BlockSpec((1, 1, d), mod_idx),
            pl.BlockSpec((1, 1, d), mod_idx),
            pl.BlockSpec((d, LANES), fixed2),
            pl.BlockSpec((d, LANES), fixed2),
        ],
        out_specs=(
            pl.BlockSpec((tm, d), row_out),
            pl.BlockSpec((tm, d), row_out),
            pl.BlockSpec((N_EXPERTS, tm), lambda t: (0, t)),
        ),
        out_shape=(
            jax.ShapeDtypeStruct((n_out, d), F32),
            jax.ShapeDtypeStruct((n_out, d), F32),
            jax.ShapeDtypeStruct((N_EXPERTS, n_out), F32),
        ),
        compiler_params=_cparams(("arbitrary",)),
        name="merge",
    )(x_all, hfb, gr, yg, yw, mg, wb, wo, mod_g1, n2, mod_sh2, mod_sc2, rw_hi, rw_lo)


def _first_argmax(vals):
    best, idx = vals[0], jnp.zeros(vals[0].shape, I32)
    for k in range(1, len(vals)):
        take = vals[k] > best
        best = jnp.where(take, vals[k], best)
        idx = jnp.where(take, k, idx)
    return best, idx


def _routing_kernel(lg_ref, rb_ref, tri_ref, bucket_o, rank_o, count_o, carry_scr):
    t = pl.program_id(0)

    @pl.when(t == 0)
    def _():
        carry_scr[...] = jnp.zeros_like(carry_scr)

    s = _sigmoid(lg_ref[...])
    sel = s + rb_ref[...]
    rows = [sel[e:e + 1, :] for e in range(N_EXPERTS)]
    tn = s.shape[1]

    grp_scores = []
    for g in range(N_GROUPS):
        r = rows[g * EXPERTS_PER_GROUP:(g + 1) * EXPERTS_PER_GROUP]
        best = None
        for a in range(EXPERTS_PER_GROUP):
            for b in range(a + 1, EXPERTS_PER_GROUP):
                pair = r[a] + r[b]
                best = pair if best is None else jnp.maximum(best, pair)
        grp_scores.append(best)
    _, grp = _first_argmax(grp_scores)

    vals = []
    for k in range(EXPERTS_PER_GROUP):
        v = rows[k]
        for g in range(1, N_GROUPS):
            v = jnp.where(grp == g, rows[g * EXPERTS_PER_GROUP + k], v)
        vals.append(v)
    _, i1 = _first_argmax(vals)
    _, i2 = _first_argmax([jnp.where(i1 == k, -jnp.inf, vals[k]) for k in range(EXPERTS_PER_GROUP)])
    lo = jnp.minimum(i1, i2)
    hi = jnp.maximum(i1, i2)
    pair = jnp.where(lo == 0, hi - 1, jnp.where(lo == 1, 6 - hi, 5))
    bucket = grp * N_PAIRS + pair
    bucket_o[...] = bucket

    onehot = (lax.broadcasted_iota(I32, (BUCKET_ROWS, tn), 0) == bucket).astype(F32)
    incl = jnp.dot(onehot.astype(BF16), tri_ref[...], preferred_element_type=F32)
    before = carry_scr[...] + incl - 1.0
    rank_o[...] = jnp.sum(onehot * before, axis=0, keepdims=True).astype(I32)
    carry_scr[...] = carry_scr[...] + jnp.sum(onehot, axis=1, keepdims=True)
    count_o[...] = jnp.broadcast_to(carry_scr[...], count_o.shape).astype(I32)


def _routing(logits_t, router_b, tri, *, tn):
    n = logits_t.shape[1]
    return pl.pallas_call(
        _routing_kernel,
        grid=(n // tn,),
        in_specs=[
            pl.BlockSpec((N_EXPERTS, tn), lambda t: (0, t)),
            pl.BlockSpec((N_EXPERTS, 1), lambda t: (0, 0)),
            pl.BlockSpec((tn, tn), lambda t: (0, 0)),
        ],
        out_specs=(
            pl.BlockSpec((1, tn), lambda t: (0, t)),
            pl.BlockSpec((1, tn), lambda t: (0, t)),
            pl.BlockSpec((BUCKET_ROWS, LANES), lambda t: (0, 0)),
        ),
        out_shape=(
            jax.ShapeDtypeStruct((1, n), I32),
            jax.ShapeDtypeStruct((1, n), I32),
            jax.ShapeDtypeStruct((BUCKET_ROWS, LANES), I32),
        ),
        scratch_shapes=[pltpu.VMEM((BUCKET_ROWS, 1), F32)],
        compiler_params=_cparams(("arbitrary",)),
        name="routing",
    )(logits_t, router_b, tri)


def _position_kernel(bucket_ref, rank_ref, base_ref, pos_o):
    bucket = bucket_ref[...]
    ids = lax.broadcasted_iota(I32, (BUCKET_ROWS, bucket.shape[1]), 0)
    base = jnp.sum(jnp.where(ids == bucket, base_ref[...], 0), axis=0, keepdims=True)
    pos_o[...] = base + rank_ref[...]


def _positions(bucket, rank, base, *, tn):
    n = bucket.shape[1]
    blk = pl.BlockSpec((1, tn), lambda t: (0, t))
    return pl.pallas_call(
        _position_kernel,
        grid=(n // tn,),
        in_specs=[blk, blk, pl.BlockSpec((BUCKET_ROWS, 1), lambda t: (0, 0))],
        out_specs=blk,
        out_shape=jax.ShapeDtypeStruct((1, n), I32),
        compiler_params=_cparams(("arbitrary",)),
        name="positions",
    )(bucket, rank, base)


def _row_gather(idx_ref, src, dst, sem, n):
    for r in range(n):
        pltpu.make_async_copy(src.at[pl.ds(idx_ref[0, 0, r], 1), :], dst.at[pl.ds(r, 1), :], sem).start(priority=r % 2)


def _row_copy_wait(src, dst, sem, n):
    def wait(r, _):
        pltpu.make_async_copy(src.at[pl.ds(0, 1), :], dst.at[pl.ds(0, 1), :], sem).wait()
        return 0

    lax.fori_loop(0, n, wait, 0, unroll=8)


def _dispatch_kernel(pos_ref, h_ref, init_ref, xs_ref, sem):
    del init_ref
    tm = h_ref.shape[0]

    for r in range(tm):
        pltpu.make_async_copy(h_ref.at[pl.ds(r, 1), :], xs_ref.at[pl.ds(pos_ref[0, 0, r], 1), :],
                              sem).start(priority=r % 2)
    _row_copy_wait(h_ref, xs_ref, sem, tm)


def _dispatch(pos3, h2, xs_init, *, tm):
    n, d = h2.shape
    return pl.pallas_call(
        _dispatch_kernel,
        grid=(n // tm,),
        in_specs=[
            pl.BlockSpec((1, 1, tm), lambda t: (t, 0, 0), memory_space=pltpu.SMEM),
            pl.BlockSpec((tm, d), lambda t: (t, 0)),
            pl.BlockSpec(memory_space=pl.ANY),
        ],
        out_specs=pl.BlockSpec(memory_space=pl.ANY),
        out_shape=jax.ShapeDtypeStruct(xs_init.shape, F32),
        scratch_shapes=[pltpu.SemaphoreType.DMA(())],
        input_output_aliases={2: 0},
        compiler_params=_cparams(("arbitrary",)),
        name="dispatch",
    )(pos3, h2, xs_init)


def _moe_kernel(ea_ref, eb_ref, act_ref, xs_ref, w1a, w3a, w2a, w1b, w3b, w2b, rwa, rwb, ys_ref):
    j = pl.program_id(0)

    @pl.when(act_ref[j] == 0)
    def _():
        ys_ref[...] = jnp.zeros_like(ys_ref)

    @pl.when(act_ref[j] != 0)
    def _():
        x = xs_ref[...]
        xb = x.astype(BF16)

        def gated(u, g):
            return ((u * _sigmoid(u)) * g).astype(BF16)

        ua = jnp.dot(xb, w1a[0], preferred_element_type=F32)
        ga = jnp.dot(xb, w3a[0], preferred_element_type=F32)
        ub = jnp.dot(xb, w1b[0], preferred_element_type=F32)
        gb = jnp.dot(xb, w3b[0], preferred_element_type=F32)
        ya = jnp.dot(gated(ua, ga), w2a[0], preferred_element_type=F32)
        yb = jnp.dot(gated(ub, gb), w2b[0], preferred_element_type=F32)
        sa = _sigmoid(jnp.sum(x * rwa[0], axis=-1, keepdims=True))
        sb = _sigmoid(jnp.sum(x * rwb[0], axis=-1, keepdims=True))
        inv = 1.0 / (sa + sb)
        ys_ref[...] = (sa * inv) * ya + (sb * inv) * yb


def _moe(ea, eb, act, xs, w1, w3, w2, rw3, *, tm):
    npad, d = xs.shape
    de = w1.shape[2]
    row = lambda j, *_: (j, 0)
    wa = lambda j, ea, eb, act: (ea[j], 0, 0)
    wb = lambda j, ea, eb, act: (eb[j], 0, 0)
    grid_spec = pltpu.PrefetchScalarGridSpec(
        num_scalar_prefetch=3,
        grid=(npad // tm,),
        in_specs=[
            pl.BlockSpec((tm, d), row),
            pl.BlockSpec((1, d, de), wa), pl.BlockSpec((1, d, de), wa), pl.BlockSpec((1, de, d), wa),
            pl.BlockSpec((1, d, de), wb), pl.BlockSpec((1, d, de), wb), pl.BlockSpec((1, de, d), wb),
            pl.BlockSpec((1, 1, d), wa), pl.BlockSpec((1, 1, d), wb),
        ],
        out_specs=pl.BlockSpec((tm, d), row),
    )
    return pl.pallas_call(
        _moe_kernel,
        grid_spec=grid_spec,
        out_shape=jax.ShapeDtypeStruct((npad, d), F32),
        compiler_params=_cparams(("arbitrary",)),
        name="expert_ffn",
    )(ea, eb, act, xs, w1, w3, w2, w1, w3, w2, rw3, rw3)


def _combine_kernel(pos_ref, nxt_ref, x_ref, g2_ref, ys_ref, o_ref, buf, sem):
    t = pl.program_id(0)
    tm = x_ref.shape[0]
    slot = t % 2

    @pl.when(t == 0)
    def _():
        _row_gather(pos_ref, ys_ref, buf.at[0], sem.at[0], tm)

    @pl.when(t + 1 < pl.num_programs(0))
    def _():
        _row_gather(nxt_ref, ys_ref, buf.at[1 - slot], sem.at[1 - slot], tm)

    _row_copy_wait(ys_ref, buf.at[slot], sem.at[slot], tm)
    o_ref[...] = x_ref[...] + g2_ref[0] * buf[slot]


def _combine(pos3, x1, mod_g2, ys, *, n_ctx, seq, with_ctx, tm):
    n, d = x1.shape
    nct = n_ctx // tm if with_ctx else 0
    tps = seq // tm

    def mod_idx(t):
        return (jnp.where(t >= nct, 1 + (t - nct) // tps, 0), 0, 0)

    return pl.pallas_call(
        _combine_kernel,
        grid=(n // tm,),
        in_specs=[
            pl.BlockSpec((1, 1, tm), lambda t: (t, 0, 0), memory_space=pltpu.SMEM),
            pl.BlockSpec((1, 1, tm), lambda t: (jnp.minimum(t + 1, n // tm - 1), 0, 0), memory_space=pltpu.SMEM),
            pl.BlockSpec((tm, d), lambda t: (t, 0)),
            pl.BlockSpec((1, 1, d), mod_idx),
            pl.BlockSpec(memory_space=pl.ANY),
        ],
        out_specs=pl.BlockSpec((tm, d), lambda t: (t, 0)),
        out_shape=jax.ShapeDtypeStruct((n, d), F32),
        scratch_shapes=[pltpu.VMEM((2, tm, d), F32), pltpu.SemaphoreType.DMA((2,))],
        compiler_params=_cparams(("arbitrary",)),
        name="combine",
    )(pos3, pos3, x1, mod_g2, ys)


def _block_diag(w):
    n, d, e = w.shape
    eye = jnp.eye(n, dtype=w.dtype)
    return (eye[:, None, :, None] * w[:, :, None, :]).reshape(n * d, n * e)


def _rope_tables(seq):
    rows = seq // GRID_W
    row = jnp.repeat(jnp.arange(rows, dtype=F32), GRID_W)
    col = jnp.tile(jnp.arange(GRID_W, dtype=F32), rows)
    n_freq = HEAD_DIM // 4
    inv = ROPE_BASE ** (-jnp.arange(n_freq, dtype=F32) / n_freq)
    ang = jnp.concatenate([row[:, None] * inv, col[:, None] * inv], axis=-1)
    cos, sin = jnp.cos(ang), jnp.sin(ang)
    reps = LANES // (HEAD_DIM // 2)
    sign = jnp.tile(jnp.concatenate([-jnp.ones((HEAD_DIM // 2,), F32), jnp.ones((HEAD_DIM // 2,), F32)]),
                    LANES // HEAD_DIM)
    return jnp.tile(cos, (1, reps)), jnp.tile(sin, (1, reps)) * sign


def _max_tiles(n_tokens, tm):
    return -(-(n_tokens + N_BUCKETS * (tm - 1)) // tm)


def _tile_plan(counts, n_tiles, tm):
    padded = ((counts + tm - 1) // tm) * tm
    ends = jnp.cumsum(padded)
    base = ends - padded
    tile_start = jnp.arange(n_tiles, dtype=I32) * tm
    tile_bucket = jnp.sum((tile_start[:, None] >= ends[None, :]).astype(I32), axis=1)
    active = (tile_bucket < N_BUCKETS).astype(I32)
    last_used = jnp.maximum(jnp.sum(active) - 1, 0)
    tile_bucket = jnp.where(active == 1, tile_bucket, tile_bucket[last_used])
    tile_bucket = jnp.minimum(tile_bucket, N_BUCKETS - 1)
    grp, pair = tile_bucket // N_PAIRS, tile_bucket % N_PAIRS
    ea = grp * EXPERTS_PER_GROUP + jnp.asarray(PAIR_SLOT_A, I32)[pair]
    eb = grp * EXPERTS_PER_GROUP + jnp.asarray(PAIR_SLOT_B, I32)[pair]
    return base, ea, eb, active


def kernel(x, c, ctx, c_ctx, w_mod, b_mod, norm1_g, norm2_g, w_in, b_merge, conv_w, conv_b, lru_wa, lru_ba,
           lru_wx, lru_bx, lru_lambda, q_norm_g, k_norm_g, sink, w_branch, w_out, router_w, router_b,
           expert_w1, expert_w3, expert_w2):
    batch, seq, d = x.shape
    n_ctx_per = ctx.shape[1]
    n_layers = w_mod.shape[0]
    n_ctx = batch * n_ctx_per
    n_lat = batch * seq
    tm = min(256, n_ctx_per)
    tm_mxu = min(512, n_ctx, seq)
    tq = 128
    tk = min(512, seq)
    tm_moe = 256
    tn_route = min(1024, n_ctx_per)

    n_mod_rows = -(-(batch + 1) // 8) * 8
    cvec = jnp.zeros((n_mod_rows, d), F32).at[0].set(c_ctx).at[1:batch + 1].set(c)
    mod = _modulation(cvec, w_mod, b_mod)[:, :batch + 1]
    mod = mod.reshape(n_layers, batch + 1, 6, 1, d)

    cos_t, sin_t = _rope_tables(seq)
    seg = jnp.kron(jnp.eye(LANES // HEAD_DIM, dtype=F32),
                   jnp.full((HEAD_DIM, HEAD_DIM), 1.0 / HEAD_DIM, F32)).astype(BF16)
    eye = jnp.eye(LANES, dtype=BF16)
    tri = (jnp.arange(tn_route)[:, None] <= jnp.arange(tn_route)[None, :]).astype(BF16)
    rw_t = router_w.T
    rw_pad = jnp.zeros((d, LANES), F32).at[:, :N_EXPERTS].set(router_w)
    rw_hi = rw_pad.astype(BF16)
    rw_lo = (rw_pad - rw_hi.astype(F32)).astype(BF16)
    rw3 = rw_t.reshape(N_EXPERTS, 1, d)
    rb = router_b.reshape(N_EXPERTS, 1)

    x_all = jnp.concatenate([ctx.reshape(n_ctx, d), x.reshape(n_lat, d)], axis=0)
    n_moe_tiles = _max_tiles(n_ctx + n_lat, tm_moe)
    sorted_rows = jnp.zeros((n_moe_tiles * tm_moe, d), F32)
    routed = None
    for l in range(n_layers):
        last = l == n_layers - 1
        sh1, sc1, g1, sh2, sc2, g2 = (mod[l, :, k] for k in range(6))
        qgain = jnp.tile(q_norm_g[l], (1, LANES // HEAD_DIM))
        kgain = jnp.tile(k_norm_g[l], (1, LANES // HEAD_DIM))
        proj_args = (sh1, sc1, norm1_g[l][None], w_in[l].astype(BF16), b_merge[l][None], qgain, kgain,
                     cos_t, sin_t, seg, eye)
        if routed is None:
            proj = _input_projection(x_all, *proj_args, n_ctx=n_ctx, seq=seq, tm=tm_mxu)
        else:
            proj, x_all = _input_projection(x_all, *proj_args, n_ctx=n_ctx, seq=seq, tm=tm_mxu, combine=routed)
        xr, gr, qg, kg, vg, qw, kw, vw, mg = proj
        hfb = _rglru(xr, conv_w[l], conv_b[l][None],
                     jax.vmap(_block_diag)(lru_wa[l]).astype(BF16), jax.vmap(_block_diag)(lru_wx[l]).astype(BF16),
                     lru_ba[l][:, None], lru_bx[l][:, None], lru_lambda[l][:, None],
                     batch=batch, n_ctx_per=n_ctx_per, seq=seq)
        attn_args = dict(batch=batch, n_ctx_per=n_ctx_per, seq=seq, with_ctx_queries=not last, tq=tq, tk=tk)
        yg = _attention(sink[l], qg, kg, vg, eye, band=False, use_sink=False, n_sub=min(2, n_ctx_per // tq),
                        **attn_args)
        yw = _attention(sink[l], qw, kw, vw, eye, band=True, use_sink=True, n_sub=min(2, n_ctx_per // tq),
                        **attn_args)
        x1, h2, logits_t = _merge(x_all, hfb, gr, yg, yw, mg, w_branch[l].astype(BF16), w_out[l].astype(BF16),
                                  g1, norm2_g[l][None], sh2, sc2, rw_hi, rw_lo,
                                  n_ctx=n_ctx, seq=seq, with_ctx=not last, tm=tm_mxu)
        n_tok = x1.shape[0]
        bucket, rank, counts = _routing(logits_t, rb, tri, tn=tn_route)
        base, ea, eb, active = _tile_plan(counts[:N_BUCKETS, 0], n_moe_tiles, tm_moe)
        base_col = jnp.zeros((BUCKET_ROWS, 1), I32).at[:N_BUCKETS, 0].set(base.astype(I32))
        pos = _positions(bucket, rank, base_col, tn=min(4096, n_ctx))
        pos3 = pos.reshape(n_tok // tm, 1, tm)
        xs = _dispatch(pos3, h2, sorted_rows, tm=tm)
        ys = _moe(ea, eb, active, xs, expert_w1[l].astype(BF16), expert_w3[l].astype(BF16),
                  expert_w2[l].astype(BF16), rw3, tm=tm_moe)
        sorted_rows = xs
        if last:
            x_all = _combine(pos3, x1, g2, ys, n_ctx=n_ctx, seq=seq, with_ctx=False, tm=tm)
        else:
            x_all, routed = x1, (pos.reshape(n_tok // tm_mxu, 1, tm_mxu), g2, ys)
    return x_all.reshape(batch, seq, d)
```

```python
import functools

import jax
import jax.numpy as jnp
from jax import lax
from jax.experimental import pallas as pl
from jax.experimental.pallas import tpu as pltpu

F32 = jnp.float32
BF16 = jnp.bfloat16
I32 = jnp.int32

HEAD_DIM = 64
N_Q_HEADS = 8
N_KV_HEADS = 2
GQA_GROUP = N_Q_HEADS // N_KV_HEADS
ATTN_WIDTH = N_Q_HEADS * HEAD_DIM
KV_WIDTH = N_KV_HEADS * HEAD_DIM
RNN_WIDTH = 512
RNN_BLOCKS = 8
CONV_WIDTH = 4
LRU_C = 8.0
WINDOW = 128
GRID_W = 64
ROPE_BASE = 10000.0
N_BRANCHES = 3
N_EXPERTS = 16
N_GROUPS = 4
EXPERTS_PER_GROUP = 4
EPS = 1e-6
NEG_INF = -1e30
V_ROWS = HEAD_DIM + 16
LOG2E = 1.4426950408889634

PAIR_SLOT_A = (0, 0, 0, 1, 1, 3)
PAIR_SLOT_B = (1, 2, 3, 3, 2, 2)
N_PAIRS = 6
N_BUCKETS = N_GROUPS * N_PAIRS
BUCKET_ROWS = 32

LANES = 128
SUBLANES = 8
VMEM_LIMIT = 56 * 1024 * 1024

HIGHEST = lax.Precision.HIGHEST


def _cparams(sem):
    return pltpu.CompilerParams(dimension_semantics=sem, vmem_limit_bytes=VMEM_LIMIT)


def _sigmoid(x):
    return 0.5 * jnp.tanh(0.5 * x) + 0.5


def _rms_mod(x, g, scale, shift):
    ms = jnp.mean(x * x, axis=-1, keepdims=True)
    return (x * lax.rsqrt(ms + EPS) * g) * (1.0 + scale) + shift


def _mod_kernel(a_ref, w_ref, b_ref, o_ref):
    a = a_ref[...]
    a = a * _sigmoid(a)
    o_ref[0] = jnp.dot(a, w_ref[0], precision=HIGHEST, preferred_element_type=F32) + b_ref[0]


def _modulation(cvec, w_mod, b_mod):
    rows, d = cvec.shape
    n_layers, _, n_out = w_mod.shape
    tn = 1536
    return pl.pallas_call(
        _mod_kernel,
        grid=(n_layers, n_out // tn),
        in_specs=[
            pl.BlockSpec((rows, d), lambda l, j: (0, 0)),
            pl.BlockSpec((1, d, tn), lambda l, j: (l, 0, j)),
            pl.BlockSpec((1, 1, tn), lambda l, j: (l, 0, j)),
        ],
        out_specs=pl.BlockSpec((1, rows, tn), lambda l, j: (l, 0, j)),
        out_shape=jax.ShapeDtypeStruct((n_layers, rows, n_out), F32),
        compiler_params=_cparams(("arbitrary", "arbitrary")),
        name="modulation",
    )(cvec, w_mod, b_mod.reshape(n_layers, 1, n_out))


def _head_norm_rope(p, gain, cos_t, sin_t, seg, lo32):
    ss = p * p
    hi = ss.astype(BF16)
    lo = (ss - hi.astype(F32)).astype(BF16)
    mean = jnp.dot(hi, seg, preferred_element_type=F32) + jnp.dot(lo, seg, preferred_element_type=F32)
    n = p * lax.rsqrt(mean + EPS) * gain
    partner = jnp.where(lo32, pltpu.roll(n, 96, 1), pltpu.roll(n, 32, 1))
    return n * cos_t + partner * sin_t


def _inproj_kernel(xc_ref, xl_ref, *refs, n_ctx_tiles):
    *refs, x_o = refs
    x = jnp.where(pl.program_id(0) >= n_ctx_tiles, xl_ref[...], xc_ref[...])
    x_o[...] = x
    _inproj_body(x, *refs, n_ctx_tiles=n_ctx_tiles)


def _inproj_combine_kernel(pos_ref, nxt_ref, g2_ref, ys_ref, x1_ref, *refs, n_ctx_tiles):
    *refs, x_o, buf, sem = refs
    t = pl.program_id(0)
    tm = x1_ref.shape[0]
    slot = t % 2

    @pl.when(t == 0)
    def _():
        _row_gather(pos_ref, ys_ref, buf.at[0], sem.at[0], tm)

    _row_copy_wait(ys_ref, buf.at[slot], sem.at[slot], tm)
    x = x1_ref[...] + g2_ref[0] * buf[slot]
    x_o[...] = x

    def gather_part(n, n_parts):
        rows = tm // n_parts
        for r in range(n * rows, (n + 1) * rows):
            pltpu.make_async_copy(ys_ref.at[pl.ds(nxt_ref[0, 0, r], 1), :], buf.at[1 - slot, pl.ds(r, 1), :],
                                  sem.at[1 - slot]).start(priority=r % 2)

    _inproj_body(x, *refs, n_ctx_tiles=n_ctx_tiles, between_groups=gather_part)

    @pl.when(t == pl.num_programs(0) - 1)
    def _():
        _row_copy_wait(ys_ref, buf.at[1 - slot], sem.at[1 - slot], tm)


def _inproj_body(x, sh_ref, sc_ref, g_ref, w_ref, bm_ref, qgain_ref, kgain_ref, cos_ref, sin_ref, seg_ref,
                 eye_ref, xr_o, gr_o, qg_o, kg_o, vg_o, qw_o, kw_o, vw_o, mg_o, *, n_ctx_tiles, between_groups=None):
    i = pl.program_id(0)
    is_lat = i >= n_ctx_tiles
    h = _rms_mod(x, g_ref[...], sc_ref[0], sh_ref[0])
    hb = h.astype(BF16)
    tm = hb.shape[0]

    def proj(c0, width):
        return jnp.dot(hb, w_ref[:, c0:c0 + width], preferred_element_type=F32)

    lane = lax.broadcasted_iota(I32, (tm, LANES), 1)
    lo32 = (lane & (HEAD_DIM - 1)) < (HEAD_DIM // 2)
    lo64 = lane < HEAD_DIM
    cos_t = jnp.where(is_lat, cos_ref[...], 1.0)
    sin_t = jnp.where(is_lat, sin_ref[...], 0.0)
    seg = seg_ref[...]
    zero = jnp.zeros((tm, LANES), F32)
    d = x.shape[1]

    def rnn_epilogue(p):
        xr_o[...] = p[:, :RNN_WIDTH].astype(BF16)
        gr_o[...] = jax.nn.gelu(p[:, RNN_WIDTH:]).astype(BF16)

    def q_epilogue(a, q_o, p):
        qgain = qgain_ref[a:a + 1, :]
        for c in range(ATTN_WIDTH // LANES):
            y = _head_norm_rope(p[:, c * LANES:(c + 1) * LANES], qgain, cos_t, sin_t, seg, lo32)
            y = y * (HEAD_DIM ** -0.5 * LOG2E)
            yr = pltpu.roll(y, HEAD_DIM, 1)
            if c < 2:
                out_a, out_b = jnp.where(lo64, y, zero), jnp.where(lo64, yr, zero)
            else:
                out_a, out_b = jnp.where(lo64, zero, yr), jnp.where(lo64, zero, y)
            q_o[2 * c] = out_a.astype(BF16)
            q_o[2 * c + 1] = out_b.astype(BF16)

    def kv_epilogue(a, k_o, v_o, p):
        k_o[...] = _head_norm_rope(p[:, :KV_WIDTH], kgain_ref[a:a + 1, :], cos_t, sin_t, seg, lo32).astype(BF16)
        pv = p[:, KV_WIDTH:]
        heads = (jnp.where(lo64, pv, 1.0), jnp.where(lo64, pltpu.roll(pv, HEAD_DIM, 1), 1.0))
        for hd, vh in enumerate(heads):
            vh = vh.astype(BF16)
            if a == 0:
                vh = lax.dot_general(eye_ref[...], vh, (((1,), (1,)), ((), ())),
                                     preferred_element_type=F32).astype(BF16)[:V_ROWS]
            v_o[hd] = vh

    def gate_epilogue(b, p):
        mg_o[:, b * d:(b + 1) * d] = _sigmoid(p + bm_ref[:, b * d:(b + 1) * d]).astype(BF16)

    groups = [(0, 2 * RNN_WIDTH, rnn_epilogue)]
    base = 2 * RNN_WIDTH
    for a, (q_o, k_o, v_o) in enumerate(((qg_o, kg_o, vg_o), (qw_o, kw_o, vw_o))):
        groups.append((base, ATTN_WIDTH, functools.partial(q_epilogue, a, q_o)))
        groups.append((base + ATTN_WIDTH, 2 * KV_WIDTH, functools.partial(kv_epilogue, a, k_o, v_o)))
        base += ATTN_WIDTH + 2 * KV_WIDTH
    for b in range(N_BRANCHES):
        groups.append((base + b * d, d, functools.partial(gate_epilogue, b)))

    p_next = proj(groups[0][0], groups[0][1])
    for n, (_, _, epilogue) in enumerate(groups):
        p_cur = p_next
        if n + 1 < len(groups):
            p_next = proj(groups[n + 1][0], groups[n + 1][1])
        if between_groups is not None:
            between_groups(n, len(groups))
        epilogue(p_cur)


def _input_projection(x_src, mod_sh, mod_sc, g, w_in, b_merge, qgain, kgain, cos_t, sin_t, seg, eye, *, n_ctx, seq, tm,
                      combine=None):
    if combine is None:
        x_ctx, x_lat = x_src
        ntok, d = x_ctx.shape[0] + x_lat.shape[0], x_ctx.shape[1]
    else:
        ntok, d = x_src.shape
    n_in = w_in.shape[1]
    nct = n_ctx // tm
    tps = seq // tm

    def mod_idx(i):
        return (jnp.where(i >= nct, 1 + (i - nct) // tps, 0), 0, 0)

    def rope_idx(i):
        return (jnp.where(i >= nct, (i - nct) % tps, 0), 0)

    row = lambda i: (i, 0)
    fixed = lambda i: (0, 0)
    out_shape = (
        jax.ShapeDtypeStruct((ntok, RNN_WIDTH), BF16),
        jax.ShapeDtypeStruct((ntok, RNN_WIDTH), BF16),
        jax.ShapeDtypeStruct((N_Q_HEADS, ntok, LANES), BF16),
        jax.ShapeDtypeStruct((ntok, KV_WIDTH), BF16),
        jax.ShapeDtypeStruct((N_KV_HEADS, V_ROWS, ntok), BF16),
        jax.ShapeDtypeStruct((N_Q_HEADS, ntok, LANES), BF16),
        jax.ShapeDtypeStruct((ntok, KV_WIDTH), BF16),
        jax.ShapeDtypeStruct((N_KV_HEADS, ntok, LANES), BF16),
        jax.ShapeDtypeStruct((ntok, N_BRANCHES * d), BF16),
    )
    q_spec = pl.BlockSpec((N_Q_HEADS, tm, LANES), lambda i: (0, i, 0))
    out_specs = (
        pl.BlockSpec((tm, RNN_WIDTH), row), pl.BlockSpec((tm, RNN_WIDTH), row),
        q_spec, pl.BlockSpec((tm, KV_WIDTH), row), pl.BlockSpec((N_KV_HEADS, V_ROWS, tm), lambda i: (0, 0, i)),
        q_spec, pl.BlockSpec((tm, KV_WIDTH), row), pl.BlockSpec((N_KV_HEADS, tm, LANES), lambda i: (0, i, 0)),
        pl.BlockSpec((tm, N_BRANCHES * d), row),
    )
    in_specs = [
        pl.BlockSpec((tm, d), row),
        pl.BlockSpec((1, 1, d), mod_idx),
        pl.BlockSpec((1, 1, d), mod_idx),
        pl.BlockSpec((1, d), fixed),
        pl.BlockSpec((d, n_in), fixed, pipeline_mode=pl.Buffered(1)),
        pl.BlockSpec((1, N_BRANCHES * d), fixed),
        pl.BlockSpec((2, LANES), fixed),
        pl.BlockSpec((2, LANES), fixed),
        pl.BlockSpec((tm, LANES), rope_idx),
        pl.BlockSpec((tm, LANES), rope_idx),
        pl.BlockSpec((LANES, LANES), fixed),
        pl.BlockSpec((LANES, LANES), fixed),
    ]
    common = (mod_sh, mod_sc, g, w_in, b_merge, qgain, kgain, cos_t, sin_t, seg, eye)
    out_specs = out_specs + (pl.BlockSpec((tm, d), row),)
    out_shape = out_shape + (jax.ShapeDtypeStruct((ntok, d), F32),)
    if combine is None:
        outs = pl.pallas_call(
            functools.partial(_inproj_kernel, n_ctx_tiles=nct),
            grid=(ntok // tm,),
            in_specs=[pl.BlockSpec((tm, d), lambda i: (jnp.minimum(i, nct - 1), 0)),
                      pl.BlockSpec((tm, d), lambda i: (jnp.maximum(i - nct, 0), 0))] + in_specs[1:],
            out_specs=out_specs,
            out_shape=out_shape,
            compiler_params=_cparams(("arbitrary",)),
            name="input_projection",
        )(x_ctx, x_lat, *common)
        return outs[:-1], outs[-1]
    pos3, mod_g2, ys = combine
    n_steps = ntok // tm
    smem_blk = lambda f: pl.BlockSpec((1, 1, tm), f, memory_space=pltpu.SMEM)
    outs = pl.pallas_call(
        functools.partial(_inproj_combine_kernel, n_ctx_tiles=nct),
        grid=(n_steps,),
        in_specs=[
            smem_blk(lambda i: (i, 0, 0)),
            smem_blk(lambda i: (jnp.minimum(i + 1, n_steps - 1), 0, 0)),
            pl.BlockSpec((1, 1, d), mod_idx),
            pl.BlockSpec(memory_space=pl.ANY),
        ] + in_specs,
        out_specs=out_specs,
        out_shape=out_shape,
        scratch_shapes=[pltpu.VMEM((2, tm, d), F32), pltpu.SemaphoreType.DMA((2,))],
        compiler_params=_cparams(("arbitrary",)),
        name="input_projection_combine",
    )(pos3, pos3, mod_g2, ys, x_src, *common)
    return outs[:-1], outs[-1]


def _rglru_kernel(x_ref, prev_ref, next_ref, cw_ref, cb_ref, wa_ref, wx_ref, ba_ref, bx_ref, lam_ref,
                  o_ref, a_scr, b_scr, h_scr, y_scr, pe_scr, he_scr, c_scr, *, n_chunks):
    d = pl.program_id(0)
    j = pl.program_id(2)
    jj = jnp.where(d == 0, j, jnp.where(j == 0, 0, n_chunks - j))
    prev_valid = jj >= 2
    next_valid = (jj >= 1) & (jj <= n_chunks - 2)

    x = x_ref[...].astype(F32)
    tc = x.shape[0]
    hp = prev_ref[...].astype(F32)
    hn = next_ref[...].astype(F32)
    halo = prev_ref.shape[0]
    pm1 = jnp.where(prev_valid, hp[halo - 1:halo, :], 0.0)
    pm2 = jnp.where(prev_valid, hp[halo - 2:halo - 1, :], 0.0)
    nx0 = jnp.where(next_valid, hn[0:1, :], 0.0)
    row = lax.broadcasted_iota(I32, x.shape, 0)
    xm1 = jnp.where(row == 0, pm1, pltpu.roll(x, 1, 0))
    xm2 = jnp.where(row == 0, pm2, jnp.where(row == 1, pm1, pltpu.roll(x, 2, 0)))
    xp1 = jnp.where(row == tc - 1, nx0, pltpu.roll(x, tc - 1, 0))
    xc = (xm2 * cw_ref[0:1, :] + xm1 * cw_ref[1:2, :] + x * cw_ref[2:3, :] + xp1 * cw_ref[3:4, :]) + cb_ref[...]

    xb = xc.astype(BF16)
    r = _sigmoid(jnp.dot(xb, wa_ref[0], preferred_element_type=F32) + ba_ref[0])
    g = _sigmoid(jnp.dot(xb, wx_ref[0], preferred_element_type=F32) + bx_ref[0])
    neg_lam = -lam_ref[0]
    softplus = jnp.maximum(neg_lam, 0.0) + jnp.log(1.0 + jnp.exp(-jnp.abs(neg_lam)))
    a = jnp.exp(-LRU_C * r * softplus)
    bb = jnp.sqrt(1.0 - a * a) * (g * xc)
    n_lt = a.shape[1] // LANES
    for lt in range(n_lt):
        a_scr[lt * tc:(lt + 1) * tc, :] = a[:, lt * LANES:(lt + 1) * LANES]
        b_scr[lt * tc:(lt + 1) * tc, :] = bb[:, lt * LANES:(lt + 1) * LANES]

    @pl.when(j == 0)
    def _():
        h_scr[...] = jnp.zeros_like(h_scr)

    ng = tc // SUBLANES

    def scan(reverse):
        order = range(SUBLANES - 1, -1, -1) if reverse else range(SUBLANES)
        prod, loc = {}, {}
        for lt in range(n_lt):
            p = hl = None
            for k in order:
                ak = a_scr[pl.ds(lt * tc + k, ng, stride=SUBLANES), :]
                bk = b_scr[pl.ds(lt * tc + k, ng, stride=SUBLANES), :]
                p, hl = (ak, bk) if p is None else (ak * p, ak * hl + bk)
                prod[lt, k], loc[lt, k] = p, hl
            pe_scr[lt * ng:(lt + 1) * ng, :] = p
            he_scr[lt * ng:(lt + 1) * ng, :] = hl
        c = [h_scr[lt:lt + 1, :] for lt in range(n_lt)]
        for g in (range(ng - 1, -1, -1) if reverse else range(ng)):
            for lt in range(n_lt):
                r = lt * ng + g
                c_scr[r:r + 1, :] = c[lt]
                c[lt] = pe_scr[r:r + 1, :] * c[lt] + he_scr[r:r + 1, :]
        for lt in range(n_lt):
            h_scr[lt:lt + 1, :] = c[lt]
            carry_in = c_scr[lt * ng:(lt + 1) * ng, :]
            for k in order:
                y_scr[pl.ds(lt * tc + k, ng, stride=SUBLANES), :] = loc[lt, k] + prod[lt, k] * carry_in

    pl.when(d == 0)(lambda: scan(False))
    pl.when(d != 0)(lambda: scan(True))
    for lt in range(n_lt):
        o_ref[0, :, lt * LANES:(lt + 1) * LANES] = y_scr[lt * tc:(lt + 1) * tc, :].astype(BF16)


def _rglru(xr, conv_w, conv_b, wa, wx, ba, bx, lam, *, batch, n_ctx_per, seq):
    ntok, width = xr.shape
    tc = n_ctx_per
    halo = 16
    n_chunks = 1 + seq // tc
    n_lt = width // LANES
    lat0 = batch

    def seq_chunk(d, j):
        return jnp.where(d == 0, j, jnp.where(j == 0, 0, n_chunks - j))

    def chunk_blk(d, b, j):
        jj = seq_chunk(d, j)
        return jnp.where(jj == 0, b, lat0 + b * (seq // tc) + jj - 1)

    def x_idx(d, b, j):
        return (chunk_blk(d, b, j), 0)

    def prev_idx(d, b, j):
        return (jnp.maximum(chunk_blk(d, b, j) * (tc // halo) - 1, 0), 0)

    def next_idx(d, b, j):
        return (jnp.minimum((chunk_blk(d, b, j) + 1) * (tc // halo), ntok // halo - 1), 0)

    per_dir = lambda d, b, j: (d, 0, 0)
    fixed = lambda d, b, j: (0, 0)
    return pl.pallas_call(
        functools.partial(_rglru_kernel, n_chunks=n_chunks),
        grid=(2, batch, n_chunks),
        in_specs=[
            pl.BlockSpec((tc, width), x_idx),
            pl.BlockSpec((halo, width), prev_idx),
            pl.BlockSpec((halo, width), next_idx),
            pl.BlockSpec((CONV_WIDTH, width), fixed),
            pl.BlockSpec((1, width), fixed),
            pl.BlockSpec((1, width, width), per_dir),
            pl.BlockSpec((1, width, width), per_dir),
            pl.BlockSpec((1, 1, width), per_dir),
            pl.BlockSpec((1, 1, width), per_dir),
            pl.BlockSpec((1, 1, width), per_dir),
        ],
        out_specs=pl.BlockSpec((1, tc, width), lambda d, b, j: (d, chunk_blk(d, b, j), 0)),
        out_shape=jax.ShapeDtypeStruct((2, ntok, width), BF16),
        scratch_shapes=[
            pltpu.VMEM((n_lt * tc, LANES), F32), pltpu.VMEM((n_lt * tc, LANES), F32),
            pltpu.VMEM((n_lt, LANES), F32), pltpu.VMEM((n_lt * tc, LANES), F32),
            pltpu.VMEM((n_lt * tc // SUBLANES, LANES), F32), pltpu.VMEM((n_lt * tc // SUBLANES, LANES), F32),
            pltpu.VMEM((n_lt * tc // SUBLANES, LANES), F32),
        ],
        compiler_params=_cparams(("arbitrary", "arbitrary", "arbitrary")),
        name="rglru",
    )(xr, xr, xr, conv_w, conv_b, wa, wx, ba, bx, lam)


LOOKAHEAD = 2


def _col_max(s):
    parts = [s[r:r + SUBLANES] for r in range(0, s.shape[0], SUBLANES)]
    while len(parts) > 1:
        nxt = [jnp.maximum(parts[k], parts[k + 1]) for k in range(0, len(parts) - 1, 2)]
        if len(parts) % 2:
            nxt.append(parts[-1])
        parts = nxt
    return jnp.max(parts[0], axis=0, keepdims=True)


def _attn_kernel(sink_ref, q_ref, kc_ref, vc_ref, kl_ref, vl_ref, eye_ref, o_ref, *,
                 band, use_sink, tile_off, n_ctx_tiles, tiles_per_seq, tk, tq, pv_delay):
    i = pl.program_id(0) + tile_off
    is_lat = i >= n_ctx_tiles
    n_sub = q_ref.shape[1] // tq
    qi0 = jnp.where(is_lat, (i - n_ctx_tiles) % tiles_per_seq, 0) * n_sub
    nq = tiles_per_seq * n_sub
    nq_rows = GQA_GROUP * tq
    seq = kl_ref.shape[0]
    contract_last = (((1,), (1,)), ((), ()))
    lo64 = lax.broadcasted_iota(I32, (tq, LANES), 1) < HEAD_DIM
    chan = lax.broadcasted_iota(I32, (V_ROWS, nq_rows), 0)

    def transposed(v):
        return lax.dot_general(eye_ref[:V_ROWS, :], v, contract_last, preferred_element_type=F32).astype(BF16)

    def scores(q, k, mask):
        s = lax.dot_general(k, q, contract_last, preferred_element_type=F32)
        return s if mask is None else jnp.where(mask, s, NEG_INF)

    def run(latent):
        qs, carries, blocks = [], [], []
        for u in range(n_sub):
            if latent and band:
                qi = qi0 + u
                starts = (jnp.maximum(qi - 1, 0) * tq, qi * tq, jnp.minimum(qi + 1, nq - 1) * tq)
                starts = [pl.multiple_of(s0, tq) for s0 in starts]
                kpos = lax.broadcasted_iota(I32, (3 * tq, nq_rows), 0)
                qpos = lax.broadcasted_iota(I32, (3 * tq, nq_rows), 1) & (tq - 1)
                rel = kpos - tq - qpos
                kmin = jnp.where(qi > 0, 0, tq)
                kmax = jnp.where(qi < nq - 1, 3 * tq, 2 * tq)
                band_ok = (rel >= -WINDOW) & (rel <= WINDOW) & (kpos >= kmin) & (kpos < kmax)
            for h in range(N_KV_HEADS):
                qs.append(q_ref[GQA_GROUP * h:GQA_GROUP * (h + 1), u * tq:(u + 1) * tq, :].reshape(nq_rows, LANES))
                if use_sink:
                    m0 = jnp.concatenate(
                        [jnp.full((1, tq), sink_ref[GQA_GROUP * h + g] * LOG2E, F32) for g in range(GQA_GROUP)],
                        axis=1)
                    acc0 = jnp.where(chan >= HEAD_DIM, 1.0, 0.0)
                else:
                    m0 = jnp.full((1, nq_rows), NEG_INF, F32)
                    acc0 = jnp.zeros((V_ROWS, nq_rows), F32)
                carries.append((m0, acc0))
                blk = [(lambda: kc_ref[...],
                        (lambda h=h: transposed(vc_ref[h])) if band else (lambda h=h: vc_ref[h]), None)]
                if latent and band:
                    blk.append((
                        lambda starts=starts: jnp.concatenate([kl_ref[pl.ds(s0, tq), :] for s0 in starts], axis=0),
                        lambda starts=starts, h=h: transposed(
                            jnp.concatenate([vl_ref[h, pl.ds(s0, tq), :] for s0 in starts], axis=0)),
                        band_ok))
                elif latent:
                    for c in range(seq // tk):
                        blk.append((lambda c=c: kl_ref[c * tk:(c + 1) * tk, :],
                                    lambda c=c, h=h: vl_ref[h, :, c * tk:(c + 1) * tk], None))
                blocks.append(blk)

        n_streams = len(qs)
        items = [(st, c) for c in range(len(blocks[0])) for st in range(n_streams)]

        def issue(item):
            st, c = item
            return scores(qs[st], blocks[st][c][0](), blocks[st][c][2])

        depth = LOOKAHEAD if band else LOOKAHEAD * n_sub
        ahead = [issue(it) for it in items[:depth]]
        pending = []

        def value_matmul():
            st, alpha, p, vt = pending.pop(0)
            m, acc = carries[st]
            carries[st] = (m, alpha * acc + jnp.dot(vt, p, preferred_element_type=F32))

        for idx, (st, c) in enumerate(items):
            s_cur = ahead.pop(0)
            if idx + depth < len(items):
                ahead.append(issue(items[idx + depth]))
            if pv_delay and len(pending) >= pv_delay:
                value_matmul()
            m, acc = carries[st]
            m_new = jnp.maximum(m, _col_max(s_cur))
            pending.append((st, jnp.exp2(m - m_new), jnp.exp2(s_cur - m_new).astype(BF16), blocks[st][c][1]()))
            carries[st] = (m_new, acc)
            if not pv_delay:
                value_matmul()
        while pending:
            value_matmul()

        for st in range(n_streams):
            u, h = divmod(st, N_KV_HEADS)
            _, acc = carries[st]
            o = acc * (1.0 / acc[HEAD_DIM:HEAD_DIM + 1, :])
            o = jnp.concatenate([o, jnp.zeros((LANES - V_ROWS, nq_rows), F32)], axis=0)
            o = jnp.transpose(o)
            og = [o[g * tq:(g + 1) * tq] for g in range(GQA_GROUP)]
            for c in range(GQA_GROUP // 2):
                chunk = jnp.where(lo64, og[2 * c], pltpu.roll(og[2 * c + 1], HEAD_DIM, 1))
                col = (h * (GQA_GROUP // 2) + c) * LANES
                o_ref[u * tq:(u + 1) * tq, col:col + LANES] = chunk.astype(BF16)

    if tile_off == 0:
        pl.when(is_lat)(lambda: run(True))
        pl.when(jnp.logical_not(is_lat))(lambda: run(False))
    else:
        run(True)


def _attention(sink, q, k, v, eye, *, band, use_sink, batch, n_ctx_per, seq, with_ctx_queries, tq, tk, n_sub):
    ntok = k.shape[0]
    n_ctx = batch * n_ctx_per
    assert n_ctx % seq == 0, "context rows must cover whole latent-sequence blocks"
    assert tq == WINDOW
    rows = n_sub * tq
    assert n_ctx_per % rows == 0 and seq % rows == 0
    nct = n_ctx // rows
    tps = seq // rows
    tile_off = 0 if with_ctx_queries else nct
    n_tiles = ntok // rows - tile_off

    def batch_of(t):
        i = t + tile_off
        return jnp.where(i >= nct, (i - nct) // tps, (i * rows) // n_ctx_per)

    ctx_idx = lambda t, *_: (batch_of(t), 0)
    lat_idx = lambda t, *_: (n_ctx // seq + batch_of(t), 0)
    if band:
        vc_spec = pl.BlockSpec((N_KV_HEADS, n_ctx_per, LANES), lambda t, *_: (0, batch_of(t), 0))
        vl_spec = pl.BlockSpec((N_KV_HEADS, seq, LANES), lambda t, *_: (0, n_ctx // seq + batch_of(t), 0))
    else:
        vc_spec = pl.BlockSpec((N_KV_HEADS, V_ROWS, n_ctx_per), lambda t, *_: (0, 0, batch_of(t)))
        vl_spec = pl.BlockSpec((N_KV_HEADS, V_ROWS, seq), lambda t, *_: (0, 0, n_ctx // seq + batch_of(t)))
    grid_spec = pltpu.PrefetchScalarGridSpec(
        num_scalar_prefetch=1,
        grid=(n_tiles,),
        in_specs=[
            pl.BlockSpec((N_Q_HEADS, rows, LANES), lambda t, *_: (0, t + tile_off, 0)),
            pl.BlockSpec((n_ctx_per, KV_WIDTH), ctx_idx),
            vc_spec,
            pl.BlockSpec((seq, KV_WIDTH), lat_idx),
            vl_spec,
            pl.BlockSpec((LANES, LANES), lambda t, *_: (0, 0)),
        ],
        out_specs=pl.BlockSpec((rows, ATTN_WIDTH), lambda t, *_: (t, 0)),
    )
    return pl.pallas_call(
        functools.partial(_attn_kernel, band=band, use_sink=use_sink, tile_off=tile_off,
                          n_ctx_tiles=nct, tiles_per_seq=tps, tk=tk, tq=tq, pv_delay=2),
        grid_spec=grid_spec,
        out_shape=jax.ShapeDtypeStruct((n_tiles * rows, ATTN_WIDTH), BF16),
        compiler_params=_cparams(("arbitrary",)),
        name="window_attention" if band else "global_attention",
    )(sink, q, k, v, k, v, eye)


def _merge_kernel(x_ref, h_ref, gr_ref, yg_ref, yw_ref, mg_ref, wb_ref, wo_ref, g1_ref, n2_ref, sh2_ref, sc2_ref,
                  rwhl_ref, x1_o, h2_o, lg_o):
    tm, d = x_ref.shape
    n_parts = 2 if tm % 256 == 0 else 1
    part = tm // n_parts
    rows = [slice(k * part, (k + 1) * part) for k in range(n_parts)]

    def branches(r):
        y_rnn = ((h_ref[0, r, :].astype(F32) + h_ref[1, r, :].astype(F32)) * gr_ref[r, :].astype(F32)).astype(BF16)
        ys = (y_rnn, yg_ref[r, :], yw_ref[r, :])
        merged = None
        for b in range(N_BRANCHES):
            t = mg_ref[r, b * d:(b + 1) * d].astype(F32) * jnp.dot(ys[b], wb_ref[b], preferred_element_type=F32)
            merged = t if merged is None else merged + t
        return merged.astype(BF16)

    merged = [branches(r) for r in rows]
    outs = [jnp.dot(m, wo_ref[...], preferred_element_type=F32) for m in merged]
    h2s = []
    for r, out in zip(rows, outs):
        x1 = x_ref[r, :] + g1_ref[0] * out
        x1_o[r, :] = x1
        h2 = _rms_mod(x1, n2_ref[...], sc2_ref[0], sh2_ref[0])
        h2_o[r, :] = h2
        h2s.append(h2)
    for r, h2 in zip(rows, h2s):
        hi = h2.astype(BF16)
        lo = (h2 - hi.astype(F32)).astype(BF16)
        a = jnp.dot(hi, rwhl_ref[...], preferred_element_type=F32)
        lg = a[:, :LANES] + (jnp.dot(lo, rwhl_ref[:, :LANES], preferred_element_type=F32) + a[:, LANES:])
        lg_o[:, r] = jnp.transpose(lg)[:N_EXPERTS, :]


def _merge(x_all, hfb, gr, yg, yw, mg, wb, wo, mod_g1, n2, mod_sh2, mod_sc2, rw_hl, *, n_ctx, seq, with_ctx, tm):
    ntok, d = x_all.shape
    nct = n_ctx // tm
    tps = seq // tm
    off = 0 if with_ctx else nct
    n_tiles = ntok // tm - off
    n_out = n_tiles * tm

    def mod_idx(t):
        i = t + off
        return (jnp.where(i >= nct, 1 + (i - nct) // tps, 0), 0, 0)

    row_in = lambda t: (t + off, 0)
    row_out = lambda t: (t, 0)
    fixed2 = lambda t: (0, 0)
    return pl.pallas_call(
        _merge_kernel,
        grid=(n_tiles,),
        in_specs=[
            pl.BlockSpec((tm, d), row_in),
            pl.BlockSpec((2, tm, RNN_WIDTH), lambda t: (0, t + off, 0)),
            pl.BlockSpec((tm, RNN_WIDTH), row_in),
            pl.BlockSpec((tm, ATTN_WIDTH), row_out),
            pl.BlockSpec((tm, ATTN_WIDTH), row_out),
            pl.BlockSpec((tm, N_BRANCHES * d), row_in),
            pl.BlockSpec((N_BRANCHES, RNN_WIDTH, d), lambda t: (0, 0, 0)),
            pl.BlockSpec((d, d), fixed2),
            pl.BlockSpec((1, 1, d), mod_idx),
            pl.BlockSpec((1, d), fixed2),
            pl.BlockSpec((1, 1, d), mod_idx),
            pl.BlockSpec((1, 1, d), mod_idx),
            pl.BlockSpec((d, 2 * LANES), fixed2),
        ],
        out_specs=(
            pl.BlockSpec((tm, d), row_out),
            pl.BlockSpec((tm, d), row_out),
            pl.BlockSpec((N_EXPERTS, tm), lambda t: (0, t)),
        ),
        out_shape=(
            jax.ShapeDtypeStruct((n_out, d), F32),
            jax.ShapeDtypeStruct((n_out, d), F32),
            jax.ShapeDtypeStruct((N_EXPERTS, n_out), F32),
        ),
        compiler_params=_cparams(("arbitrary",)),
        name="merge",
    )(x_all, hfb, gr, yg, yw, mg, wb, wo, mod_g1, n2, mod_sh2, mod_sc2, rw_hl)


def _first_argmax(vals):
    best, idx = vals[0], jnp.zeros(vals[0].shape, I32)
    for k in range(1, len(vals)):
        take = vals[k] > best
        best = jnp.where(take, vals[k], best)
        idx = jnp.where(take, k, idx)
    return best, idx


def _routing_kernel(lg_ref, rb_ref, tri_ref, bucket_o, rank_o, count_o, carry_scr):
    t = pl.program_id(0)

    @pl.when(t == 0)
    def _():
        carry_scr[...] = jnp.zeros_like(carry_scr)

    s = _sigmoid(lg_ref[...])
    sel = s + rb_ref[...]
    rows = [sel[e:e + 1, :] for e in range(N_EXPERTS)]
    tn = s.shape[1]

    grp_scores = []
    for g in range(N_GROUPS):
        r = rows[g * EXPERTS_PER_GROUP:(g + 1) * EXPERTS_PER_GROUP]
        best = None
        for a in range(EXPERTS_PER_GROUP):
            for b in range(a + 1, EXPERTS_PER_GROUP):
                pair = r[a] + r[b]
                best = pair if best is None else jnp.maximum(best, pair)
        grp_scores.append(best)
    _, grp = _first_argmax(grp_scores)

    vals = []
    for k in range(EXPERTS_PER_GROUP):
        v = rows[k]
        for g in range(1, N_GROUPS):
            v = jnp.where(grp == g, rows[g * EXPERTS_PER_GROUP + k], v)
        vals.append(v)
    _, i1 = _first_argmax(vals)
    _, i2 = _first_argmax([jnp.where(i1 == k, -jnp.inf, vals[k]) for k in range(EXPERTS_PER_GROUP)])
    lo = jnp.minimum(i1, i2)
    hi = jnp.maximum(i1, i2)
    pair = jnp.where(lo == 0, hi - 1, jnp.where(lo == 1, 6 - hi, 5))
    bucket = grp * N_PAIRS + pair
    bucket_o[...] = bucket

    onehot = (lax.broadcasted_iota(I32, (BUCKET_ROWS, tn), 0) == bucket).astype(F32)
    incl = jnp.dot(onehot.astype(BF16), tri_ref[...], preferred_element_type=F32)
    before = carry_scr[...] + incl - 1.0
    rank_o[...] = jnp.sum(onehot * before, axis=0, keepdims=True).astype(I32)
    carry_scr[...] = carry_scr[...] + jnp.sum(onehot, axis=1, keepdims=True)
    count_o[...] = jnp.broadcast_to(carry_scr[...], count_o.shape).astype(I32)


def _routing(logits_t, router_b, tri, *, tn):
    n = logits_t.shape[1]
    return pl.pallas_call(
        _routing_kernel,
        grid=(n // tn,),
        in_specs=[
            pl.BlockSpec((N_EXPERTS, tn), lambda t: (0, t)),
            pl.BlockSpec((N_EXPERTS, 1), lambda t: (0, 0)),
            pl.BlockSpec((tn, tn), lambda t: (0, 0)),
        ],
        out_specs=(
            pl.BlockSpec((1, tn), lambda t: (0, t)),
            pl.BlockSpec((1, tn), lambda t: (0, t)),
            pl.BlockSpec((BUCKET_ROWS, LANES), lambda t: (0, 0)),
        ),
        out_shape=(
            jax.ShapeDtypeStruct((1, n), I32),
            jax.ShapeDtypeStruct((1, n), I32),
            jax.ShapeDtypeStruct((BUCKET_ROWS, LANES), I32),
        ),
        scratch_shapes=[pltpu.VMEM((BUCKET_ROWS, 1), F32)],
        compiler_params=_cparams(("arbitrary",)),
        name="routing",
    )(logits_t, router_b, tri)


def _position_kernel(bucket_ref, rank_ref, base_ref, pos_o):
    bucket = bucket_ref[...]
    ids = lax.broadcasted_iota(I32, (BUCKET_ROWS, bucket.shape[1]), 0)
    base = jnp.sum(jnp.where(ids == bucket, base_ref[...], 0), axis=0, keepdims=True)
    pos_o[...] = base + rank_ref[...]


def _positions(bucket, rank, base, *, tn):
    n = bucket.shape[1]
    blk = pl.BlockSpec((1, tn), lambda t: (0, t))
    return pl.pallas_call(
        _position_kernel,
        grid=(n // tn,),
        in_specs=[blk, blk, pl.BlockSpec((BUCKET_ROWS, 1), lambda t: (0, 0))],
        out_specs=blk,
        out_shape=jax.ShapeDtypeStruct((1, n), I32),
        compiler_params=_cparams(("arbitrary",)),
        name="positions",
    )(bucket, rank, base)


def _row_gather(idx_ref, src, dst, sem, n):
    for r in range(n):
        pltpu.make_async_copy(src.at[pl.ds(idx_ref[0, 0, r], 1), :], dst.at[pl.ds(r, 1), :], sem).start(priority=r % 2)


def _row_copy_wait(src, dst, sem, n):
    def wait(r, _):
        pltpu.make_async_copy(src.at[pl.ds(0, 1), :], dst.at[pl.ds(0, 1), :], sem).wait()
        return 0

    lax.fori_loop(0, n, wait, 0, unroll=8)


def _dispatch_kernel(pos_ref, h_ref, init_ref, xs_ref, sem):
    del init_ref
    tm = h_ref.shape[0]

    for r in range(tm):
        pltpu.make_async_copy(h_ref.at[pl.ds(r, 1), :], xs_ref.at[pl.ds(pos_ref[0, 0, r], 1), :],
                              sem).start(priority=r % 2)
    _row_copy_wait(h_ref, xs_ref, sem, tm)


def _dispatch(pos3, h2, xs_init, *, tm):
    n, d = h2.shape
    return pl.pallas_call(
        _dispatch_kernel,
        grid=(n // tm,),
        in_specs=[
            pl.BlockSpec((1, 1, tm), lambda t: (t, 0, 0), memory_space=pltpu.SMEM),
            pl.BlockSpec((tm, d), lambda t: (t, 0)),
            pl.BlockSpec(memory_space=pl.ANY),
        ],
        out_specs=pl.BlockSpec(memory_space=pl.ANY),
        out_shape=jax.ShapeDtypeStruct(xs_init.shape, F32),
        scratch_shapes=[pltpu.SemaphoreType.DMA(())],
        input_output_aliases={2: 0},
        compiler_params=_cparams(("arbitrary",)),
        name="dispatch",
    )(pos3, h2, xs_init)


def _moe_kernel(ea_ref, eb_ref, act_ref, xs_ref, w1a, w3a, w2a, w1b, w3b, w2b, rwa, rwb, ys_ref):
    j = pl.program_id(0)

    @pl.when(act_ref[j] == 0)
    def _():
        ys_ref[...] = jnp.zeros_like(ys_ref)

    @pl.when(act_ref[j] != 0)
    def _():
        x = xs_ref[...]
        xb = x.astype(BF16)

        def gated(u, g):
            return ((u * _sigmoid(u)) * g).astype(BF16)

        ua = jnp.dot(xb, w1a[0], preferred_element_type=F32)
        ga = jnp.dot(xb, w3a[0], preferred_element_type=F32)
        ub = jnp.dot(xb, w1b[0], preferred_element_type=F32)
        gb = jnp.dot(xb, w3b[0], preferred_element_type=F32)
        ya = jnp.dot(gated(ua, ga), w2a[0], preferred_element_type=F32)
        yb = jnp.dot(gated(ub, gb), w2b[0], preferred_element_type=F32)
        sa = _sigmoid(jnp.sum(x * rwa[0], axis=-1, keepdims=True))
        sb = _sigmoid(jnp.sum(x * rwb[0], axis=-1, keepdims=True))
        inv = 1.0 / (sa + sb)
        ys_ref[...] = (sa * inv) * ya + (sb * inv) * yb


def _moe(ea, eb, act, xs, w1, w3, w2, rw3, *, tm):
    npad, d = xs.shape
    de = w1.shape[2]
    row = lambda j, *_: (j, 0)
    wa = lambda j, ea, eb, act: (ea[j], 0, 0)
    wb = lambda j, ea, eb, act: (eb[j], 0, 0)
    grid_spec = pltpu.PrefetchScalarGridSpec(
        num_scalar_prefetch=3,
        grid=(npad // tm,),
        in_specs=[
            pl.BlockSpec((tm, d), row),
            pl.BlockSpec((1, d, de), wa), pl.BlockSpec((1, d, de), wa), pl.BlockSpec((1, de, d), wa),
            pl.BlockSpec((1, d, de), wb), pl.BlockSpec((1, d, de), wb), pl.BlockSpec((1, de, d), wb),
            pl.BlockSpec((1, 1, d), wa), pl.BlockSpec((1, 1, d), wb),
        ],
        out_specs=pl.BlockSpec((tm, d), row),
    )
    return pl.pallas_call(
        _moe_kernel,
        grid_spec=grid_spec,
        out_shape=jax.ShapeDtypeStruct((npad, d), F32),
        compiler_params=_cparams(("arbitrary",)),
        name="expert_ffn",
    )(ea, eb, act, xs, w1, w3, w2, w1, w3, w2, rw3, rw3)


def _combine_kernel(pos_ref, nxt_ref, x_ref, g2_ref, ys_ref, o_ref, buf, sem):
    t = pl.program_id(0)
    tm = x_ref.shape[0]
    slot = t % 2

    @pl.when(t == 0)
    def _():
        _row_gather(pos_ref, ys_ref, buf.at[0], sem.at[0], tm)

    @pl.when(t + 1 < pl.num_programs(0))
    def _():
        _row_gather(nxt_ref, ys_ref, buf.at[1 - slot], sem.at[1 - slot], tm)

    _row_copy_wait(ys_ref, buf.at[slot], sem.at[slot], tm)
    o_ref[...] = x_ref[...] + g2_ref[0] * buf[slot]


def _combine(pos3, x1, mod_g2, ys, *, n_ctx, seq, with_ctx, tm):
    n, d = x1.shape
    nct = n_ctx // tm if with_ctx else 0
    tps = seq // tm

    def mod_idx(t):
        return (jnp.where(t >= nct, 1 + (t - nct) // tps, 0), 0, 0)

    return pl.pallas_call(
        _combine_kernel,
        grid=(n // tm,),
        in_specs=[
            pl.BlockSpec((1, 1, tm), lambda t: (t, 0, 0), memory_space=pltpu.SMEM),
            pl.BlockSpec((1, 1, tm), lambda t: (jnp.minimum(t + 1, n // tm - 1), 0, 0), memory_space=pltpu.SMEM),
            pl.BlockSpec((tm, d), lambda t: (t, 0)),
            pl.BlockSpec((1, 1, d), mod_idx),
            pl.BlockSpec(memory_space=pl.ANY),
        ],
        out_specs=pl.BlockSpec((tm, d), lambda t: (t, 0)),
        out_shape=jax.ShapeDtypeStruct((n, d), F32),
        scratch_shapes=[pltpu.VMEM((2, tm, d), F32), pltpu.SemaphoreType.DMA((2,))],
        compiler_params=_cparams(("arbitrary",)),
        name="combine",
    )(pos3, pos3, x1, mod_g2, ys)


def _block_diag(w):
    n, d, e = w.shape
    eye = jnp.eye(n, dtype=w.dtype)
    return (eye[:, None, :, None] * w[:, :, None, :]).reshape(n * d, n * e)


def _rope_tables(seq):
    rows = seq // GRID_W
    row = jnp.repeat(jnp.arange(rows, dtype=F32), GRID_W)
    col = jnp.tile(jnp.arange(GRID_W, dtype=F32), rows)
    n_freq = HEAD_DIM // 4
    inv = ROPE_BASE ** (-jnp.arange(n_freq, dtype=F32) / n_freq)
    ang = jnp.concatenate([row[:, None] * inv, col[:, None] * inv], axis=-1)
    cos, sin = jnp.cos(ang), jnp.sin(ang)
    reps = LANES // (HEAD_DIM // 2)
    sign = jnp.tile(jnp.concatenate([-jnp.ones((HEAD_DIM // 2,), F32), jnp.ones((HEAD_DIM // 2,), F32)]),
                    LANES // HEAD_DIM)
    return jnp.tile(cos, (1, reps)), jnp.tile(sin, (1, reps)) * sign


def _max_tiles(n_tokens, tm):
    return -(-(n_tokens + N_BUCKETS * (tm - 1)) // tm)


def _tile_plan(counts, n_tiles, tm):
    padded = ((counts + tm - 1) // tm) * tm
    ends = jnp.cumsum(padded)
    base = ends - padded
    tile_start = jnp.arange(n_tiles, dtype=I32) * tm
    tile_bucket = jnp.sum((tile_start[:, None] >= ends[None, :]).astype(I32), axis=1)
    active = (tile_bucket < N_BUCKETS).astype(I32)
    last_used = jnp.maximum(jnp.sum(active) - 1, 0)
    tile_bucket = jnp.where(active == 1, tile_bucket, tile_bucket[last_used])
    tile_bucket = jnp.minimum(tile_bucket, N_BUCKETS - 1)
    grp, pair = tile_bucket // N_PAIRS, tile_bucket % N_PAIRS
    ea = grp * EXPERTS_PER_GROUP + jnp.asarray(PAIR_SLOT_A, I32)[pair]
    eb = grp * EXPERTS_PER_GROUP + jnp.asarray(PAIR_SLOT_B, I32)[pair]
    return base, ea, eb, active


def kernel(x, c, ctx, c_ctx, w_mod, b_mod, norm1_g, norm2_g, w_in, b_merge, conv_w, conv_b, lru_wa, lru_ba,
           lru_wx, lru_bx, lru_lambda, q_norm_g, k_norm_g, sink, w_branch, w_out, router_w, router_b,
           expert_w1, expert_w3, expert_w2):
    batch, seq, d = x.shape
    n_ctx_per = ctx.shape[1]
    n_layers = w_mod.shape[0]
    n_ctx = batch * n_ctx_per
    n_lat = batch * seq
    tm = min(256, n_ctx_per)
    tm_mxu = min(512, n_ctx, seq)
    tq = 128
    tk = min(512, seq)
    tm_moe = 256
    tn_route = min(1024, n_ctx_per)

    n_mod_rows = -(-(batch + 1) // 8) * 8
    cvec = jnp.zeros((n_mod_rows, d), F32).at[0].set(c_ctx).at[1:batch + 1].set(c)
    mod = _modulation(cvec, w_mod, b_mod)[:, :batch + 1]
    mod = mod.reshape(n_layers, batch + 1, 6, 1, d)

    cos_t, sin_t = _rope_tables(seq)
    seg = jnp.kron(jnp.eye(LANES // HEAD_DIM, dtype=F32),
                   jnp.full((HEAD_DIM, HEAD_DIM), 1.0 / HEAD_DIM, F32)).astype(BF16)
    eye = jnp.eye(LANES, dtype=BF16)
    tri = (jnp.arange(tn_route)[:, None] <= jnp.arange(tn_route)[None, :]).astype(BF16)
    rw_t = router_w.T
    rw_pad = jnp.zeros((d, LANES), F32).at[:, :N_EXPERTS].set(router_w)
    rw_hi = rw_pad.astype(BF16)
    rw_lo = (rw_pad - rw_hi.astype(F32)).astype(BF16)
    rw_hl = jnp.concatenate([rw_hi, rw_lo], axis=1)
    rw3 = rw_t.reshape(N_EXPERTS, 1, d)
    rb = router_b.reshape(N_EXPERTS, 1)

    x_all = (ctx.reshape(n_ctx, d), x.reshape(n_lat, d))
    n_moe_tiles = _max_tiles(n_ctx + n_lat, tm_moe)
    sorted_rows = jnp.zeros((n_moe_tiles * tm_moe, d), F32)
    routed = None
    for l in range(n_layers):
        last = l == n_layers - 1
        sh1, sc1, g1, sh2, sc2, g2 = (mod[l, :, k] for k in range(6))
        qgain = jnp.tile(q_norm_g[l], (1, LANES // HEAD_DIM))
        kgain = jnp.tile(k_norm_g[l], (1, LANES // HEAD_DIM))
        proj_args = (sh1, sc1, norm1_g[l][None], w_in[l].astype(BF16), b_merge[l][None], qgain, kgain,
                     cos_t, sin_t, seg, eye)
        proj, x_all = _input_projection(x_all, *proj_args, n_ctx=n_ctx, seq=seq, tm=tm_mxu, combine=routed)
        xr, gr, qg, kg, vg, qw, kw, vw, mg = proj
        hfb = _rglru(xr, conv_w[l], conv_b[l][None],
                     jax.vmap(_block_diag)(lru_wa[l]).astype(BF16), jax.vmap(_block_diag)(lru_wx[l]).astype(BF16),
                     lru_ba[l][:, None], lru_bx[l][:, None], lru_lambda[l][:, None],
                     batch=batch, n_ctx_per=n_ctx_per, seq=seq)
        attn_args = dict(batch=batch, n_ctx_per=n_ctx_per, seq=seq, with_ctx_queries=not last, tq=tq, tk=tk)
        yg = _attention(sink[l], qg, kg, vg, eye, band=False, use_sink=False, n_sub=min(2, n_ctx_per // tq),
                        **attn_args)
        yw = _attention(sink[l], qw, kw, vw, eye, band=True, use_sink=True, n_sub=min(2, n_ctx_per // tq),
                        **attn_args)
        x1, h2, logits_t = _merge(x_all, hfb, gr, yg, yw, mg, w_branch[l].astype(BF16), w_out[l].astype(BF16),
                                  g1, norm2_g[l][None], sh2, sc2, rw_hl,
                                  n_ctx=n_ctx, seq=seq, with_ctx=not last, tm=tm_mxu)
        n_tok = x1.shape[0]
        bucket, rank, counts = _routing(logits_t, rb, tri, tn=tn_route)
        base, ea, eb, active = _tile_plan(counts[:N_BUCKETS, 0], n_moe_tiles, tm_moe)
        base_col = jnp.zeros((BUCKET_ROWS, 1), I32).at[:N_BUCKETS, 0].set(base.astype(I32))
        pos = _positions(bucket, rank, base_col, tn=min(4096, n_ctx))
        pos3 = pos.reshape(n_tok // tm, 1, tm)
        xs = _dispatch(pos3, h2, sorted_rows, tm=tm)
        ys = _moe(ea, eb, active, xs, expert_w1[l].astype(BF16), expert_w3[l].astype(BF16),
                  expert_w2[l].astype(BF16), rw3, tm=tm_moe)
        sorted_rows = xs
        if last:
            x_all = _combine(pos3, x1, g2, ys, n_ctx=n_ctx, seq=seq, with_ctx=False, tm=tm)
        else:
            x_all, routed = x1, (pos.reshape(n_tok // tm_mxu, 1, tm_mxu), g2, ys)
    return x_all.reshape(batch, seq, d)
```

```python
import functools

import jax
import jax.numpy as jnp
from jax import lax
from jax.experimental import pallas as pl
from jax.experimental.pallas import tpu as pltpu

F32 = jnp.float32
BF16 = jnp.bfloat16
I32 = jnp.int32

HEAD_DIM = 64
N_Q_HEADS = 8
N_KV_HEADS = 2
GQA_GROUP = N_Q_HEADS // N_KV_HEADS
ATTN_WIDTH = N_Q_HEADS * HEAD_DIM
KV_WIDTH = N_KV_HEADS * HEAD_DIM
RNN_WIDTH = 512
RNN_BLOCKS = 8
CONV_WIDTH = 4
LRU_C = 8.0
WINDOW = 128
GRID_W = 64
ROPE_BASE = 10000.0
N_BRANCHES = 3
N_EXPERTS = 16
N_GROUPS = 4
EXPERTS_PER_GROUP = 4
EPS = 1e-6
NEG_INF = -1e30
V_ROWS = HEAD_DIM + 16
LOG2E = 1.4426950408889634

PAIR_SLOT_A = (0, 0, 0, 1, 1, 3)
PAIR_SLOT_B = (1, 2, 3, 3, 2, 2)
N_PAIRS = 6
N_BUCKETS = N_GROUPS * N_PAIRS
BUCKET_ROWS = 32

LANES = 128
SUBLANES = 8
VMEM_LIMIT = 56 * 1024 * 1024

HIGHEST = lax.Precision.HIGHEST


def _cparams(sem):
    return pltpu.CompilerParams(dimension_semantics=sem, vmem_limit_bytes=VMEM_LIMIT)


def _sigmoid(x):
    return 0.5 * jnp.tanh(0.5 * x) + 0.5


def _rms_mod(x, g, scale, shift):
    ms = jnp.mean(x * x, axis=-1, keepdims=True)
    return (x * lax.rsqrt(ms + EPS) * g) * (1.0 + scale) + shift


def _mod_kernel(a_ref, w_ref, b_ref, o_ref):
    a = a_ref[...]
    a = a * _sigmoid(a)
    o_ref[0] = jnp.dot(a, w_ref[0], precision=HIGHEST, preferred_element_type=F32) + b_ref[0]


def _modulation(cvec, w_mod, b_mod):
    rows, d = cvec.shape
    n_layers, _, n_out = w_mod.shape
    tn = 1536
    return pl.pallas_call(
        _mod_kernel,
        grid=(n_layers, n_out // tn),
        in_specs=[
            pl.BlockSpec((rows, d), lambda l, j: (0, 0)),
            pl.BlockSpec((1, d, tn), lambda l, j: (l, 0, j)),
            pl.BlockSpec((1, 1, tn), lambda l, j: (l, 0, j)),
        ],
        out_specs=pl.BlockSpec((1, rows, tn), lambda l, j: (l, 0, j)),
        out_shape=jax.ShapeDtypeStruct((n_layers, rows, n_out), F32),
        compiler_params=_cparams(("arbitrary", "arbitrary")),
        name="modulation",
    )(cvec, w_mod, b_mod.reshape(n_layers, 1, n_out))


def _head_norm_rope(p, gain, cos_t, sin_t, seg, lo32):
    ss = p * p
    hi = ss.astype(BF16)
    lo = (ss - hi.astype(F32)).astype(BF16)
    mean = jnp.dot(hi, seg, preferred_element_type=F32) + jnp.dot(lo, seg, preferred_element_type=F32)
    n = p * lax.rsqrt(mean + EPS) * gain
    partner = jnp.where(lo32, pltpu.roll(n, 96, 1), pltpu.roll(n, 32, 1))
    return n * cos_t + partner * sin_t


def _inproj_kernel(xc_ref, xl_ref, *refs, n_ctx_tiles):
    *refs, x_o = refs
    x = jnp.where(pl.program_id(0) >= n_ctx_tiles, xl_ref[...], xc_ref[...])
    x_o[...] = x
    _inproj_body(x, *refs, n_ctx_tiles=n_ctx_tiles)


def _inproj_combine_kernel(pos_ref, nxt_ref, g2_ref, ys_ref, x1_ref, *refs, n_ctx_tiles):
    *refs, x_o, buf, sem = refs
    t = pl.program_id(0)
    tm = x1_ref.shape[0]
    slot = t % 2

    @pl.when(t == 0)
    def _():
        _row_gather(pos_ref, ys_ref, buf.at[0], sem.at[0], tm)

    _row_copy_wait(ys_ref, buf.at[slot], sem.at[slot], tm)
    x = x1_ref[...] + g2_ref[0] * buf[slot]
    x_o[...] = x

    def gather_part(n, n_parts):
        rows = tm // n_parts
        for r in range(n * rows, (n + 1) * rows):
            pltpu.make_async_copy(ys_ref.at[pl.ds(nxt_ref[0, 0, r], 1), :], buf.at[1 - slot, pl.ds(r, 1), :],
                                  sem.at[1 - slot]).start(priority=r % 2)

    _inproj_body(x, *refs, n_ctx_tiles=n_ctx_tiles, between_groups=gather_part)

    @pl.when(t == pl.num_programs(0) - 1)
    def _():
        _row_copy_wait(ys_ref, buf.at[1 - slot], sem.at[1 - slot], tm)


def _inproj_body(x, sh_ref, sc_ref, g_ref, w_ref, bm_ref, qgain_ref, kgain_ref, cos_ref, sin_ref, seg_ref,
                 eye_ref, xr_o, gr_o, qg_o, kg_o, vg_o, qw_o, kw_o, vw_o, mg_o, *, n_ctx_tiles, between_groups=None):
    i = pl.program_id(0)
    is_lat = i >= n_ctx_tiles
    h = _rms_mod(x, g_ref[...], sc_ref[0], sh_ref[0])
    hb = h.astype(BF16)
    tm = hb.shape[0]

    def proj(c0, width):
        return jnp.dot(hb, w_ref[:, c0:c0 + width], preferred_element_type=F32)

    lane = lax.broadcasted_iota(I32, (tm, LANES), 1)
    lo32 = (lane & (HEAD_DIM - 1)) < (HEAD_DIM // 2)
    lo64 = lane < HEAD_DIM
    cos_t = jnp.where(is_lat, cos_ref[...], 1.0)
    sin_t = jnp.where(is_lat, sin_ref[...], 0.0)
    seg = seg_ref[...]
    zero = jnp.zeros((tm, LANES), F32)
    d = x.shape[1]

    def rnn_epilogue(p):
        xr_o[...] = p[:, :RNN_WIDTH].astype(BF16)
        gr_o[...] = jax.nn.gelu(p[:, RNN_WIDTH:]).astype(BF16)

    def q_epilogue(a, q_o, p):
        qgain = qgain_ref[a:a + 1, :]
        for c in range(ATTN_WIDTH // LANES):
            y = _head_norm_rope(p[:, c * LANES:(c + 1) * LANES], qgain, cos_t, sin_t, seg, lo32)
            y = y * (HEAD_DIM ** -0.5 * LOG2E)
            yr = pltpu.roll(y, HEAD_DIM, 1)
            if c < 2:
                out_a, out_b = jnp.where(lo64, y, zero), jnp.where(lo64, yr, zero)
            else:
                out_a, out_b = jnp.where(lo64, zero, yr), jnp.where(lo64, zero, y)
            q_o[2 * c] = out_a.astype(BF16)
            q_o[2 * c + 1] = out_b.astype(BF16)

    def kv_epilogue(a, k_o, v_o, p):
        k_o[...] = _head_norm_rope(p[:, :KV_WIDTH], kgain_ref[a:a + 1, :], cos_t, sin_t, seg, lo32).astype(BF16)
        pv = p[:, KV_WIDTH:]
        heads = (jnp.where(lo64, pv, 1.0), jnp.where(lo64, pltpu.roll(pv, HEAD_DIM, 1), 1.0))
        for hd, vh in enumerate(heads):
            vh = vh.astype(BF16)
            if a == 0:
                vh = lax.dot_general(eye_ref[...], vh, (((1,), (1,)), ((), ())),
                                     preferred_element_type=F32).astype(BF16)[:V_ROWS]
            v_o[hd] = vh

    def gate_epilogue(b, p):
        mg_o[:, b * d:(b + 1) * d] = _sigmoid(p + bm_ref[:, b * d:(b + 1) * d]).astype(BF16)

    groups = [(0, 2 * RNN_WIDTH, rnn_epilogue)]
    base = 2 * RNN_WIDTH
    for a, (q_o, k_o, v_o) in enumerate(((qg_o, kg_o, vg_o), (qw_o, kw_o, vw_o))):
        groups.append((base, ATTN_WIDTH, functools.partial(q_epilogue, a, q_o)))
        groups.append((base + ATTN_WIDTH, 2 * KV_WIDTH, functools.partial(kv_epilogue, a, k_o, v_o)))
        base += ATTN_WIDTH + 2 * KV_WIDTH
    for b in range(N_BRANCHES):
        groups.append((base + b * d, d, functools.partial(gate_epilogue, b)))

    p_next = proj(groups[0][0], groups[0][1])
    for n, (_, _, epilogue) in enumerate(groups):
        p_cur = p_next
        if n + 1 < len(groups):
            p_next = proj(groups[n + 1][0], groups[n + 1][1])
        if between_groups is not None:
            between_groups(n, len(groups))
        epilogue(p_cur)


def _input_projection(x_src, mod_sh, mod_sc, g, w_in, b_merge, qgain, kgain, cos_t, sin_t, seg, eye, *, n_ctx, seq, tm,
                      combine=None):
    if combine is None:
        x_ctx, x_lat = x_src
        ntok, d = x_ctx.shape[0] + x_lat.shape[0], x_ctx.shape[1]
    else:
        ntok, d = x_src.shape
    n_in = w_in.shape[1]
    nct = n_ctx // tm
    tps = seq // tm

    def mod_idx(i):
        return (jnp.where(i >= nct, 1 + (i - nct) // tps, 0), 0, 0)

    def rope_idx(i):
        return (jnp.where(i >= nct, (i - nct) % tps, 0), 0)

    row = lambda i: (i, 0)
    fixed = lambda i: (0, 0)
    out_shape = (
        jax.ShapeDtypeStruct((ntok, RNN_WIDTH), BF16),
        jax.ShapeDtypeStruct((ntok, RNN_WIDTH), BF16),
        jax.ShapeDtypeStruct((N_Q_HEADS, ntok, LANES), BF16),
        jax.ShapeDtypeStruct((ntok, KV_WIDTH), BF16),
        jax.ShapeDtypeStruct((N_KV_HEADS, V_ROWS, ntok), BF16),
        jax.ShapeDtypeStruct((N_Q_HEADS, ntok, LANES), BF16),
        jax.ShapeDtypeStruct((ntok, KV_WIDTH), BF16),
        jax.ShapeDtypeStruct((N_KV_HEADS, ntok, LANES), BF16),
        jax.ShapeDtypeStruct((ntok, N_BRANCHES * d), BF16),
    )
    q_spec = pl.BlockSpec((N_Q_HEADS, tm, LANES), lambda i: (0, i, 0))
    out_specs = (
        pl.BlockSpec((tm, RNN_WIDTH), row), pl.BlockSpec((tm, RNN_WIDTH), row),
        q_spec, pl.BlockSpec((tm, KV_WIDTH), row), pl.BlockSpec((N_KV_HEADS, V_ROWS, tm), lambda i: (0, 0, i)),
        q_spec, pl.BlockSpec((tm, KV_WIDTH), row), pl.BlockSpec((N_KV_HEADS, tm, LANES), lambda i: (0, i, 0)),
        pl.BlockSpec((tm, N_BRANCHES * d), row),
    )
    in_specs = [
        pl.BlockSpec((tm, d), row),
        pl.BlockSpec((1, 1, d), mod_idx),
        pl.BlockSpec((1, 1, d), mod_idx),
        pl.BlockSpec((1, d), fixed),
        pl.BlockSpec((d, n_in), fixed, pipeline_mode=pl.Buffered(1)),
        pl.BlockSpec((1, N_BRANCHES * d), fixed),
        pl.BlockSpec((2, LANES), fixed),
        pl.BlockSpec((2, LANES), fixed),
        pl.BlockSpec((tm, LANES), rope_idx),
        pl.BlockSpec((tm, LANES), rope_idx),
        pl.BlockSpec((LANES, LANES), fixed),
        pl.BlockSpec((LANES, LANES), fixed),
    ]
    common = (mod_sh, mod_sc, g, w_in, b_merge, qgain, kgain, cos_t, sin_t, seg, eye)
    out_specs = out_specs + (pl.BlockSpec((tm, d), row),)
    out_shape = out_shape + (jax.ShapeDtypeStruct((ntok, d), F32),)
    if combine is None:
        outs = pl.pallas_call(
            functools.partial(_inproj_kernel, n_ctx_tiles=nct),
            grid=(ntok // tm,),
            in_specs=[pl.BlockSpec((tm, d), lambda i: (jnp.minimum(i, nct - 1), 0)),
                      pl.BlockSpec((tm, d), lambda i: (jnp.maximum(i - nct, 0), 0))] + in_specs[1:],
            out_specs=out_specs,
            out_shape=out_shape,
            compiler_params=_cparams(("arbitrary",)),
            name="input_projection",
        )(x_ctx, x_lat, *common)
        return outs[:-1], outs[-1]
    pos3, mod_g2, ys = combine
    n_steps = ntok // tm
    smem_blk = lambda f: pl.BlockSpec((1, 1, tm), f, memory_space=pltpu.SMEM)
    outs = pl.pallas_call(
        functools.partial(_inproj_combine_kernel, n_ctx_tiles=nct),
        grid=(n_steps,),
        in_specs=[
            smem_blk(lambda i: (i, 0, 0)),
            smem_blk(lambda i: (jnp.minimum(i + 1, n_steps - 1), 0, 0)),
            pl.BlockSpec((1, 1, d), mod_idx),
            pl.BlockSpec(memory_space=pl.ANY),
        ] + in_specs,
        out_specs=out_specs,
        out_shape=out_shape,
        scratch_shapes=[pltpu.VMEM((2, tm, d), F32), pltpu.SemaphoreType.DMA((2,))],
        compiler_params=_cparams(("arbitrary",)),
        name="input_projection_combine",
    )(pos3, pos3, mod_g2, ys, x_src, *common)
    return outs[:-1], outs[-1]


def _rglru_kernel(xf_ref, pf_ref, nf_ref, xb_ref, pb_ref, nb_ref, cw_ref, cb_ref, wa_ref, wx_ref, ba_ref, bx_ref,
                  lam_ref, of_ref, ob_ref, a_scr, b_scr, h_scr, y_scr, pe_scr, he_scr, c_scr, *, n_chunks):
    j = pl.program_id(1)
    tc = xf_ref.shape[0]
    n_lt = xf_ref.shape[1] // LANES
    ng = tc // SUBLANES
    halo = pf_ref.shape[0]

    def gates(d, x_ref, prev_ref, next_ref, jj):
        prev_valid = jj >= 2
        next_valid = (jj >= 1) & (jj <= n_chunks - 2)
        x = x_ref[...].astype(F32)
        hp = prev_ref[...].astype(F32)
        hn = next_ref[...].astype(F32)
        pm1 = jnp.where(prev_valid, hp[halo - 1:halo, :], 0.0)
        pm2 = jnp.where(prev_valid, hp[halo - 2:halo - 1, :], 0.0)
        nx0 = jnp.where(next_valid, hn[0:1, :], 0.0)
        row = lax.broadcasted_iota(I32, x.shape, 0)
        xm1 = jnp.where(row == 0, pm1, pltpu.roll(x, 1, 0))
        xm2 = jnp.where(row == 0, pm2, jnp.where(row == 1, pm1, pltpu.roll(x, 2, 0)))
        xp1 = jnp.where(row == tc - 1, nx0, pltpu.roll(x, tc - 1, 0))
        xc = (xm2 * cw_ref[0:1, :] + xm1 * cw_ref[1:2, :] + x * cw_ref[2:3, :] + xp1 * cw_ref[3:4, :]) + cb_ref[...]
        xb = xc.astype(BF16)
        r = _sigmoid(jnp.dot(xb, wa_ref[d], preferred_element_type=F32) + ba_ref[d])
        g = _sigmoid(jnp.dot(xb, wx_ref[d], preferred_element_type=F32) + bx_ref[d])
        neg_lam = -lam_ref[d]
        softplus = jnp.maximum(neg_lam, 0.0) + jnp.log(1.0 + jnp.exp(-jnp.abs(neg_lam)))
        a = jnp.exp(-LRU_C * r * softplus)
        bb = jnp.sqrt(1.0 - a * a) * (g * xc)
        for lt in range(n_lt):
            base = (d * n_lt + lt) * tc
            a_scr[base:base + tc, :] = a[:, lt * LANES:(lt + 1) * LANES]
            b_scr[base:base + tc, :] = bb[:, lt * LANES:(lt + 1) * LANES]

    gates(0, xf_ref, pf_ref, nf_ref, j)
    gates(1, xb_ref, pb_ref, nb_ref, jnp.where(j == 0, 0, n_chunks - j))

    @pl.when(j == 0)
    def _():
        h_scr[...] = jnp.zeros_like(h_scr)

    streams = [(d, lt) for d in range(2) for lt in range(n_lt)]
    order = {0: range(SUBLANES), 1: range(SUBLANES - 1, -1, -1)}
    prod, loc = {}, {}
    for s, (d, lt) in enumerate(streams):
        p = hl = None
        for k in order[d]:
            ak = a_scr[pl.ds(s * tc + k, ng, stride=SUBLANES), :]
            bk = b_scr[pl.ds(s * tc + k, ng, stride=SUBLANES), :]
            p, hl = (ak, bk) if p is None else (ak * p, ak * hl + bk)
            prod[s, k], loc[s, k] = p, hl
        pe_scr[s * ng:(s + 1) * ng, :] = p
        he_scr[s * ng:(s + 1) * ng, :] = hl
    c = [h_scr[s:s + 1, :] for s in range(len(streams))]
    for step in range(ng):
        for s, (d, lt) in enumerate(streams):
            r = s * ng + (step if d == 0 else ng - 1 - step)
            c_scr[r:r + 1, :] = c[s]
            c[s] = pe_scr[r:r + 1, :] * c[s] + he_scr[r:r + 1, :]
    for s, (d, lt) in enumerate(streams):
        h_scr[s:s + 1, :] = c[s]
        carry_in = c_scr[s * ng:(s + 1) * ng, :]
        for k in order[d]:
            y_scr[pl.ds(s * tc + k, ng, stride=SUBLANES), :] = loc[s, k] + prod[s, k] * carry_in
        o_ref = of_ref if d == 0 else ob_ref
        o_ref[:, lt * LANES:(lt + 1) * LANES] = y_scr[s * tc:(s + 1) * tc, :].astype(BF16)


def _rglru(xr, conv_w, conv_b, wa, wx, ba, bx, lam, *, batch, n_ctx_per, seq):
    ntok, width = xr.shape
    tc = n_ctx_per
    halo = 16
    n_chunks = 1 + seq // tc
    n_lt = width // LANES
    lat0 = batch

    def seq_chunk(d, j):
        return jnp.where(d == 0, j, jnp.where(j == 0, 0, n_chunks - j))

    def chunk_blk(d, b, j):
        jj = seq_chunk(d, j)
        return jnp.where(jj == 0, b, lat0 + b * (seq // tc) + jj - 1)

    def x_spec(d):
        return pl.BlockSpec((tc, width), lambda b, j: (chunk_blk(d, b, j), 0))

    def prev_spec(d):
        return pl.BlockSpec((halo, width), lambda b, j: (jnp.maximum(chunk_blk(d, b, j) * (tc // halo) - 1, 0), 0))

    def next_spec(d):
        return pl.BlockSpec(
            (halo, width), lambda b, j: (jnp.minimum((chunk_blk(d, b, j) + 1) * (tc // halo), ntok // halo - 1), 0))

    both = lambda b, j: (0, 0, 0)
    fixed = lambda b, j: (0, 0)
    n_str = 2 * n_lt
    return pl.pallas_call(
        functools.partial(_rglru_kernel, n_chunks=n_chunks),
        grid=(batch, n_chunks),
        in_specs=[
            x_spec(0), prev_spec(0), next_spec(0), x_spec(1), prev_spec(1), next_spec(1),
            pl.BlockSpec((CONV_WIDTH, width), fixed),
            pl.BlockSpec((1, width), fixed),
            pl.BlockSpec((2, width, width), both),
            pl.BlockSpec((2, width, width), both),
            pl.BlockSpec((2, 1, width), both),
            pl.BlockSpec((2, 1, width), both),
            pl.BlockSpec((2, 1, width), both),
        ],
        out_specs=(pl.BlockSpec((tc, width), lambda b, j: (chunk_blk(0, b, j), 0)),
                   pl.BlockSpec((tc, width), lambda b, j: (chunk_blk(1, b, j), 0))),
        out_shape=(jax.ShapeDtypeStruct((ntok, width), BF16), jax.ShapeDtypeStruct((ntok, width), BF16)),
        scratch_shapes=[
            pltpu.VMEM((n_str * tc, LANES), F32), pltpu.VMEM((n_str * tc, LANES), F32),
            pltpu.VMEM((n_str, LANES), F32), pltpu.VMEM((n_str * tc, LANES), F32),
            pltpu.VMEM((n_str * tc // SUBLANES, LANES), F32), pltpu.VMEM((n_str * tc // SUBLANES, LANES), F32),
            pltpu.VMEM((n_str * tc // SUBLANES, LANES), F32),
        ],
        compiler_params=_cparams(("arbitrary", "arbitrary")),
        name="rglru",
    )(xr, xr, xr, xr, xr, xr, conv_w, conv_b, wa, wx, ba, bx, lam)


LOOKAHEAD = 2


def _col_max(s):
    parts = [s[r:r + SUBLANES] for r in range(0, s.shape[0], SUBLANES)]
    while len(parts) > 1:
        nxt = [jnp.maximum(parts[k], parts[k + 1]) for k in range(0, len(parts) - 1, 2)]
        if len(parts) % 2:
            nxt.append(parts[-1])
        parts = nxt
    return jnp.max(parts[0], axis=0, keepdims=True)


def _attn_kernel(sink_ref, q_ref, kc_ref, vc_ref, kl_ref, vl_ref, eye_ref, o_ref, *,
                 band, use_sink, tile_off, n_ctx_tiles, tiles_per_seq, tk, tq, pv_delay):
    i = pl.program_id(0) + tile_off
    is_lat = i >= n_ctx_tiles
    n_sub = q_ref.shape[1] // tq
    qi0 = jnp.where(is_lat, (i - n_ctx_tiles) % tiles_per_seq, 0) * n_sub
    nq = tiles_per_seq * n_sub
    nq_rows = GQA_GROUP * tq
    seq = kl_ref.shape[0]
    contract_last = (((1,), (1,)), ((), ()))
    lo64 = lax.broadcasted_iota(I32, (tq, LANES), 1) < HEAD_DIM
    chan = lax.broadcasted_iota(I32, (V_ROWS, nq_rows), 0)

    def transposed(v):
        return lax.dot_general(eye_ref[:V_ROWS, :], v, contract_last, preferred_element_type=F32).astype(BF16)

    def scores(q, k, mask):
        s = lax.dot_general(k, q, contract_last, preferred_element_type=F32)
        return s if mask is None else jnp.where(mask, s, NEG_INF)

    def run(latent):
        qs, carries, blocks = [], [], []
        for u in range(n_sub):
            if latent and band:
                qi = qi0 + u
                starts = (jnp.maximum(qi - 1, 0) * tq, qi * tq, jnp.minimum(qi + 1, nq - 1) * tq)
                starts = [pl.multiple_of(s0, tq) for s0 in starts]
                kpos = lax.broadcasted_iota(I32, (3 * tq, nq_rows), 0)
                qpos = lax.broadcasted_iota(I32, (3 * tq, nq_rows), 1) & (tq - 1)
                rel = kpos - tq - qpos
                kmin = jnp.where(qi > 0, 0, tq)
                kmax = jnp.where(qi < nq - 1, 3 * tq, 2 * tq)
                band_ok = (rel >= -WINDOW) & (rel <= WINDOW) & (kpos >= kmin) & (kpos < kmax)
            for h in range(N_KV_HEADS):
                qs.append(q_ref[GQA_GROUP * h:GQA_GROUP * (h + 1), u * tq:(u + 1) * tq, :].reshape(nq_rows, LANES))
                if use_sink:
                    m0 = jnp.concatenate(
                        [jnp.full((1, tq), sink_ref[GQA_GROUP * h + g] * LOG2E, F32) for g in range(GQA_GROUP)],
                        axis=1)
                    acc0 = jnp.where(chan >= HEAD_DIM, 1.0, 0.0)
                else:
                    m0 = jnp.full((1, nq_rows), NEG_INF, F32)
                    acc0 = jnp.zeros((V_ROWS, nq_rows), F32)
                carries.append((m0, acc0))
                blk = [(lambda: kc_ref[...],
                        (lambda h=h: transposed(vc_ref[h])) if band else (lambda h=h: vc_ref[h]), None)]
                if latent and band:
                    blk.append((
                        lambda starts=starts: jnp.concatenate([kl_ref[pl.ds(s0, tq), :] for s0 in starts], axis=0),
                        lambda starts=starts, h=h: transposed(
                            jnp.concatenate([vl_ref[h, pl.ds(s0, tq), :] for s0 in starts], axis=0)),
                        band_ok))
                elif latent:
                    for c in range(seq // tk):
                        blk.append((lambda c=c: kl_ref[c * tk:(c + 1) * tk, :],
                                    lambda c=c, h=h: vl_ref[h, :, c * tk:(c + 1) * tk], None))
                blocks.append(blk)

        n_streams = len(qs)
        items = [(st, c) for c in range(len(blocks[0])) for st in range(n_streams)]

        def issue(item):
            st, c = item
            return scores(qs[st], blocks[st][c][0](), blocks[st][c][2])

        depth = LOOKAHEAD if band else LOOKAHEAD * n_sub
        ahead = [issue(it) for it in items[:depth]]
        pending = []

        def value_matmul():
            st, alpha, p, vt = pending.pop(0)
            m, acc = carries[st]
            carries[st] = (m, alpha * acc + jnp.dot(vt, p, preferred_element_type=F32))

        for idx, (st, c) in enumerate(items):
            s_cur = ahead.pop(0)
            if idx + depth < len(items):
                ahead.append(issue(items[idx + depth]))
            if pv_delay and len(pending) >= pv_delay:
                value_matmul()
            m, acc = carries[st]
            m_new = jnp.maximum(m, _col_max(s_cur))
            pending.append((st, jnp.exp2(m - m_new), jnp.exp2(s_cur - m_new).astype(BF16), blocks[st][c][1]()))
            carries[st] = (m_new, acc)
            if not pv_delay:
                value_matmul()
        while pending:
            value_matmul()

        for st in range(n_streams):
            u, h = divmod(st, N_KV_HEADS)
            _, acc = carries[st]
            o = acc * (1.0 / acc[HEAD_DIM:HEAD_DIM + 1, :])
            o = jnp.concatenate([o, jnp.zeros((LANES - V_ROWS, nq_rows), F32)], axis=0)
            o = jnp.transpose(o)
            og = [o[g * tq:(g + 1) * tq] for g in range(GQA_GROUP)]
            for c in range(GQA_GROUP // 2):
                chunk = jnp.where(lo64, og[2 * c], pltpu.roll(og[2 * c + 1], HEAD_DIM, 1))
                col = (h * (GQA_GROUP // 2) + c) * LANES
                o_ref[u * tq:(u + 1) * tq, col:col + LANES] = chunk.astype(BF16)

    if tile_off == 0:
        pl.when(is_lat)(lambda: run(True))
        pl.when(jnp.logical_not(is_lat))(lambda: run(False))
    else:
        run(True)


def _attention(sink, q, k, v, eye, *, band, use_sink, batch, n_ctx_per, seq, with_ctx_queries, tq, tk, n_sub):
    ntok = k.shape[0]
    n_ctx = batch * n_ctx_per
    assert n_ctx % seq == 0, "context rows must cover whole latent-sequence blocks"
    assert tq == WINDOW
    rows = n_sub * tq
    assert n_ctx_per % rows == 0 and seq % rows == 0
    nct = n_ctx // rows
    tps = seq // rows
    tile_off = 0 if with_ctx_queries else nct
    n_tiles = ntok // rows - tile_off

    def batch_of(t):
        i = t + tile_off
        return jnp.where(i >= nct, (i - nct) // tps, (i * rows) // n_ctx_per)

    ctx_idx = lambda t, *_: (batch_of(t), 0)
    lat_idx = lambda t, *_: (n_ctx // seq + batch_of(t), 0)
    if band:
        vc_spec = pl.BlockSpec((N_KV_HEADS, n_ctx_per, LANES), lambda t, *_: (0, batch_of(t), 0))
        vl_spec = pl.BlockSpec((N_KV_HEADS, seq, LANES), lambda t, *_: (0, n_ctx // seq + batch_of(t), 0))
    else:
        vc_spec = pl.BlockSpec((N_KV_HEADS, V_ROWS, n_ctx_per), lambda t, *_: (0, 0, batch_of(t)))
        vl_spec = pl.BlockSpec((N_KV_HEADS, V_ROWS, seq), lambda t, *_: (0, 0, n_ctx // seq + batch_of(t)))
    grid_spec = pltpu.PrefetchScalarGridSpec(
        num_scalar_prefetch=1,
        grid=(n_tiles,),
        in_specs=[
            pl.BlockSpec((N_Q_HEADS, rows, LANES), lambda t, *_: (0, t + tile_off, 0)),
            pl.BlockSpec((n_ctx_per, KV_WIDTH), ctx_idx),
            vc_spec,
            pl.BlockSpec((seq, KV_WIDTH), lat_idx),
            vl_spec,
            pl.BlockSpec((LANES, LANES), lambda t, *_: (0, 0)),
        ],
        out_specs=pl.BlockSpec((rows, ATTN_WIDTH), lambda t, *_: (t, 0)),
    )
    return pl.pallas_call(
        functools.partial(_attn_kernel, band=band, use_sink=use_sink, tile_off=tile_off,
                          n_ctx_tiles=nct, tiles_per_seq=tps, tk=tk, tq=tq, pv_delay=2),
        grid_spec=grid_spec,
        out_shape=jax.ShapeDtypeStruct((n_tiles * rows, ATTN_WIDTH), BF16),
        compiler_params=_cparams(("arbitrary",)),
        name="window_attention" if band else "global_attention",
    )(sink, q, k, v, k, v, eye)


def _merge_kernel(x_ref, hf_ref, hb_ref, gr_ref, yg_ref, yw_ref, mg_ref, wb_ref, wo_ref, g1_ref, n2_ref, sh2_ref, sc2_ref,
                  rwhl_ref, x1_o, h2_o, lg_o):
    tm, d = x_ref.shape
    n_parts = 2 if tm % 256 == 0 else 1
    part = tm // n_parts
    rows = [slice(k * part, (k + 1) * part) for k in range(n_parts)]

    def branches(r):
        y_rnn = ((hf_ref[r, :].astype(F32) + hb_ref[r, :].astype(F32)) * gr_ref[r, :].astype(F32)).astype(BF16)
        ys = (y_rnn, yg_ref[r, :], yw_ref[r, :])
        merged = None
        for b in range(N_BRANCHES):
            t = mg_ref[r, b * d:(b + 1) * d].astype(F32) * jnp.dot(ys[b], wb_ref[b], preferred_element_type=F32)
            merged = t if merged is None else merged + t
        return merged.astype(BF16)

    merged = [branches(r) for r in rows]
    outs = [jnp.dot(m, wo_ref[...], preferred_element_type=F32) for m in merged]
    h2s = []
    for r, out in zip(rows, outs):
        x1 = x_ref[r, :] + g1_ref[0] * out
        x1_o[r, :] = x1
        h2 = _rms_mod(x1, n2_ref[...], sc2_ref[0], sh2_ref[0])
        h2_o[r, :] = h2
        h2s.append(h2)
    for r, h2 in zip(rows, h2s):
        hi = h2.astype(BF16)
        lo = (h2 - hi.astype(F32)).astype(BF16)
        a = jnp.dot(hi, rwhl_ref[...], preferred_element_type=F32)
        lg = a[:, :LANES] + (jnp.dot(lo, rwhl_ref[:, :LANES], preferred_element_type=F32) + a[:, LANES:])
        lg_o[:, r] = jnp.transpose(lg)[:N_EXPERTS, :]


def _merge(x_all, hfb, gr, yg, yw, mg, wb, wo, mod_g1, n2, mod_sh2, mod_sc2, rw_hl, *, n_ctx, seq, with_ctx, tm):
    ntok, d = x_all.shape
    nct = n_ctx // tm
    tps = seq // tm
    off = 0 if with_ctx else nct
    n_tiles = ntok // tm - off
    n_out = n_tiles * tm

    def mod_idx(t):
        i = t + off
        return (jnp.where(i >= nct, 1 + (i - nct) // tps, 0), 0, 0)

    row_in = lambda t: (t + off, 0)
    row_out = lambda t: (t, 0)
    fixed2 = lambda t: (0, 0)
    return pl.pallas_call(
        _merge_kernel,
        grid=(n_tiles,),
        in_specs=[
            pl.BlockSpec((tm, d), row_in),
            pl.BlockSpec((tm, RNN_WIDTH), row_in),
            pl.BlockSpec((tm, RNN_WIDTH), row_in),
            pl.BlockSpec((tm, RNN_WIDTH), row_in),
            pl.BlockSpec((tm, ATTN_WIDTH), row_out),
            pl.BlockSpec((tm, ATTN_WIDTH), row_out),
            pl.BlockSpec((tm, N_BRANCHES * d), row_in),
            pl.BlockSpec((N_BRANCHES, RNN_WIDTH, d), lambda t: (0, 0, 0)),
            pl.BlockSpec((d, d), fixed2),
            pl.BlockSpec((1, 1, d), mod_idx),
            pl.BlockSpec((1, d), fixed2),
            pl.BlockSpec((1, 1, d), mod_idx),
            pl.BlockSpec((1, 1, d), mod_idx),
            pl.BlockSpec((d, 2 * LANES), fixed2),
        ],
        out_specs=(
            pl.BlockSpec((tm, d), row_out),
            pl.BlockSpec((tm, d), row_out),
            pl.BlockSpec((N_EXPERTS, tm), lambda t: (0, t)),
        ),
        out_shape=(
            jax.ShapeDtypeStruct((n_out, d), F32),
            jax.ShapeDtypeStruct((n_out, d), F32),
            jax.ShapeDtypeStruct((N_EXPERTS, n_out), F32),
        ),
        compiler_params=_cparams(("arbitrary",)),
        name="merge",
    )(x_all, *hfb, gr, yg, yw, mg, wb, wo, mod_g1, n2, mod_sh2, mod_sc2, rw_hl)


def _first_argmax(vals):
    best, idx = vals[0], jnp.zeros(vals[0].shape, I32)
    for k in range(1, len(vals)):
        take = vals[k] > best
        best = jnp.where(take, vals[k], best)
        idx = jnp.where(take, k, idx)
    return best, idx


def _routing_kernel(lg_ref, rb_ref, tri_ref, bucket_o, rank_o, count_o, carry_scr):
    t = pl.program_id(0)

    @pl.when(t == 0)
    def _():
        carry_scr[...] = jnp.zeros_like(carry_scr)

    s = _sigmoid(lg_ref[...])
    sel = s + rb_ref[...]
    rows = [sel[e:e + 1, :] for e in range(N_EXPERTS)]
    tn = s.shape[1]

    grp_scores = []
    for g in range(N_GROUPS):
        r = rows[g * EXPERTS_PER_GROUP:(g + 1) * EXPERTS_PER_GROUP]
        best = None
        for a in range(EXPERTS_PER_GROUP):
            for b in range(a + 1, EXPERTS_PER_GROUP):
                pair = r[a] + r[b]
                best = pair if best is None else jnp.maximum(best, pair)
        grp_scores.append(best)
    _, grp = _first_argmax(grp_scores)

    vals = []
    for k in range(EXPERTS_PER_GROUP):
        v = rows[k]
        for g in range(1, N_GROUPS):
            v = jnp.where(grp == g, rows[g * EXPERTS_PER_GROUP + k], v)
        vals.append(v)
    _, i1 = _first_argmax(vals)
    _, i2 = _first_argmax([jnp.where(i1 == k, -jnp.inf, vals[k]) for k in range(EXPERTS_PER_GROUP)])
    lo = jnp.minimum(i1, i2)
    hi = jnp.maximum(i1, i2)
    pair = jnp.where(lo == 0, hi - 1, jnp.where(lo == 1, 6 - hi, 5))
    bucket = grp * N_PAIRS + pair
    bucket_o[...] = bucket

    onehot = (lax.broadcasted_iota(I32, (BUCKET_ROWS, tn), 0) == bucket).astype(F32)
    incl = jnp.dot(onehot.astype(BF16), tri_ref[...], preferred_element_type=F32)
    before = carry_scr[...] + incl - 1.0
    rank_o[...] = jnp.sum(onehot * before, axis=0, keepdims=True).astype(I32)
    carry_scr[...] = carry_scr[...] + jnp.sum(onehot, axis=1, keepdims=True)
    count_o[...] = jnp.broadcast_to(carry_scr[...], count_o.shape).astype(I32)


def _routing(logits_t, router_b, tri, *, tn):
    n = logits_t.shape[1]
    return pl.pallas_call(
        _routing_kernel,
        grid=(n // tn,),
        in_specs=[
            pl.BlockSpec((N_EXPERTS, tn), lambda t: (0, t)),
            pl.BlockSpec((N_EXPERTS, 1), lambda t: (0, 0)),
            pl.BlockSpec((tn, tn), lambda t: (0, 0)),
        ],
        out_specs=(
            pl.BlockSpec((1, tn), lambda t: (0, t)),
            pl.BlockSpec((1, tn), lambda t: (0, t)),
            pl.BlockSpec((BUCKET_ROWS, LANES), lambda t: (0, 0)),
        ),
        out_shape=(
            jax.ShapeDtypeStruct((1, n), I32),
            jax.ShapeDtypeStruct((1, n), I32),
            jax.ShapeDtypeStruct((BUCKET_ROWS, LANES), I32),
        ),
        scratch_shapes=[pltpu.VMEM((BUCKET_ROWS, 1), F32)],
        compiler_params=_cparams(("arbitrary",)),
        name="routing",
    )(logits_t, router_b, tri)


def _position_kernel(bucket_ref, rank_ref, base_ref, pos_o):
    bucket = bucket_ref[...]
    ids = lax.broadcasted_iota(I32, (BUCKET_ROWS, bucket.shape[1]), 0)
    base = jnp.sum(jnp.where(ids == bucket, base_ref[...], 0), axis=0, keepdims=True)
    pos_o[...] = base + rank_ref[...]


def _positions(bucket, rank, base, *, tn):
    n = bucket.shape[1]
    blk = pl.BlockSpec((1, tn), lambda t: (0, t))
    return pl.pallas_call(
        _position_kernel,
        grid=(n // tn,),
        in_specs=[blk, blk, pl.BlockSpec((BUCKET_ROWS, 1), lambda t: (0, 0))],
        out_specs=blk,
        out_shape=jax.ShapeDtypeStruct((1, n), I32),
        compiler_params=_cparams(("arbitrary",)),
        name="positions",
    )(bucket, rank, base)


def _row_gather(idx_ref, src, dst, sem, n):
    for r in range(n):
        pltpu.make_async_copy(src.at[pl.ds(idx_ref[0, 0, r], 1), :], dst.at[pl.ds(r, 1), :], sem).start(priority=r % 2)


def _row_copy_wait(src, dst, sem, n):
    def wait(r, _):
        pltpu.make_async_copy(src.at[pl.ds(0, 1), :], dst.at[pl.ds(0, 1), :], sem).wait()
        return 0

    lax.fori_loop(0, n, wait, 0, unroll=8)


def _dispatch_kernel(pos_ref, h_ref, init_ref, xs_ref, sem):
    del init_ref
    tm = h_ref.shape[0]

    for r in range(tm):
        pltpu.make_async_copy(h_ref.at[pl.ds(r, 1), :], xs_ref.at[pl.ds(pos_ref[0, 0, r], 1), :],
                              sem).start(priority=r % 2)
    _row_copy_wait(h_ref, xs_ref, sem, tm)


def _dispatch(pos3, h2, xs_init, *, tm):
    n, d = h2.shape
    return pl.pallas_call(
        _dispatch_kernel,
        grid=(n // tm,),
        in_specs=[
            pl.BlockSpec((1, 1, tm), lambda t: (t, 0, 0), memory_space=pltpu.SMEM),
            pl.BlockSpec((tm, d), lambda t: (t, 0)),
            pl.BlockSpec(memory_space=pl.ANY),
        ],
        out_specs=pl.BlockSpec(memory_space=pl.ANY),
        out_shape=jax.ShapeDtypeStruct(xs_init.shape, F32),
        scratch_shapes=[pltpu.SemaphoreType.DMA(())],
        input_output_aliases={2: 0},
        compiler_params=_cparams(("arbitrary",)),
        name="dispatch",
    )(pos3, h2, xs_init)


def _moe_kernel(ea_ref, eb_ref, act_ref, xs_ref, w1a, w3a, w2a, w1b, w3b, w2b, rwa, rwb, ys_ref):
    j = pl.program_id(0)

    @pl.when(act_ref[j] == 0)
    def _():
        ys_ref[...] = jnp.zeros_like(ys_ref)

    @pl.when(act_ref[j] != 0)
    def _():
        x = xs_ref[...]
        xb = x.astype(BF16)

        def gated(u, g):
            return ((u * _sigmoid(u)) * g).astype(BF16)

        ua = jnp.dot(xb, w1a[0], preferred_element_type=F32)
        ga = jnp.dot(xb, w3a[0], preferred_element_type=F32)
        ub = jnp.dot(xb, w1b[0], preferred_element_type=F32)
        gb = jnp.dot(xb, w3b[0], preferred_element_type=F32)
        ya = jnp.dot(gated(ua, ga), w2a[0], preferred_element_type=F32)
        yb = jnp.dot(gated(ub, gb), w2b[0], preferred_element_type=F32)
        sa = _sigmoid(jnp.sum(x * rwa[0], axis=-1, keepdims=True))
        sb = _sigmoid(jnp.sum(x * rwb[0], axis=-1, keepdims=True))
        inv = 1.0 / (sa + sb)
        ys_ref[...] = (sa * inv) * ya + (sb * inv) * yb


def _moe(ea, eb, act, xs, w1, w3, w2, rw3, *, tm):
    npad, d = xs.shape
    de = w1.shape[2]
    row = lambda j, *_: (j, 0)
    wa = lambda j, ea, eb, act: (ea[j], 0, 0)
    wb = lambda j, ea, eb, act: (eb[j], 0, 0)
    grid_spec = pltpu.PrefetchScalarGridSpec(
        num_scalar_prefetch=3,
        grid=(npad // tm,),
        in_specs=[
            pl.BlockSpec((tm, d), row),
            pl.BlockSpec((1, d, de), wa), pl.BlockSpec((1, d, de), wa), pl.BlockSpec((1, de, d), wa),
            pl.BlockSpec((1, d, de), wb), pl.BlockSpec((1, d, de), wb), pl.BlockSpec((1, de, d), wb),
            pl.BlockSpec((1, 1, d), wa), pl.BlockSpec((1, 1, d), wb),
        ],
        out_specs=pl.BlockSpec((tm, d), row),
    )
    return pl.pallas_call(
        _moe_kernel,
        grid_spec=grid_spec,
        out_shape=jax.ShapeDtypeStruct((npad, d), F32),
        compiler_params=_cparams(("arbitrary",)),
        name="expert_ffn",
    )(ea, eb, act, xs, w1, w3, w2, w1, w3, w2, rw3, rw3)


def _combine_kernel(pos_ref, nxt_ref, x_ref, g2_ref, ys_ref, o_ref, buf, sem):
    t = pl.program_id(0)
    tm = x_ref.shape[0]
    slot = t % 2

    @pl.when(t == 0)
    def _():
        _row_gather(pos_ref, ys_ref, buf.at[0], sem.at[0], tm)

    @pl.when(t + 1 < pl.num_programs(0))
    def _():
        _row_gather(nxt_ref, ys_ref, buf.at[1 - slot], sem.at[1 - slot], tm)

    _row_copy_wait(ys_ref, buf.at[slot], sem.at[slot], tm)
    o_ref[...] = x_ref[...] + g2_ref[0] * buf[slot]


def _combine(pos3, x1, mod_g2, ys, *, n_ctx, seq, with_ctx, tm):
    n, d = x1.shape
    nct = n_ctx // tm if with_ctx else 0
    tps = seq // tm

    def mod_idx(t):
        return (jnp.where(t >= nct, 1 + (t - nct) // tps, 0), 0, 0)

    return pl.pallas_call(
        _combine_kernel,
        grid=(n // tm,),
        in_specs=[
            pl.BlockSpec((1, 1, tm), lambda t: (t, 0, 0), memory_space=pltpu.SMEM),
            pl.BlockSpec((1, 1, tm), lambda t: (jnp.minimum(t + 1, n // tm - 1), 0, 0), memory_space=pltpu.SMEM),
            pl.BlockSpec((tm, d), lambda t: (t, 0)),
            pl.BlockSpec((1, 1, d), mod_idx),
            pl.BlockSpec(memory_space=pl.ANY),
        ],
        out_specs=pl.BlockSpec((tm, d), lambda t: (t, 0)),
        out_shape=jax.ShapeDtypeStruct((n, d), F32),
        scratch_shapes=[pltpu.VMEM((2, tm, d), F32), pltpu.SemaphoreType.DMA((2,))],
        compiler_params=_cparams(("arbitrary",)),
        name="combine",
    )(pos3, pos3, x1, mod_g2, ys)


def _block_diag(w):
    n, d, e = w.shape
    eye = jnp.eye(n, dtype=w.dtype)
    return (eye[:, None, :, None] * w[:, :, None, :]).reshape(n * d, n * e)


def _rope_tables(seq):
    rows = seq // GRID_W
    row = jnp.repeat(jnp.arange(rows, dtype=F32), GRID_W)
    col = jnp.tile(jnp.arange(GRID_W, dtype=F32), rows)
    n_freq = HEAD_DIM // 4
    inv = ROPE_BASE ** (-jnp.arange(n_freq, dtype=F32) / n_freq)
    ang = jnp.concatenate([row[:, None] * inv, col[:, None] * inv], axis=-1)
    cos, sin = jnp.cos(ang), jnp.sin(ang)
    reps = LANES // (HEAD_DIM // 2)
    sign = jnp.tile(jnp.concatenate([-jnp.ones((HEAD_DIM // 2,), F32), jnp.ones((HEAD_DIM // 2,), F32)]),
                    LANES // HEAD_DIM)
    return jnp.tile(cos, (1, reps)), jnp.tile(sin, (1, reps)) * sign


def _max_tiles(n_tokens, tm):
    return -(-(n_tokens + N_BUCKETS * (tm - 1)) // tm)


def _tile_plan(counts, n_tiles, tm):
    padded = ((counts + tm - 1) // tm) * tm
    ends = jnp.cumsum(padded)
    base = ends - padded
    tile_start = jnp.arange(n_tiles, dtype=I32) * tm
    tile_bucket = jnp.sum((tile_start[:, None] >= ends[None, :]).astype(I32), axis=1)
    active = (tile_bucket < N_BUCKETS).astype(I32)
    last_used = jnp.maximum(jnp.sum(active) - 1, 0)
    tile_bucket = jnp.where(active == 1, tile_bucket, tile_bucket[last_used])
    tile_bucket = jnp.minimum(tile_bucket, N_BUCKETS - 1)
    grp, pair = tile_bucket // N_PAIRS, tile_bucket % N_PAIRS
    ea = grp * EXPERTS_PER_GROUP + jnp.asarray(PAIR_SLOT_A, I32)[pair]
    eb = grp * EXPERTS_PER_GROUP + jnp.asarray(PAIR_SLOT_B, I32)[pair]
    return base, ea, eb, active


def kernel(x, c, ctx, c_ctx, w_mod, b_mod, norm1_g, norm2_g, w_in, b_merge, conv_w, conv_b, lru_wa, lru_ba,
           lru_wx, lru_bx, lru_lambda, q_norm_g, k_norm_g, sink, w_branch, w_out, router_w, router_b,
           expert_w1, expert_w3, expert_w2):
    batch, seq, d = x.shape
    n_ctx_per = ctx.shape[1]
    n_layers = w_mod.shape[0]
    n_ctx = batch * n_ctx_per
    n_lat = batch * seq
    tm = min(256, n_ctx_per)
    tm_mxu = min(512, n_ctx, seq)
    tq = 128
    tk = min(512, seq)
    tm_moe = 256
    tn_route = min(1024, n_ctx_per)

    n_mod_rows = -(-(batch + 1) // 8) * 8
    cvec = jnp.zeros((n_mod_rows, d), F32).at[0].set(c_ctx).at[1:batch + 1].set(c)
    mod = _modulation(cvec, w_mod, b_mod)[:, :batch + 1]
    mod = mod.reshape(n_layers, batch + 1, 6, 1, d)

    cos_t, sin_t = _rope_tables(seq)
    seg = jnp.kron(jnp.eye(LANES // HEAD_DIM, dtype=F32),
                   jnp.full((HEAD_DIM, HEAD_DIM), 1.0 / HEAD_DIM, F32)).astype(BF16)
    eye = jnp.eye(LANES, dtype=BF16)
    tri = (jnp.arange(tn_route)[:, None] <= jnp.arange(tn_route)[None, :]).astype(BF16)
    rw_t = router_w.T
    rw_pad = jnp.zeros((d, LANES), F32).at[:, :N_EXPERTS].set(router_w)
    rw_hi = rw_pad.astype(BF16)
    rw_lo = (rw_pad - rw_hi.astype(F32)).astype(BF16)
    rw_hl = jnp.concatenate([rw_hi, rw_lo], axis=1)
    rw3 = rw_t.reshape(N_EXPERTS, 1, d)
    rb = router_b.reshape(N_EXPERTS, 1)

    x_all = (ctx.reshape(n_ctx, d), x.reshape(n_lat, d))
    n_moe_tiles = _max_tiles(n_ctx + n_lat, tm_moe)
    sorted_rows = jnp.zeros((n_moe_tiles * tm_moe, d), F32)
    routed = None
    for l in range(n_layers):
        last = l == n_layers - 1
        sh1, sc1, g1, sh2, sc2, g2 = (mod[l, :, k] for k in range(6))
        qgain = jnp.tile(q_norm_g[l], (1, LANES // HEAD_DIM))
        kgain = jnp.tile(k_norm_g[l], (1, LANES // HEAD_DIM))
        proj_args = (sh1, sc1, norm1_g[l][None], w_in[l].astype(BF16), b_merge[l][None], qgain, kgain,
                     cos_t, sin_t, seg, eye)
        proj, x_all = _input_projection(x_all, *proj_args, n_ctx=n_ctx, seq=seq, tm=tm_mxu, combine=routed)
        xr, gr, qg, kg, vg, qw, kw, vw, mg = proj
        hfb = _rglru(xr, conv_w[l], conv_b[l][None],
                     jax.vmap(_block_diag)(lru_wa[l]).astype(BF16), jax.vmap(_block_diag)(lru_wx[l]).astype(BF16),
                     lru_ba[l][:, None], lru_bx[l][:, None], lru_lambda[l][:, None],
                     batch=batch, n_ctx_per=n_ctx_per, seq=seq)
        attn_args = dict(batch=batch, n_ctx_per=n_ctx_per, seq=seq, with_ctx_queries=not last, tq=tq, tk=tk)
        yg = _attention(sink[l], qg, kg, vg, eye, band=False, use_sink=False, n_sub=min(2, n_ctx_per // tq),
                        **attn_args)
        yw = _attention(sink[l], qw, kw, vw, eye, band=True, use_sink=True, n_sub=min(2, n_ctx_per // tq),
                        **attn_args)
        x1, h2, logits_t = _merge(x_all, hfb, gr, yg, yw, mg, w_branch[l].astype(BF16), w_out[l].astype(BF16),
                                  g1, norm2_g[l][None], sh2, sc2, rw_hl,
                                  n_ctx=n_ctx, seq=seq, with_ctx=not last, tm=tm_mxu)
        n_tok = x1.shape[0]
        bucket, rank, counts = _routing(logits_t, rb, tri, tn=tn_route)
        base, ea, eb, active = _tile_plan(counts[:N_BUCKETS, 0], n_moe_tiles, tm_moe)
        base_col = jnp.zeros((BUCKET_ROWS, 1), I32).at[:N_BUCKETS, 0].set(base.astype(I32))
        pos = _positions(bucket, rank, base_col, tn=min(4096, n_ctx))
        pos3 = pos.reshape(n_tok // tm, 1, tm)
        xs = _dispatch(pos3, h2, sorted_rows, tm=tm)
        ys = _moe(ea, eb, active, xs, expert_w1[l].astype(BF16), expert_w3[l].astype(BF16),
                  expert_w2[l].astype(BF16), rw3, tm=tm_moe)
        sorted_rows = xs
        if last:
            x_all = _combine(pos3, x1, g2, ys, n_ctx=n_ctx, seq=seq, with_ctx=False, tm=tm)
        else:
            x_all, routed = x1, (pos.reshape(n_tok // tm_mxu, 1, tm_mxu), g2, ys)
    return x_all.reshape(batch, seq, d)
```
